```python
import jax
import jax.numpy as jnp
from jax import lax
import numpy as np

D_MODEL = 1024
BATCH = 8
SEQ = 4096
DEPTH = 1

ATT_HEADS = 8
ATT_KV_HEADS = 2
ATT_GROUP = ATT_HEADS // ATT_KV_HEADS
ATT_HEAD_DIM = 64
WINDOW = 128
ATT_BLOCK = 128
ROPE_DIM = ATT_HEAD_DIM // 4
ROPE_THETA = 500000.0

RET_HEADS = 4
RET_KEY_DIM = 128
RET_VAL_DIM = 256
RET_CHUNK = 128
RET_ROT_BASE = 10000.0

D_FF = 4 * D_MODEL

NORM_EPS = 1e-6
GN_EPS = 1e-6
NEG_INF = -1e30

ATT_Q_W = ATT_HEADS * ATT_HEAD_DIM
ATT_KV_W = ATT_KV_HEADS * ATT_HEAD_DIM
RET_QK_W = RET_HEADS * RET_KEY_DIM
RET_V_W = RET_HEADS * RET_VAL_DIM
IN_SPLITS = (ATT_Q_W, ATT_KV_W, ATT_KV_W, RET_QK_W, RET_QK_W, RET_V_W, RET_V_W, D_MODEL, D_MODEL)
IN_WIDTH = sum(IN_SPLITS)
IN_OFFSETS = tuple(int(v) for v in np.cumsum(IN_SPLITS)[:-1])

kernel_name = "hybrid_swa_sink_retention_gated_block"


def rmsnorm(x, gain):
    xf = x.astype(jnp.float32)
    y = xf * lax.rsqrt(jnp.mean(xf * xf, axis=-1, keepdims=True) + NORM_EPS)
    return (y * gain.astype(jnp.float32)).astype(x.dtype)


def rope_tables(seq_len, dim, theta):
    pos = jnp.arange(seq_len, dtype=jnp.float32)
    inv_freq = theta ** (-jnp.arange(0, dim, 2, dtype=jnp.float32) / dim)
    ang = pos[:, None] * inv_freq[None, :]
    return jnp.cos(ang)[:, None, :], jnp.sin(ang)[:, None, :]


def apply_rope(x, cos, sin):
    xf = x.astype(jnp.float32)
    x1, x2 = jnp.split(xf, 2, axis=-1)
    out = jnp.concatenate([x1 * cos - x2 * sin, x2 * cos + x1 * sin], axis=-1)
    return out.astype(x.dtype)


def partial_rope(x, cos, sin):
    return jnp.concatenate([apply_rope(x[..., :ROPE_DIM], cos, sin), x[..., ROPE_DIM:]], axis=-1)


def sliding_window_sink_attention(q, k, v, sinks):
    b, s, _, dh = q.shape
    c = ATT_BLOCK
    nb = s // c
    qb = q.reshape(b, nb, c, ATT_KV_HEADS, ATT_GROUP, dh)
    kb = k.reshape(b, nb, c, ATT_KV_HEADS, dh)
    vb = v.reshape(b, nb, c, ATT_KV_HEADS, dh)

    def with_prev(t):
        prev = jnp.concatenate([jnp.zeros_like(t[:, :1]), t[:, :-1]], axis=1)
        return jnp.concatenate([prev, t], axis=2)

    kw, vw = with_prev(kb), with_prev(vb)
    scores = jnp.einsum('bnqhgd,bnkhd->bnhgqk', qb, kw,
                        preferred_element_type=jnp.float32) * (dh ** -0.5)
    qi = jnp.arange(c)[:, None] + c
    kj = jnp.arange(2 * c)[None, :]
    delta = qi - kj
    in_window = (delta >= 0) & (delta < WINDOW)
    has_prev = (jnp.arange(nb) > 0)[:, None, None] | (kj >= c)[None]
    mask = in_window[None] & has_prev
    scores = jnp.where(mask[None, :, None, None], scores, NEG_INF)
    sink = sinks.astype(jnp.float32).reshape(ATT_KV_HEADS, ATT_GROUP)[None, None, :, :, None, None]
    sink = jnp.broadcast_to(sink, scores.shape[:-1] + (1,))
    probs = jax.nn.softmax(jnp.concatenate([scores, sink], axis=-1), axis=-1)[..., :-1]
    out = jnp.einsum('bnhgqk,bnkhd->bnqhgd', probs.astype(v.dtype), vw)
    return out.reshape(b, s, ATT_HEADS * dh)


def chunkwise_retention(q, k, v):
    b, s, h, dk = q.shape
    dv = v.shape[-1]
    c = RET_CHUNK
    nc = s // c
    log_gamma = jnp.log1p(-jnp.exp2(-5.0 - jnp.arange(h, dtype=jnp.float32)))
    idx = jnp.arange(c, dtype=jnp.float32)
    diff = idx[:, None] - idx[None, :]
    intra = jnp.where(diff >= 0, jnp.exp(jnp.maximum(diff, 0.0) * log_gamma[:, None, None]), 0.0)
    q_decay = jnp.exp((idx + 1.0)[None, :] * log_gamma[:, None])[..., None]
    k_decay = jnp.exp((c - 1.0 - idx)[None, :] * log_gamma[:, None])[..., None]
    chunk_decay = jnp.exp(c * log_gamma)[:, None, None]

    def to_chunks(t):
        return t.astype(jnp.float32).reshape(b, nc, c, h, t.shape[-1]).transpose(1, 0, 3, 2, 4)

    qc, kc, vc = to_chunks(q), to_chunks(k), to_chunks(v)

    def step(state, inp):
        qi, ki, vi = inp
        att = jnp.einsum('bhqd,bhkd->bhqk', qi, ki) * intra
        inner = jnp.einsum('bhqk,bhkv->bhqv', att, vi)
        cross = jnp.einsum('bhqd,bhdv->bhqv', qi * q_decay, state)
        new_state = state * chunk_decay + jnp.einsum('bhkd,bhkv->bhdv', ki * k_decay, vi)
        return new_state, inner + cross

    state0 = jnp.zeros((b, h, dk, dv), jnp.float32)
    _, out = lax.scan(step, state0, (qc, kc, vc))
    return out.transpose(1, 0, 3, 2, 4).reshape(b, s, h, dv)


def head_groupnorm(y, gain):
    b, s, h, dv = y.shape
    mu = jnp.mean(y, axis=-1, keepdims=True)
    var = jnp.mean(jnp.square(y - mu), axis=-1, keepdims=True)
    yn = (y - mu) * lax.rsqrt(var + GN_EPS)
    return yn.reshape(b, s, h * dv) * gain.astype(jnp.float32)


def mixer_block(xn, w_in, b_gates, attn_sinks, ret_gn_gain, w_att_up, w_ret_up, w_out, att_rope, ret_rope):
    b, s, _ = xn.shape
    proj = jnp.einsum('bsd,dn->bsn', xn, w_in)
    q_a, k_a, v_a, q_r, k_r, v_r, g_r, gate_a, gate_r = jnp.split(proj, IN_OFFSETS, axis=-1)

    q_a = partial_rope(q_a.reshape(b, s, ATT_HEADS, ATT_HEAD_DIM), *att_rope)
    k_a = partial_rope(k_a.reshape(b, s, ATT_KV_HEADS, ATT_HEAD_DIM), *att_rope)
    v_a = v_a.reshape(b, s, ATT_KV_HEADS, ATT_HEAD_DIM)
    y_a = sliding_window_sink_attention(q_a, k_a, v_a, attn_sinks) @ w_att_up

    q_r = apply_rope(q_r.reshape(b, s, RET_HEADS, RET_KEY_DIM), *ret_rope)
    k_r = apply_rope(k_r.reshape(b, s, RET_HEADS, RET_KEY_DIM), *ret_rope) * (RET_KEY_DIM ** -0.5)
    ret = chunkwise_retention(q_r, k_r, v_r.reshape(b, s, RET_HEADS, RET_VAL_DIM))
    ret = head_groupnorm(ret, ret_gn_gain)
    y_r = (jax.nn.silu(g_r.astype(jnp.float32)) * ret).astype(xn.dtype) @ w_ret_up

    bg = b_gates.astype(jnp.float32)
    ga = jax.nn.sigmoid(gate_a.astype(jnp.float32) + bg[:D_MODEL])
    gr = jax.nn.sigmoid(gate_r.astype(jnp.float32) + bg[D_MODEL:])
    merged = (ga * y_a.astype(jnp.float32) + gr * y_r.astype(jnp.float32)).astype(xn.dtype)
    return merged @ w_out


def squared_relu_mlp(xn, w_ff1, w_ff2):
    hdn = jnp.square(jax.nn.relu(xn @ w_ff1))
    return hdn @ w_ff2


def setup_inputs(seed: int = 0) -> dict:
    key = jax.random.key(seed)
    ks = jax.random.split(key, 14)
    f32 = jnp.float32

    def normal(k, shape, scale):
        return jax.random.normal(k, shape, f32) * scale

    def gain(k, shape):
        return 1.0 + 0.02 * jax.random.normal(k, shape, f32)

    return {
        'x': normal(ks[0], (BATCH, SEQ, D_MODEL), 1.0),
        'norm_mix_gain': gain(ks[1], (DEPTH, D_MODEL)),
        'w_in': normal(ks[2], (DEPTH, D_MODEL, IN_WIDTH), D_MODEL ** -0.5),
        'b_gates': normal(ks[3], (DEPTH, 2 * D_MODEL), 0.1),
        'attn_sinks': normal(ks[4], (DEPTH, ATT_HEADS), 0.5),
        'ret_gn_gain': gain(ks[5], (DEPTH, RET_V_W)),
        'w_att_up': normal(ks[6], (DEPTH, ATT_Q_W, D_MODEL), ATT_Q_W ** -0.5),
        'w_ret_up': normal(ks[7], (DEPTH, RET_V_W, D_MODEL), RET_V_W ** -0.5),
        'w_out': normal(ks[8], (DEPTH, D_MODEL, D_MODEL), D_MODEL ** -0.5),
        'norm_mlp_gain': gain(ks[9], (DEPTH, D_MODEL)),
        'w_ff1': normal(ks[10], (DEPTH, D_MODEL, D_FF), D_MODEL ** -0.5),
        'w_ff2': normal(ks[11], (DEPTH, D_FF, D_MODEL), D_FF ** -0.5),
        'norm_final_gain': gain(ks[12], (D_MODEL,)),
    }


def reference(x, norm_mix_gain, w_in, b_gates, attn_sinks, ret_gn_gain, w_att_up, w_ret_up, w_out,
              norm_mlp_gain, w_ff1, w_ff2, norm_final_gain):
    s = x.shape[1]
    att_rope = rope_tables(s, ROPE_DIM, ROPE_THETA)
    ret_rope = rope_tables(s, RET_KEY_DIM, RET_ROT_BASE)
    h = x
    for l in range(DEPTH):
        h = h + mixer_block(rmsnorm(h, norm_mix_gain[l]), w_in[l], b_gates[l], attn_sinks[l],
                            ret_gn_gain[l], w_att_up[l], w_ret_up[l], w_out[l], att_rope, ret_rope)
        h = h + squared_relu_mlp(rmsnorm(h, norm_mlp_gain[l]), w_ff1[l], w_ff2[l])
    return rmsnorm(h, norm_final_gain)
```

```python
import functools

import jax
import jax.numpy as jnp
import numpy as np
from jax import lax
from jax.experimental import pallas as pl
from jax.experimental.pallas import tpu as pltpu

D_MODEL = 1024
ATT_HEADS = 8
ATT_KV_HEADS = 2
ATT_GROUP = ATT_HEADS // ATT_KV_HEADS
ATT_HEAD_DIM = 64
WINDOW = 128
ATT_BLOCK = 128
ROPE_DIM = ATT_HEAD_DIM // 4
ROPE_THETA = 500000.0
RET_HEADS = 4
RET_KEY_DIM = 128
RET_VAL_DIM = 256
RET_CHUNK = 128
RET_ROT_BASE = 10000.0
D_FF = 4 * D_MODEL
NORM_EPS = 1e-6
GN_EPS = 1e-6
NEG_INF = -1e30

ATT_Q_W = ATT_HEADS * ATT_HEAD_DIM
ATT_KV_W = ATT_KV_HEADS * ATT_HEAD_DIM
RET_QK_W = RET_HEADS * RET_KEY_DIM
RET_V_W = RET_HEADS * RET_VAL_DIM
IN_SPLITS = (ATT_Q_W, ATT_KV_W, ATT_KV_W, RET_QK_W, RET_QK_W, RET_V_W, RET_V_W, D_MODEL, D_MODEL)
IN_WIDTH = sum(IN_SPLITS)
(OFF_QA, OFF_KA, OFF_VA, OFF_QR, OFF_KR, OFF_VR, OFF_GR, OFF_GA, OFF_GB) = (
    int(v) for v in np.concatenate([[0], np.cumsum(IN_SPLITS)[:-1]]))

LANES = 128
V7X_VMEM_LIMIT_BYTES = 56 * 1024 * 1024

PROJ_ROWS = 512
PROJ_COLS = 512
MIX_ROWS = 512
MIX_CHUNKS = MIX_ROWS // RET_CHUNK
FFN_ROWS = 512
FFN_COLS = 512

BF16 = jnp.bfloat16
F32 = jnp.float32

assert WINDOW == ATT_BLOCK == RET_CHUNK
assert RET_KEY_DIM == LANES and 2 * ATT_HEAD_DIM == LANES


def _rms_scale(x, gain):
    ms = jnp.mean(x * x, axis=-1, keepdims=True)
    return (x * lax.rsqrt(ms + NORM_EPS)) * gain


def _proj_kernel(x_ref, gain_ref, w_ref, bg_ref, ca_ref, s1_ref, s2_ref, cr_ref, sr_ref,
                 qa_ref, ka_ref, va_ref, qr_ref, kr_ref, vr_ref, sg_ref, ga_ref, gb_ref):
    xb = _rms_scale(x_ref[...], gain_ref[...]).astype(BF16)

    def proj(off, width):
        return jnp.dot(xb, w_ref[:, off:off + width], preferred_element_type=F32)

    ca, s1, s2 = ca_ref[...], s1_ref[...], s2_ref[...]
    cr, sr = cr_ref[...], sr_ref[...]

    def rope_att(z):
        return z * ca + pltpu.roll(z, LANES - ROPE_DIM // 2, 1) * s1 + pltpu.roll(z, ROPE_DIM // 2, 1) * s2

    def rope_ret(z):
        return z * cr + pltpu.roll(z, RET_KEY_DIM // 2, 1) * sr

    z = proj(OFF_QA, ATT_Q_W)
    for c in range(ATT_Q_W // LANES):
        sl = slice(c * LANES, (c + 1) * LANES)
        qa_ref[:, sl] = rope_att(z[:, sl]).astype(BF16)
    z = proj(OFF_KA, 2 * ATT_KV_W)
    ka_ref[...] = rope_att(z[:, :ATT_KV_W]).astype(BF16)
    va_ref[...] = z[:, ATT_KV_W:].astype(BF16)
    z = proj(OFF_QR, RET_QK_W)
    for h in range(RET_HEADS):
        sl = slice(h * RET_KEY_DIM, (h + 1) * RET_KEY_DIM)
        qr_ref[:, sl] = rope_ret(z[:, sl]).astype(BF16)
    z = proj(OFF_KR, RET_QK_W)
    for h in range(RET_HEADS):
        sl = slice(h * RET_KEY_DIM, (h + 1) * RET_KEY_DIM)
        kr_ref[:, sl] = (rope_ret(z[:, sl]) * (RET_KEY_DIM ** -0.5)).astype(BF16)
    for c in range(RET_V_W // PROJ_COLS):
        sl = slice(c * PROJ_COLS, (c + 1) * PROJ_COLS)
        vr_ref[:, sl] = proj(OFF_VR + c * PROJ_COLS, PROJ_COLS).astype(BF16)
    for c in range(RET_V_W // PROJ_COLS):
        sl = slice(c * PROJ_COLS, (c + 1) * PROJ_COLS)
        sg_ref[:, sl] = jax.nn.silu(proj(OFF_GR + c * PROJ_COLS, PROJ_COLS)).astype(BF16)
    for c in range(D_MODEL // PROJ_COLS):
        sl = slice(c * PROJ_COLS, (c + 1) * PROJ_COLS)
        g = proj(OFF_GA + c * PROJ_COLS, PROJ_COLS) + bg_ref[:, sl]
        ga_ref[:, sl] = jax.nn.sigmoid(g).astype(BF16)
    for c in range(D_MODEL // PROJ_COLS):
        sl = slice(c * PROJ_COLS, (c + 1) * PROJ_COLS)
        g = proj(OFF_GB + c * PROJ_COLS, PROJ_COLS) + bg_ref[:, D_MODEL + c * PROJ_COLS:D_MODEL + (c + 1) * PROJ_COLS]
        gb_ref[:, sl] = jax.nn.sigmoid(g).astype(BF16)


def _proj_call(x, gain, w_in, b_gates, tables):
    b, s, d = x.shape
    rows = PROJ_ROWS
    grid = (s // rows, b)

    def tok(width):
        return pl.BlockSpec((None, rows, width), lambda j, i: (i, j, 0))

    def const(shape):
        return pl.BlockSpec(shape, lambda j, i: (0,) * len(shape), pipeline_mode=pl.Buffered(1))

    table = pl.BlockSpec((rows, LANES), lambda j, i: (j, 0))
    widths = (ATT_Q_W, ATT_KV_W, ATT_KV_W, RET_QK_W, RET_QK_W, RET_V_W, RET_V_W, D_MODEL, D_MODEL)
    return pl.pallas_call(
        _proj_kernel,
        grid=grid,
        in_specs=[tok(d), const((1, d)), const((d, IN_WIDTH)), const((1, 2 * D_MODEL))] + [table] * 5,
        out_specs=[tok(w) for w in widths],
        out_shape=[jax.ShapeDtypeStruct((b, s, w), BF16) for w in widths],
        compiler_params=pltpu.CompilerParams(
            dimension_semantics=("arbitrary", "arbitrary"), vmem_limit_bytes=V7X_VMEM_LIMIT_BYTES),
        name="proj",
    )(x, gain, w_in, b_gates, *tables)


def _mix_kernel(sinks_ref, cdec_ref,
                qa_ref, ka_ref, va_ref, qr_ref, kr_ref, vr_ref, sg_ref, ga_ref, gb_ref, x_ref,
                gn_ref, intra_ref, qdec_ref, kdec_ref, watt_ref, wret_ref, wout_ref,
                h_ref,
                kext_ref, vext_ref, state_ref, att_ref, ret_ref):
    j = pl.program_id(1)

    @pl.when(j == 0)
    def _():
        kext_ref[:ATT_BLOCK, :] = jnp.zeros((ATT_BLOCK, ATT_KV_W), BF16)
        vext_ref[:ATT_BLOCK, :] = jnp.zeros((ATT_BLOCK, ATT_KV_W), BF16)
        state_ref[...] = jnp.zeros_like(state_ref)

    kext_ref[ATT_BLOCK:, :] = ka_ref[...]
    vext_ref[ATT_BLOCK:, :] = va_ref[...]

    row = lax.broadcasted_iota(jnp.int32, (ATT_BLOCK, 2 * ATT_BLOCK), 0)
    col = lax.broadcasted_iota(jnp.int32, (ATT_BLOCK, 2 * ATT_BLOCK), 1)
    band = (col > row) & (col <= row + WINDOW)

    def chunk(c, carry):
        r0 = pl.multiple_of(c * RET_CHUNK, RET_CHUNK)
        rows = pl.ds(r0, RET_CHUNK)

        lo = jnp.where((j == 0) & (c == 0), ATT_BLOCK, 0)
        mask = band & (col >= lo)
        q = qa_ref[rows, :]
        kc = kext_ref[pl.ds(r0, 2 * ATT_BLOCK), :]
        vc = vext_ref[pl.ds(r0, 2 * ATT_BLOCK), :]
        outs = []
        for hd in range(ATT_HEADS):
            kv = hd // ATT_GROUP
            qh = q[:, hd * ATT_HEAD_DIM:(hd + 1) * ATT_HEAD_DIM]
            kh = kc[:, kv * ATT_HEAD_DIM:(kv + 1) * ATT_HEAD_DIM]
            vh = vc[:, kv * ATT_HEAD_DIM:(kv + 1) * ATT_HEAD_DIM]
            sc = lax.dot_general(qh, kh, (((1,), (1,)), ((), ())), preferred_element_type=F32)
            sc = jnp.where(mask, sc * (ATT_HEAD_DIM ** -0.5), NEG_INF)
            sink = sinks_ref[hd]
            m = jnp.maximum(jnp.max(sc, axis=-1, keepdims=True), sink)
            e = jnp.exp(sc - m)
            den = jnp.sum(e, axis=-1, keepdims=True) + jnp.exp(sink - m)
            o = jnp.dot(e.astype(BF16), vh, preferred_element_type=F32)
            outs.append(o / den)
        att_ref[rows, :] = jnp.concatenate(outs, axis=-1).astype(BF16)

        for h in range(RET_HEADS):
            ksl = slice(h * RET_KEY_DIM, (h + 1) * RET_KEY_DIM)
            vsl = slice(h * RET_VAL_DIM, (h + 1) * RET_VAL_DIM)
            qh = qr_ref[rows, ksl]
            kh = kr_ref[rows, ksl]
            vh = vr_ref[rows, vsl]
            st = state_ref[h]
            att = lax.dot_general(qh, kh, (((1,), (1,)), ((), ())), preferred_element_type=F32)
            att = (att * intra_ref[h]).astype(BF16)
            qd = (qh.astype(F32) * qdec_ref[h]).astype(BF16)
            kd = (kh.astype(F32) * kdec_ref[h]).astype(BF16)
            out = (jnp.dot(att, vh, preferred_element_type=F32)
                   + jnp.dot(qd, st.astype(BF16), preferred_element_type=F32))
            upd = lax.dot_general(kd, vh, (((0,), (0,)), ((), ())), preferred_element_type=F32)
            state_ref[h] = st * cdec_ref[h] + upd
            mu = jnp.mean(out, axis=-1, keepdims=True)
            dev = out - mu
            var = jnp.mean(dev * dev, axis=-1, keepdims=True)
            yn = dev * lax.rsqrt(var + GN_EPS) * gn_ref[:, vsl]
            ret_ref[rows, vsl] = (sg_ref[rows, vsl].astype(F32) * yn).astype(BF16)
        return carry

    lax.fori_loop(0, MIX_CHUNKS, chunk, 0)

    kext_ref[:ATT_BLOCK, :] = kext_ref[MIX_ROWS:, :]
    vext_ref[:ATT_BLOCK, :] = vext_ref[MIX_ROWS:, :]

    ya = jnp.dot(att_ref[...], watt_ref[...], preferred_element_type=F32)
    yr = jnp.dot(ret_ref[...], wret_ref[...], preferred_element_type=F32)
    merged = (ga_ref[...].astype(F32) * ya + gb_ref[...].astype(F32) * yr).astype(BF16)
    h_ref[...] = x_ref[...] + jnp.dot(merged, wout_ref[...], preferred_element_type=F32)


def _mix_call(proj_out, x, sinks, cdec, gn_gain, intra, qdec, kdec, w_att_up, w_ret_up, w_out):
    b, s, d = x.shape
    rows = MIX_ROWS
    grid = (b, s // rows)

    def tok(width):
        return pl.BlockSpec((None, rows, width), lambda i, j: (i, j, 0))

    def const(shape):
        return pl.BlockSpec(shape, lambda i, j: (0,) * len(shape), pipeline_mode=pl.Buffered(1))

    smem = pl.BlockSpec(memory_space=pltpu.SMEM)
    widths = (ATT_Q_W, ATT_KV_W, ATT_KV_W, RET_QK_W, RET_QK_W, RET_V_W, RET_V_W, D_MODEL, D_MODEL)
    dec = (RET_HEADS, RET_CHUNK, RET_KEY_DIM)
    return pl.pallas_call(
        _mix_kernel,
        grid=grid,
        in_specs=[smem, smem] + [tok(w) for w in widths] + [tok(d)]
        + [const((1, RET_V_W)), const((RET_HEADS, RET_CHUNK, RET_CHUNK)), const(dec), const(dec),
           const((ATT_Q_W, d)), const((RET_V_W, d)), const((d, d))],
        out_specs=tok(d),
        out_shape=jax.ShapeDtypeStruct((b, s, d), F32),
        scratch_shapes=[
            pltpu.VMEM((ATT_BLOCK + rows, ATT_KV_W), BF16),
            pltpu.VMEM((ATT_BLOCK + rows, ATT_KV_W), BF16),
            pltpu.VMEM((RET_HEADS, RET_KEY_DIM, RET_VAL_DIM), F32),
            pltpu.VMEM((rows, ATT_Q_W), BF16),
            pltpu.VMEM((rows, RET_V_W), BF16),
        ],
        compiler_params=pltpu.CompilerParams(
            dimension_semantics=("arbitrary", "arbitrary"), vmem_limit_bytes=V7X_VMEM_LIMIT_BYTES),
        name="mix",
    )(sinks, cdec, *proj_out, x, gn_gain, intra, qdec, kdec, w_att_up, w_ret_up, w_out)


def _ffn_kernel(h_ref, gain_ref, w1_ref, w2_ref, fgain_ref, o_ref, *, final_norm):
    h = h_ref[...]
    xb = _rms_scale(h, gain_ref[...]).astype(BF16)
    acc = jnp.zeros(h.shape, F32)
    for c in range(D_FF // FFN_COLS):
        sl = slice(c * FFN_COLS, (c + 1) * FFN_COLS)
        a = jnp.maximum(jnp.dot(xb, w1_ref[:, sl], preferred_element_type=F32), 0.0)
        acc = acc + jnp.dot((a * a).astype(BF16), w2_ref[sl, :], preferred_element_type=F32)
    y = h + acc
    if final_norm:
        y = _rms_scale(y, fgain_ref[...])
    o_ref[...] = y


def _ffn_call(h, gain, w1, w2, fgain, final_norm):
    b, s, d = h.shape
    rows = FFN_ROWS
    grid = (b, s // rows)

    def const(shape):
        return pl.BlockSpec(shape, lambda i, j: (0,) * len(shape), pipeline_mode=pl.Buffered(1))

    tok = pl.BlockSpec((None, rows, d), lambda i, j: (i, j, 0))
    return pl.pallas_call(
        functools.partial(_ffn_kernel, final_norm=final_norm),
        grid=grid,
        in_specs=[tok, const((1, d)), const((d, D_FF)), const((D_FF, d)), const((1, d))],
        out_specs=tok,
        out_shape=jax.ShapeDtypeStruct((b, s, d), F32),
        compiler_params=pltpu.CompilerParams(
            dimension_semantics=("arbitrary", "arbitrary"), vmem_limit_bytes=V7X_VMEM_LIMIT_BYTES),
        name="ffn",
    )(h, gain, w1, w2, fgain)


def _rope_tables(seq_len):
    pos = jnp.arange(seq_len, dtype=F32)

    def cos_sin(dim, theta):
        inv_freq = theta ** (-jnp.arange(0, dim, 2, dtype=F32) / dim)
        ang = pos[:, None] * inv_freq[None, :]
        return jnp.cos(ang), jnp.sin(ang)

    cos, sin = cos_sin(ROPE_DIM, ROPE_THETA)
    pad = ATT_HEAD_DIM - ROPE_DIM
    zeros_half = jnp.zeros_like(sin)
    one_head = lambda parts: jnp.tile(jnp.concatenate(parts, axis=-1), (1, LANES // ATT_HEAD_DIM))
    ca = one_head([cos, cos, jnp.ones((seq_len, pad), F32)])
    s1 = one_head([-sin, zeros_half, jnp.zeros((seq_len, pad), F32)])
    s2 = one_head([zeros_half, sin, jnp.zeros((seq_len, pad), F32)])
    cos, sin = cos_sin(RET_KEY_DIM, RET_ROT_BASE)
    cr = jnp.concatenate([cos, cos], axis=-1)
    sr = jnp.concatenate([-sin, sin], axis=-1)
    return ca, s1, s2, cr, sr


def _decay_tables():
    h, c = RET_HEADS, RET_CHUNK
    log_gamma = jnp.log1p(-jnp.exp2(-5.0 - jnp.arange(h, dtype=F32)))
    idx = jnp.arange(c, dtype=F32)
    diff = idx[:, None] - idx[None, :]
    intra = jnp.where(diff >= 0, jnp.exp(jnp.maximum(diff, 0.0) * log_gamma[:, None, None]), 0.0)
    q_decay = jnp.exp((idx + 1.0)[None, :] * log_gamma[:, None])[..., None]
    k_decay = jnp.exp((c - 1.0 - idx)[None, :] * log_gamma[:, None])[..., None]
    chunk_decay = jnp.exp(c * log_gamma)
    qdec = jnp.broadcast_to(q_decay, (h, c, RET_KEY_DIM))
    kdec = jnp.broadcast_to(k_decay, (h, c, RET_KEY_DIM))
    return intra, qdec, kdec, chunk_decay


def kernel(x, norm_mix_gain, w_in, b_gates, attn_sinks, ret_gn_gain, w_att_up, w_ret_up, w_out,
           norm_mlp_gain, w_ff1, w_ff2, norm_final_gain):
    b, s, d = x.shape
    depth = w_in.shape[0]
    assert d == D_MODEL and s % MIX_ROWS == 0 and s % PROJ_ROWS == 0 and s % FFN_ROWS == 0
    tables = _rope_tables(s)
    intra, qdec, kdec, cdec = _decay_tables()
    row = lambda v: v.reshape(1, -1).astype(F32)
    h = x
    for l in range(depth):
        proj_out = _proj_call(h, row(norm_mix_gain[l]), w_in[l].astype(BF16), row(b_gates[l]), tables)
        h = _mix_call(proj_out, h, attn_sinks[l].astype(F32), cdec, row(ret_gn_gain[l]), intra, qdec, kdec,
                      w_att_up[l].astype(BF16), w_ret_up[l].astype(BF16), w_out[l].astype(BF16))
        h = _ffn_call(h, row(norm_mlp_gain[l]), w_ff1[l].astype(BF16), w_ff2[l].astype(BF16),
                      row(norm_final_gain), final_norm=(l == depth - 1))
    return h
```

```python
import functools

import jax
import jax.numpy as jnp
import numpy as np
from jax import lax
from jax.experimental import pallas as pl
from jax.experimental.pallas import tpu as pltpu

D_MODEL = 1024
ATT_HEADS = 8
ATT_KV_HEADS = 2
ATT_GROUP = ATT_HEADS // ATT_KV_HEADS
ATT_HEAD_DIM = 64
WINDOW = 128
ATT_BLOCK = 128
ROPE_DIM = ATT_HEAD_DIM // 4
ROPE_HALF = ROPE_DIM // 2
ROPE_THETA = 500000.0
RET_HEADS = 4
RET_KEY_DIM = 128
RET_VAL_DIM = 256
RET_CHUNK = 128
RET_ROT_BASE = 10000.0
D_FF = 4 * D_MODEL
NORM_EPS = 1e-6
GN_EPS = 1e-6
NEG_INF = -1e30

ATT_Q_W = ATT_HEADS * ATT_HEAD_DIM
ATT_KV_W = ATT_KV_HEADS * ATT_HEAD_DIM
RET_QK_W = RET_HEADS * RET_KEY_DIM
RET_V_W = RET_HEADS * RET_VAL_DIM
IN_SPLITS = (ATT_Q_W, ATT_KV_W, ATT_KV_W, RET_QK_W, RET_QK_W, RET_V_W, RET_V_W, D_MODEL, D_MODEL)
IN_WIDTH = sum(IN_SPLITS)
(OFF_QA, OFF_KA, OFF_VA, OFF_QR, OFF_KR, OFF_VR, OFF_GR, OFF_GA, OFF_GB) = (
    int(v) for v in np.concatenate([[0], np.cumsum(IN_SPLITS)[:-1]]))

LANES = 128
SUBLANES = 8
V7X_VMEM_LIMIT_BYTES = 56 * 1024 * 1024

BLOCK = 128
PROJ_ROWS = 512
PROJ_COLS = 512
MIX_ROWS = 512
FFN_ROWS = 512
FFN_COLS = 512

BF16 = jnp.bfloat16
F32 = jnp.float32

assert WINDOW == ATT_BLOCK == RET_CHUNK == BLOCK == LANES
assert RET_KEY_DIM == LANES and ATT_KV_W == LANES and ROPE_HALF == SUBLANES
assert PROJ_ROWS % BLOCK == 0 and MIX_ROWS % BLOCK == 0

NT = (((1,), (1,)), ((), ()))


def _rms_scale(x, gain):
    ms = jnp.mean(x * x, axis=-1, keepdims=True)
    return (x * lax.rsqrt(ms + NORM_EPS)) * gain


def _proj_kernel(x_ref, gain_ref, w_ref, wqt_ref, wvt_ref, bg_ref,
                 cat_ref, sat_ref, ca_ref, s1_ref, s2_ref, cr_ref, sr_ref, qdec_ref, kdec_ref,
                 qat_ref, ka_ref, vat_ref, qr_ref, qdr_ref, kr_ref, kdt_ref, vr_ref, sg_ref, ga_ref, gb_ref):
    rows = x_ref.shape[0]
    nblk = rows // BLOCK
    xb = _rms_scale(x_ref[...], gain_ref[...]).astype(BF16)

    def proj(off, width):
        return jnp.dot(xb, w_ref[:, off:off + width], preferred_element_type=F32)

    def store_blocks(ref, feat, val):
        for c in range(nblk):
            ref[c, feat, :] = val[:, c * BLOCK:(c + 1) * BLOCK].astype(BF16)

    zt = lax.dot_general(wqt_ref[...], xb, NT, preferred_element_type=F32)
    cat, sat = cat_ref[...], sat_ref[...]
    for hd in range(ATT_HEADS):
        b0 = hd * ATT_HEAD_DIM
        x1, x2 = zt[b0:b0 + ROPE_HALF], zt[b0 + ROPE_HALF:b0 + ROPE_DIM]
        head = jnp.concatenate([x1 * cat - x2 * sat, x2 * cat + x1 * sat, zt[b0 + ROPE_DIM:b0 + ATT_HEAD_DIM]], axis=0)
        store_blocks(qat_ref, slice(b0, b0 + ATT_HEAD_DIM), head * (ATT_HEAD_DIM ** -0.5))
    store_blocks(vat_ref, slice(None), lax.dot_general(wvt_ref[...], xb, NT, preferred_element_type=F32))
    z = proj(OFF_KA, ATT_KV_W)
    ka_ref[...] = (z * ca_ref[...] + pltpu.roll(z, LANES - ROPE_HALF, 1) * s1_ref[...]
                   + pltpu.roll(z, ROPE_HALF, 1) * s2_ref[...]).astype(BF16)

    cr, sr = cr_ref[...], sr_ref[...]

    def rope_ret(zh):
        return zh * cr + pltpu.roll(zh, RET_KEY_DIM // 2, 1) * sr

    z = proj(OFF_QR, RET_QK_W)
    for h in range(RET_HEADS):
        sl = slice(h * RET_KEY_DIM, (h + 1) * RET_KEY_DIM)
        q = rope_ret(z[:, sl])
        qr_ref[:, sl] = q.astype(BF16)
        qdr_ref[:, sl] = (q * qdec_ref[:, sl]).astype(BF16)
    z = proj(OFF_KR, RET_QK_W)
    for h in range(RET_HEADS):
        sl = slice(h * RET_KEY_DIM, (h + 1) * RET_KEY_DIM)
        k = rope_ret(z[:, sl]) * (RET_KEY_DIM ** -0.5)
        kr_ref[:, sl] = k.astype(BF16)
        store_blocks(kdt_ref, sl, (k * kdec_ref[:, sl]).T)
    for c in range(RET_V_W // PROJ_COLS):
        sl = slice(c * PROJ_COLS, (c + 1) * PROJ_COLS)
        vr_ref[:, sl] = proj(OFF_VR + c * PROJ_COLS, PROJ_COLS).astype(BF16)
    for c in range(RET_V_W // PROJ_COLS):
        sl = slice(c * PROJ_COLS, (c + 1) * PROJ_COLS)
        sg_ref[:, sl] = jax.nn.silu(proj(OFF_GR + c * PROJ_COLS, PROJ_COLS)).astype(BF16)
    for c in range(D_MODEL // PROJ_COLS):
        sl = slice(c * PROJ_COLS, (c + 1) * PROJ_COLS)
        g = proj(OFF_GA + c * PROJ_COLS, PROJ_COLS) + bg_ref[:, sl]
        ga_ref[:, sl] = jax.nn.sigmoid(g).astype(BF16)
    for c in range(D_MODEL // PROJ_COLS):
        sl = slice(c * PROJ_COLS, (c + 1) * PROJ_COLS)
        g = proj(OFF_GB + c * PROJ_COLS, PROJ_COLS) + bg_ref[:, D_MODEL + c * PROJ_COLS:D_MODEL + (c + 1) * PROJ_COLS]
        gb_ref[:, sl] = jax.nn.sigmoid(g).astype(BF16)


def _proj_call(x, gain, w_in, wq_t, wv_t, b_gates, tables):
    b, s, d = x.shape
    rows = PROJ_ROWS
    nblk = rows // BLOCK
    grid = (s // rows, b)

    def tok(width):
        return pl.BlockSpec((None, rows, width), lambda j, i: (i, j, 0))

    def tok_t(feat):
        return pl.BlockSpec((None, nblk, feat, BLOCK), lambda j, i: (i, j, 0, 0))

    def const(shape):
        return pl.BlockSpec(shape, lambda j, i: (0,) * len(shape), pipeline_mode=pl.Buffered(1))

    pos_t = pl.BlockSpec((ROPE_HALF, rows), lambda j, i: (0, j))
    pos = pl.BlockSpec((rows, LANES), lambda j, i: (j, 0))
    tok_shape = lambda w: jax.ShapeDtypeStruct((b, s, w), BF16)
    tok_t_shape = lambda f: jax.ShapeDtypeStruct((b, s // BLOCK, f, BLOCK), BF16)
    return pl.pallas_call(
        _proj_kernel,
        grid=grid,
        in_specs=[tok(d), const((1, d)), const((d, IN_WIDTH)), const((ATT_Q_W, d)), const((ATT_KV_W, d)),
                  const((1, 2 * D_MODEL)), pos_t, pos_t, pos, pos, pos, pos, pos,
                  const((rows, RET_QK_W)), const((rows, RET_QK_W))],
        out_specs=[tok_t(ATT_Q_W), tok(ATT_KV_W), tok_t(ATT_KV_W), tok(RET_QK_W), tok(RET_QK_W), tok(RET_QK_W),
                   tok_t(RET_QK_W), tok(RET_V_W), tok(RET_V_W), tok(D_MODEL), tok(D_MODEL)],
        out_shape=[tok_t_shape(ATT_Q_W), tok_shape(ATT_KV_W), tok_t_shape(ATT_KV_W), tok_shape(RET_QK_W),
                   tok_shape(RET_QK_W), tok_shape(RET_QK_W), tok_t_shape(RET_QK_W), tok_shape(RET_V_W),
                   tok_shape(RET_V_W), tok_shape(D_MODEL), tok_shape(D_MODEL)],
        compiler_params=pltpu.CompilerParams(
            dimension_semantics=("arbitrary", "arbitrary"), vmem_limit_bytes=V7X_VMEM_LIMIT_BYTES),
        name="proj",
    )(x, gain, w_in, wq_t, wv_t, b_gates, *tables)


def _mix_kernel(cdec_ref,
                qat_ref, ka_ref, vat_ref, qr_ref, qdr_ref, kr_ref, kdt_ref, vr_ref, sg_ref, ga_ref, gb_ref, x_ref,
                sink_ref, gn_ref, intra_ref, watt_ref, wret_ref, wout_ref,
                h_ref,
                kext_ref, vext_ref, state_ref, att_ref, ret_ref):
    j = pl.program_id(1)
    nblk = x_ref.shape[0] // BLOCK

    @pl.when(j == 0)
    def _():
        kext_ref[:BLOCK, :] = jnp.zeros((BLOCK, ATT_KV_W), BF16)
        vext_ref[0] = jnp.zeros((ATT_KV_W, BLOCK), BF16)
        state_ref[...] = jnp.zeros_like(state_ref)

    kext_ref[BLOCK:, :] = ka_ref[...]
    vext_ref[1:] = vat_ref[...]

    t_idx = lax.broadcasted_iota(jnp.int32, (BLOCK, ATT_GROUP * BLOCK), 0)
    i_idx = lax.broadcasted_iota(jnp.int32, (BLOCK, ATT_GROUP * BLOCK), 1) & (BLOCK - 1)
    from_prev = t_idx > i_idx
    zeros_q = jnp.zeros((ATT_HEAD_DIM, ATT_GROUP * BLOCK), BF16)

    ksl = [slice(h * RET_KEY_DIM, (h + 1) * RET_KEY_DIM) for h in range(RET_HEADS)]
    vsl = [slice(h * RET_VAL_DIM, (h + 1) * RET_VAL_DIM) for h in range(RET_HEADS)]
    rows_of = lambda c: pl.ds(c * BLOCK, BLOCK)

    def first_matmuls(c):
        rows = rows_of(c)
        qt = qat_ref[c]
        kc = kext_ref[pl.ds(c * BLOCK, 2 * BLOCK), :]
        scores = []
        for kv in range(ATT_KV_HEADS):
            qg = jnp.concatenate(
                [qt[(kv * ATT_GROUP + g) * ATT_HEAD_DIM:(kv * ATT_GROUP + g + 1) * ATT_HEAD_DIM] for g in range(ATT_GROUP)],
                axis=1)
            rhs = jnp.concatenate([qg, zeros_q] if kv == 0 else [zeros_q, qg], axis=0)
            scores.append(jnp.dot(kc, rhs, preferred_element_type=F32))
        att = [lax.dot_general(qr_ref[rows, ksl[h]], kr_ref[rows, ksl[h]], NT, preferred_element_type=F32)
               for h in range(RET_HEADS)]
        upd = [jnp.dot(kdt_ref[c, ksl[h], :], vr_ref[rows, vsl[h]], preferred_element_type=F32)
               for h in range(RET_HEADS)]
        return scores, att, upd

    def softmax_and_decay(c, first, state):
        scores, att, upd = first
        prev_bias = jnp.where((j == 0) & (c == 0), NEG_INF, 0.0).astype(F32)
        probs = []
        for kv in range(ATT_KV_HEADS):
            sc = scores[kv]
            f = jnp.where(from_prev, sc[:BLOCK] + prev_bias, sc[BLOCK:])
            sink = sink_ref[kv]
            m = jnp.maximum(jnp.max(f, axis=0, keepdims=True), sink)
            e = jnp.exp(f - m)
            den = jnp.sum(e, axis=0, keepdims=True) + jnp.exp(sink - m)
            p = e * (1.0 / den)
            probs.append(jnp.concatenate([jnp.where(from_prev, p, 0.0), jnp.where(from_prev, 0.0, p)],
                                         axis=0).astype(BF16))
        att_b = [(att[h] * intra_ref[h]).astype(BF16) for h in range(RET_HEADS)]
        state_b = [state[h].astype(BF16) for h in range(RET_HEADS)]
        new_state = [state[h] * cdec_ref[h] + upd[h] for h in range(RET_HEADS)]
        return (probs, att_b, state_b), new_state

    def second_matmuls(c, ops):
        probs, att_b, state_b = ops
        rows = rows_of(c)
        vt_prev, vt_cur = vext_ref[c], vext_ref[c + 1]
        outs = []
        for kv in range(ATT_KV_HEADS):
            hs = slice(kv * ATT_HEAD_DIM, (kv + 1) * ATT_HEAD_DIM)
            vt = jnp.concatenate([vt_prev[hs], vt_cur[hs]], axis=1)
            outs.append(jnp.dot(vt, probs[kv], preferred_element_type=F32))
        ret = [jnp.dot(jnp.concatenate([att_b[h], qdr_ref[rows, ksl[h]]], axis=1),
                       jnp.concatenate([vr_ref[rows, vsl[h]], state_b[h]], axis=0), preferred_element_type=F32)
               for h in range(RET_HEADS)]
        return outs, ret

    def finish(c, second):
        outs, ret = second
        rows = rows_of(c)
        ot = jnp.concatenate(outs, axis=0)
        for g in range(ATT_GROUP):
            att_ref[rows, g * LANES:(g + 1) * LANES] = ot[:, g * BLOCK:(g + 1) * BLOCK].T.astype(BF16)
        for h in range(RET_HEADS):
            out = ret[h]
            mu = jnp.mean(out, axis=-1, keepdims=True)
            dev = out - mu
            var = jnp.mean(dev * dev, axis=-1, keepdims=True)
            yn = dev * lax.rsqrt(var + GN_EPS) * gn_ref[:, vsl[h]]
            ret_ref[rows, vsl[h]] = (sg_ref[rows, vsl[h]].astype(F32) * yn).astype(BF16)

    state = [state_ref[h] for h in range(RET_HEADS)]
    first = first_matmuls(0)
    for c in range(nblk):
        nxt = first_matmuls(c + 1) if c + 1 < nblk else None
        ops, state = softmax_and_decay(c, first, state)
        finish(c, second_matmuls(c, ops))
        first = nxt
    for h in range(RET_HEADS):
        state_ref[h] = state[h]

    kext_ref[:BLOCK, :] = kext_ref[nblk * BLOCK:, :]
    vext_ref[0] = vext_ref[nblk]

    ya = jnp.dot(att_ref[...], watt_ref[...], preferred_element_type=F32)
    yr = jnp.dot(ret_ref[...], wret_ref[...], preferred_element_type=F32)
    merged = (ga_ref[...].astype(F32) * ya + gb_ref[...].astype(F32) * yr).astype(BF16)
    h_ref[...] = x_ref[...] + jnp.dot(merged, wout_ref[...], preferred_element_type=F32)


def _mix_call(proj_out, x, cdec, sinks, gn_gain, intra, w_att_up, w_ret_up, w_out):
    b, s, d = x.shape
    rows = MIX_ROWS
    nblk = rows // BLOCK
    grid = (b, s // rows)

    def tok(width):
        return pl.BlockSpec((None, rows, width), lambda i, j: (i, j, 0))

    def tok_t(feat):
        return pl.BlockSpec((None, nblk, feat, BLOCK), lambda i, j: (i, j, 0, 0))

    def const(shape):
        return pl.BlockSpec(shape, lambda i, j: (0,) * len(shape), pipeline_mode=pl.Buffered(1))

    smem = pl.BlockSpec(memory_space=pltpu.SMEM)
    return pl.pallas_call(
        _mix_kernel,
        grid=grid,
        in_specs=[smem,
                  tok_t(ATT_Q_W), tok(ATT_KV_W), tok_t(ATT_KV_W), tok(RET_QK_W), tok(RET_QK_W), tok(RET_QK_W),
                  tok_t(RET_QK_W), tok(RET_V_W), tok(RET_V_W), tok(D_MODEL), tok(D_MODEL), tok(d),
                  const((ATT_KV_HEADS, 1, ATT_GROUP * BLOCK)), const((1, RET_V_W)),
                  const((RET_HEADS, RET_CHUNK, RET_CHUNK)),
                  const((ATT_Q_W, d)), const((RET_V_W, d)), const((d, d))],
        out_specs=tok(d),
        out_shape=jax.ShapeDtypeStruct((b, s, d), F32),
        scratch_shapes=[
            pltpu.VMEM((BLOCK + rows, ATT_KV_W), BF16),
            pltpu.VMEM((1 + nblk, ATT_KV_W, BLOCK), BF16),
            pltpu.VMEM((RET_HEADS, RET_KEY_DIM, RET_VAL_DIM), F32),
            pltpu.VMEM((rows, ATT_Q_W), BF16),
            pltpu.VMEM((rows, RET_V_W), BF16),
        ],
        compiler_params=pltpu.CompilerParams(
            dimension_semantics=("arbitrary", "arbitrary"), vmem_limit_bytes=V7X_VMEM_LIMIT_BYTES),
        name="mix",
    )(cdec, *proj_out, x, sinks, gn_gain, intra, w_att_up, w_ret_up, w_out)


def _ffn_kernel(h_ref, gain_ref, w1_ref, w2_ref, fgain_ref, o_ref, *, final_norm):
    h = h_ref[...]
    xb = _rms_scale(h, gain_ref[...]).astype(BF16)
    acc = jnp.zeros(h.shape, F32)
    for c in range(D_FF // FFN_COLS):
        sl = slice(c * FFN_COLS, (c + 1) * FFN_COLS)
        a = jnp.maximum(jnp.dot(xb, w1_ref[:, sl], preferred_element_type=F32), 0.0)
        acc = acc + jnp.dot((a * a).astype(BF16), w2_ref[sl, :], preferred_element_type=F32)
    y = h + acc
    if final_norm:
        y = _rms_scale(y, fgain_ref[...])
    o_ref[...] = y


def _ffn_call(h, gain, w1, w2, fgain, final_norm):
    b, s, d = h.shape
    rows = FFN_ROWS
    grid = (b, s // rows)

    def const(shape):
        return pl.BlockSpec(shape, lambda i, j: (0,) * len(shape), pipeline_mode=pl.Buffered(1))

    tok = pl.BlockSpec((None, rows, d), lambda i, j: (i, j, 0))
    return pl.pallas_call(
        functools.partial(_ffn_kernel, final_norm=final_norm),
        grid=grid,
        in_specs=[tok, const((1, d)), const((d, D_FF)), const((D_FF, d)), const((1, d))],
        out_specs=tok,
        out_shape=jax.ShapeDtypeStruct((b, s, d), F32),
        compiler_params=pltpu.CompilerParams(
            dimension_semantics=("arbitrary", "arbitrary"), vmem_limit_bytes=V7X_VMEM_LIMIT_BYTES),
        name="ffn",
    )(h, gain, w1, w2, fgain)


def _decay_terms():
    h, c = RET_HEADS, RET_CHUNK
    log_gamma = jnp.log1p(-jnp.exp2(-5.0 - jnp.arange(h, dtype=F32)))
    idx = jnp.arange(c, dtype=F32)
    diff = idx[:, None] - idx[None, :]
    intra = jnp.where(diff >= 0, jnp.exp(jnp.maximum(diff, 0.0) * log_gamma[:, None, None]), 0.0)
    q_decay = jnp.exp((idx + 1.0)[None, :] * log_gamma[:, None])
    k_decay = jnp.exp((c - 1.0 - idx)[None, :] * log_gamma[:, None])
    chunk_decay = jnp.exp(c * log_gamma)
    return intra, q_decay, k_decay, chunk_decay


def _position_tables(seq_len, q_decay, k_decay):
    pos = jnp.arange(seq_len, dtype=F32)

    def cos_sin(dim, theta):
        inv_freq = theta ** (-jnp.arange(0, dim, 2, dtype=F32) / dim)
        ang = pos[:, None] * inv_freq[None, :]
        return jnp.cos(ang), jnp.sin(ang)

    cos, sin = cos_sin(ROPE_DIM, ROPE_THETA)
    pad = ATT_HEAD_DIM - ROPE_DIM
    zeros_half = jnp.zeros_like(sin)
    one_head = lambda parts: jnp.tile(jnp.concatenate(parts, axis=-1), (1, LANES // ATT_HEAD_DIM))
    ca = one_head([cos, cos, jnp.ones((seq_len, pad), F32)])
    s1 = one_head([-sin, zeros_half, jnp.zeros((seq_len, pad), F32)])
    s2 = one_head([zeros_half, sin, jnp.zeros((seq_len, pad), F32)])
    cos_r, sin_r = cos_sin(RET_KEY_DIM, RET_ROT_BASE)
    cr = jnp.concatenate([cos_r, cos_r], axis=-1)
    sr = jnp.concatenate([-sin_r, sin_r], axis=-1)

    def per_row(decay):
        t = jnp.repeat(decay.T, RET_KEY_DIM, axis=1)
        return jnp.tile(t, (PROJ_ROWS // RET_CHUNK, 1))

    return cos.T, sin.T, ca, s1, s2, cr, sr, per_row(q_decay), per_row(k_decay)


def kernel(x, norm_mix_gain, w_in, b_gates, attn_sinks, ret_gn_gain, w_att_up, w_ret_up, w_out,
           norm_mlp_gain, w_ff1, w_ff2, norm_final_gain):
    b, s, d = x.shape
    depth = w_in.shape[0]
    assert d == D_MODEL and s % MIX_ROWS == 0 and s % PROJ_ROWS == 0 and s % FFN_ROWS == 0
    intra, q_decay, k_decay, cdec = _decay_terms()
    tables = _position_tables(s, q_decay, k_decay)
    row = lambda v: v.reshape(1, -1).astype(F32)
    att_perm = np.arange(ATT_Q_W).reshape(ATT_KV_HEADS, ATT_GROUP, ATT_HEAD_DIM).transpose(1, 0, 2).reshape(-1)
    h = x
    for l in range(depth):
        w = w_in[l].astype(BF16)
        proj_out = _proj_call(h, row(norm_mix_gain[l]), w, w[:, OFF_QA:OFF_QA + ATT_Q_W].T,
                              w[:, OFF_VA:OFF_VA + ATT_KV_W].T, row(b_gates[l]), tables)
        sinks = jnp.repeat(attn_sinks[l].astype(F32).reshape(ATT_KV_HEADS, 1, ATT_GROUP), BLOCK, axis=-1)
        h = _mix_call(proj_out, h, cdec, sinks, row(ret_gn_gain[l]), intra,
                      w_att_up[l][att_perm].astype(BF16), w_ret_up[l].astype(BF16), w_out[l].astype(BF16))
        h = _ffn_call(h, row(norm_mlp_gain[l]), w_ff1[l].astype(BF16), w_ff2[l].astype(BF16),
                      row(norm_final_gain), final_norm=(l == depth - 1))
    return h
```

```python
import functools

import jax
import jax.numpy as jnp
import numpy as np
from jax import lax
from jax.experimental import pallas as pl
from jax.experimental.pallas import tpu as pltpu

D_MODEL = 1024
ATT_HEADS = 8
ATT_KV_HEADS = 2
ATT_GROUP = ATT_HEADS // ATT_KV_HEADS
ATT_HEAD_DIM = 64
WINDOW = 128
ATT_BLOCK = 128
ROPE_DIM = ATT_HEAD_DIM // 4
ROPE_HALF = ROPE_DIM // 2
ROPE_THETA = 500000.0
RET_HEADS = 4
RET_KEY_DIM = 128
RET_VAL_DIM = 256
RET_CHUNK = 128
RET_ROT_BASE = 10000.0
D_FF = 4 * D_MODEL
NORM_EPS = 1e-6
GN_EPS = 1e-6
NEG_INF = -1e30

ATT_Q_W = ATT_HEADS * ATT_HEAD_DIM
ATT_KV_W = ATT_KV_HEADS * ATT_HEAD_DIM
RET_QK_W = RET_HEADS * RET_KEY_DIM
RET_V_W = RET_HEADS * RET_VAL_DIM
IN_SPLITS = (ATT_Q_W, ATT_KV_W, ATT_KV_W, RET_QK_W, RET_QK_W, RET_V_W, RET_V_W, D_MODEL, D_MODEL)
IN_WIDTH = sum(IN_SPLITS)
(OFF_QA, OFF_KA, OFF_VA, OFF_QR, OFF_KR, OFF_VR, OFF_GR, OFF_GA, OFF_GB) = (
    int(v) for v in np.concatenate([[0], np.cumsum(IN_SPLITS)[:-1]]))

LANES = 128
SUBLANES = 8
V7X_VMEM_LIMIT_BYTES = 60 * 1024 * 1024

BLOCK = 128
MIX_ROWS = 512
PROJ_COLS = 512
FFN_ROWS = 1024
FFN_SUB_ROWS = 512
FFN_COLS = 512

BF16 = jnp.bfloat16
F32 = jnp.float32

assert WINDOW == ATT_BLOCK == RET_CHUNK == BLOCK == LANES
assert RET_KEY_DIM == LANES and ATT_KV_W == LANES and ROPE_HALF == SUBLANES
assert MIX_ROWS % BLOCK == 0

NT = (((1,), (1,)), ((), ()))


def _sigmoid(x):
    return 0.5 * jnp.tanh(0.5 * x) + 0.5


def _silu(x):
    return x * _sigmoid(x)


def _bits_zero(v):
    half = jnp.uint32(16)
    z = lax.shift_right_logical(lax.shift_right_logical(pltpu.bitcast(v, jnp.uint32), half), half)
    return z[:1, :1].astype(jnp.int32).astype(F32)


def _rms_scale(x, gain, zero=None):
    ms = jnp.mean(x * x, axis=-1, keepdims=True) + NORM_EPS
    if zero is not None:
        ms = ms + zero
    return (x * lax.rsqrt(ms)) * gain


def _mixer_kernel(cdec_ref,
                  xn_ref, xr_ref, gain_ref, w_ref, bg_ref,
                  cat_ref, sat_ref, ca_ref, s1_ref, s2_ref, cr_ref, sr_ref, qdec_ref, kdec_ref,
                  sink_ref, gn_ref, intra_ref, watt_ref, wret_ref, wout_ref,
                  h_ref,
                  qat_s, ka_s, vat_s, qr_s, qdr_s, kr_s, kdt_s, vr_s, sg_s, ga_s, gb_s,
                  xb_s, kprev_s, vprev_s, state_s, att_s, ret_s, *, tiles_per_seq):
    t = pl.program_id(0)
    ws = t & 1
    rs = 1 - ws
    rows = xn_ref.shape[0]
    nblk = rows // BLOCK
    proj_bufs = (qat_s, ka_s, vat_s, qr_s, qdr_s, kr_s, kdt_s, vr_s, sg_s, ga_s, gb_s)
    first = (jnp.maximum(t - 1, 0) % tiles_per_seq) == 0

    @pl.when(t == 0)
    def _():
        for buf in proj_bufs:
            buf[1] = jnp.zeros(buf.shape[1:], buf.dtype)
        xb_s[0] = _rms_scale(xr_ref[...], gain_ref[...]).astype(BF16)

    @pl.when(first)
    def _():
        kprev_s[...] = jnp.zeros_like(kprev_s)
        vprev_s[...] = jnp.zeros_like(vprev_s)
        state_s[...] = jnp.zeros_like(state_s)

    def proj(off, width):
        return jnp.dot(xb_s[ws], w_ref[:, off:off + width], preferred_element_type=F32)

    def store_blocks(ref, feat, val):
        for c in range(nblk):
            ref[ws, c, feat, :] = val[:, c * BLOCK:(c + 1) * BLOCK].astype(BF16)

    def rope_ret(zh):
        return zh * cr_ref[...] + pltpu.roll(zh, RET_KEY_DIM // 2, 1) * sr_ref[...]

    def per_block(table_ref, sl, val):
        return jnp.concatenate([val[c * BLOCK:(c + 1) * BLOCK] * table_ref[:, sl] for c in range(nblk)], axis=0)

    def p_qa():
        zt = proj(OFF_QA, ATT_Q_W).T
        cat, sat = cat_ref[...], sat_ref[...]
        for hd in range(ATT_HEADS):
            b0 = hd * ATT_HEAD_DIM
            x1, x2 = zt[b0:b0 + ROPE_HALF], zt[b0 + ROPE_HALF:b0 + ROPE_DIM]
            head = jnp.concatenate([x1 * cat - x2 * sat, x2 * cat + x1 * sat, zt[b0 + ROPE_DIM:b0 + ATT_HEAD_DIM]], axis=0)
            store_blocks(qat_s, slice(b0, b0 + ATT_HEAD_DIM), head * (ATT_HEAD_DIM ** -0.5))

    def p_kv():
        z = proj(OFF_KA, 2 * ATT_KV_W)
        zk = z[:, :ATT_KV_W]
        ka_s[ws] = (zk * ca_ref[...] + pltpu.roll(zk, LANES - ROPE_HALF, 1) * s1_ref[...]
                    + pltpu.roll(zk, ROPE_HALF, 1) * s2_ref[...]).astype(BF16)
        store_blocks(vat_s, slice(None), z[:, ATT_KV_W:].T)

    def p_qr():
        z = proj(OFF_QR, RET_QK_W)
        for h in range(RET_HEADS):
            sl = slice(h * RET_KEY_DIM, (h + 1) * RET_KEY_DIM)
            q = rope_ret(z[:, sl])
            qr_s[ws, :, sl] = q.astype(BF16)
            qdr_s[ws, :, sl] = per_block(qdec_ref, sl, q).astype(BF16)

    def p_kr():
        z = proj(OFF_KR, RET_QK_W)
        for h in range(RET_HEADS):
            sl = slice(h * RET_KEY_DIM, (h + 1) * RET_KEY_DIM)
            k = rope_ret(z[:, sl]) * (RET_KEY_DIM ** -0.5)
            kr_s[ws, :, sl] = k.astype(BF16)
            store_blocks(kdt_s, sl, per_block(kdec_ref, sl, k).T)

    def p_plain(buf, off, c, fn):
        def run():
            sl = slice(c * PROJ_COLS, (c + 1) * PROJ_COLS)
            buf[ws, :, sl] = fn(proj(off + c * PROJ_COLS, PROJ_COLS), sl).astype(BF16)
        return run

    halves = range(D_MODEL // PROJ_COLS)
    p_vr = [p_plain(vr_s, OFF_VR, c, lambda z, sl: z) for c in halves]
    p_gr = [p_plain(sg_s, OFF_GR, c, lambda z, sl: _silu(z)) for c in halves]
    p_ga = [p_plain(ga_s, OFF_GA, c, lambda z, sl: _sigmoid(z + bg_ref[:, sl])) for c in halves]
    p_gb = [p_plain(gb_s, OFF_GB, c,
                    lambda z, sl: _sigmoid(z + bg_ref[:, D_MODEL + sl.start:D_MODEL + sl.stop])) for c in halves]

    t_idx = lax.broadcasted_iota(jnp.int32, (BLOCK, ATT_GROUP * BLOCK), 0)
    i_idx = lax.broadcasted_iota(jnp.int32, (BLOCK, ATT_GROUP * BLOCK), 1) & (BLOCK - 1)
    from_prev = t_idx > i_idx
    zeros_q = jnp.zeros((ATT_HEAD_DIM, ATT_GROUP * BLOCK), BF16)
    ksl = [slice(h * RET_KEY_DIM, (h + 1) * RET_KEY_DIM) for h in range(RET_HEADS)]
    vsl = [slice(h * RET_VAL_DIM, (h + 1) * RET_VAL_DIM) for h in range(RET_HEADS)]
    rows_of = lambda c: pl.ds(c * BLOCK, BLOCK)

    def first_matmuls(c):
        rws = rows_of(c)
        qt = qat_s[rs, c]
        if c == 0:
            kc = jnp.concatenate([kprev_s[...], ka_s[rs, rws, :]], axis=0)
        else:
            kc = ka_s[rs, pl.ds((c - 1) * BLOCK, 2 * BLOCK), :]
        scores = []
        for kv in range(ATT_KV_HEADS):
            qg = jnp.concatenate(
                [qt[(kv * ATT_GROUP + g) * ATT_HEAD_DIM:(kv * ATT_GROUP + g + 1) * ATT_HEAD_DIM] for g in range(ATT_GROUP)],
                axis=1)
            rhs = jnp.concatenate([qg, zeros_q] if kv == 0 else [zeros_q, qg], axis=0)
            scores.append(jnp.dot(kc, rhs, preferred_element_type=F32))
        att = [lax.dot_general(qr_s[rs, rws, ksl[h]], kr_s[rs, rws, ksl[h]], NT, preferred_element_type=F32)
               for h in range(RET_HEADS)]
        upd = [jnp.dot(kdt_s[rs, c, ksl[h], :], vr_s[rs, rws, vsl[h]], preferred_element_type=F32)
               for h in range(RET_HEADS)]
        return scores, att, upd

    def softmax_and_decay(c, stage1, state):
        scores, att, upd = stage1
        prev_bias = jnp.where(first & (c == 0), NEG_INF, 0.0).astype(F32)
        probs = []
        for kv in range(ATT_KV_HEADS):
            sc = scores[kv]
            f = jnp.where(from_prev, sc[:BLOCK] + prev_bias, sc[BLOCK:])
            sink = sink_ref[kv]
            m = jnp.maximum(jnp.max(f, axis=0, keepdims=True), sink)
            e = jnp.exp(f - m)
            den = jnp.sum(e, axis=0, keepdims=True) + jnp.exp(sink - m)
            p = e * (1.0 / den)
            probs.append(jnp.concatenate([jnp.where(from_prev, p, 0.0), jnp.where(from_prev, 0.0, p)],
                                         axis=0).astype(BF16))
        att_b = [(att[h] * intra_ref[h]).astype(BF16) for h in range(RET_HEADS)]
        state_b = [state[h].astype(BF16) for h in range(RET_HEADS)]
        new_state = [state[h] * cdec_ref[h] + upd[h] for h in range(RET_HEADS)]
        return (probs, att_b, state_b), new_state

    def second_matmuls(c, ops):
        probs, att_b, state_b = ops
        rws = rows_of(c)
        vt_prev = vprev_s[...] if c == 0 else vat_s[rs, c - 1]
        vt_cur = vat_s[rs, c]
        outs = []
        for kv in range(ATT_KV_HEADS):
            hs = slice(kv * ATT_HEAD_DIM, (kv + 1) * ATT_HEAD_DIM)
            vt = jnp.concatenate([vt_prev[hs], vt_cur[hs]], axis=1)
            outs.append(jnp.dot(vt, probs[kv], preferred_element_type=F32))
        ret = [jnp.dot(jnp.concatenate([att_b[h], qdr_s[rs, rws, ksl[h]]], axis=1),
                       jnp.concatenate([vr_s[rs, rws, vsl[h]], state_b[h]], axis=0), preferred_element_type=F32)
               for h in range(RET_HEADS)]
        return outs, ret

    def finish(c, stage2):
        outs, ret = stage2
        rws = rows_of(c)
        ot = jnp.concatenate(outs, axis=0)
        for g in range(ATT_GROUP):
            att_s[rws, g * LANES:(g + 1) * LANES] = ot[:, g * BLOCK:(g + 1) * BLOCK].T.astype(BF16)
        for h in range(RET_HEADS):
            out = ret[h]
            mu = jnp.mean(out, axis=-1, keepdims=True)
            dev = out - mu
            var = jnp.mean(dev * dev, axis=-1, keepdims=True)
            yn = dev * lax.rsqrt(var + GN_EPS) * gn_ref[:, vsl[h]]
            ret_s[rws, vsl[h]] = (sg_s[rs, rws, vsl[h]].astype(F32) * yn).astype(BF16)

    assert nblk == 4
    state = [state_s[h] for h in range(RET_HEADS)]
    stage1 = [first_matmuls(c) for c in range(nblk)]
    fillers = [p_vr[0], p_vr[1], p_gr[0], p_gr[1]]
    for c in range(nblk):
        fillers[c]()
        ops, state = softmax_and_decay(c, stage1[c], state)
        finish(c, second_matmuls(c, ops))
    for h in range(RET_HEADS):
        state_s[h] = state[h]
    kprev_s[...] = ka_s[rs, (nblk - 1) * BLOCK:, :]
    vprev_s[...] = vat_s[rs, nblk - 1]
    p_ga[0]()
    p_ga[1]()
    ya = jnp.dot(att_s[...], watt_ref[...], preferred_element_type=F32)
    yr = jnp.dot(ret_s[...], wret_ref[...], preferred_element_type=F32)
    p_gb[0]()
    p_gb[1]()
    p_qr()
    merged = (ga_s[rs].astype(F32) * ya + gb_s[rs].astype(F32) * yr).astype(BF16)
    mixed = jnp.dot(merged, wout_ref[...], preferred_element_type=F32)
    h_ref[...] = xr_ref[...] + mixed
    late_zero = _bits_zero(mixed[:SUBLANES, :LANES])
    xb_s[rs] = _rms_scale(xn_ref[...], gain_ref[...], late_zero).astype(BF16)
    p_kr()
    p_qa()
    p_kv()


def _mixer_call(x, gain, w_in, b_gates, tables, cdec, sinks, gn_gain, intra, w_att_up, w_ret_up, w_out):
    b, s, d = x.shape
    rows = MIX_ROWS
    nblk = rows // BLOCK
    tps = s // rows
    ntiles = b * tps
    grid = (ntiles + 1,)

    proj_tile = lambda t: jnp.minimum(t, ntiles - 1)
    next_tile = lambda t: jnp.minimum(t + 1, ntiles - 1)
    mix_tile = lambda t: jnp.maximum(t - 1, 0)

    def tok(tile):
        return pl.BlockSpec((None, rows, d), lambda t: (tile(t) // tps, tile(t) % tps, 0))

    def const(shape):
        return pl.BlockSpec(shape, lambda t: (0,) * len(shape), pipeline_mode=pl.Buffered(1))

    pos_t = pl.BlockSpec((ROPE_HALF, rows), lambda t: (0, proj_tile(t) % tps))
    pos = pl.BlockSpec((rows, LANES), lambda t: (proj_tile(t) % tps, 0))
    smem = pl.BlockSpec(memory_space=pltpu.SMEM)
    slab = lambda feat: pltpu.VMEM((2, nblk, feat, BLOCK), BF16)
    tokbuf = lambda width: pltpu.VMEM((2, rows, width), BF16)
    return pl.pallas_call(
        functools.partial(_mixer_kernel, tiles_per_seq=tps),
        grid=grid,
        in_specs=[smem, tok(next_tile), tok(mix_tile), const((1, d)), const((d, IN_WIDTH)), const((1, 2 * D_MODEL)),
                  pos_t, pos_t, pos, pos, pos, pos, pos, const((BLOCK, RET_QK_W)), const((BLOCK, RET_QK_W)),
                  const((ATT_KV_HEADS, 1, ATT_GROUP * BLOCK)), const((1, RET_V_W)),
                  const((RET_HEADS, RET_CHUNK, RET_CHUNK)),
                  const((ATT_Q_W, d)), const((RET_V_W, d)), const((d, d))],
        out_specs=tok(mix_tile),
        out_shape=jax.ShapeDtypeStruct((b, s, d), F32),
        scratch_shapes=[
            slab(ATT_Q_W), tokbuf(ATT_KV_W), slab(ATT_KV_W), tokbuf(RET_QK_W), tokbuf(RET_QK_W), tokbuf(RET_QK_W),
            slab(RET_QK_W), tokbuf(RET_V_W), tokbuf(RET_V_W), tokbuf(D_MODEL), tokbuf(D_MODEL),
            tokbuf(d),
            pltpu.VMEM((BLOCK, ATT_KV_W), BF16),
            pltpu.VMEM((ATT_KV_W, BLOCK), BF16),
            pltpu.VMEM((RET_HEADS, RET_KEY_DIM, RET_VAL_DIM), F32),
            pltpu.VMEM((rows, ATT_Q_W), BF16),
            pltpu.VMEM((rows, RET_V_W), BF16),
        ],
        compiler_params=pltpu.CompilerParams(
            dimension_semantics=("arbitrary",), vmem_limit_bytes=V7X_VMEM_LIMIT_BYTES),
        name="mixer",
    )(cdec, x, x, gain, w_in, b_gates, *tables, sinks, gn_gain, intra, w_att_up, w_ret_up, w_out)


def _ffn_kernel(h_ref, gain_ref, w1_ref, w2_ref, fgain_ref, o_ref, *, final_norm):
    for s in range(h_ref.shape[0] // FFN_SUB_ROWS):
        rs = slice(s * FFN_SUB_ROWS, (s + 1) * FFN_SUB_ROWS)
        h = h_ref[rs, :]
        xb = _rms_scale(h, gain_ref[...]).astype(BF16)
        acc = jnp.zeros(h.shape, F32)
        for c in range(D_FF // FFN_COLS):
            sl = slice(c * FFN_COLS, (c + 1) * FFN_COLS)
            a = jnp.maximum(jnp.dot(xb, w1_ref[:, sl], preferred_element_type=F32), 0.0)
            acc = acc + jnp.dot((a * a).astype(BF16), w2_ref[sl, :], preferred_element_type=F32)
        y = h + acc
        if final_norm:
            y = _rms_scale(y, fgain_ref[...])
        o_ref[rs, :] = y


def _ffn_call(h, gain, w1, w2, fgain, final_norm):
    b, s, d = h.shape
    rows = FFN_ROWS
    grid = (b, s // rows)

    def const(shape):
        return pl.BlockSpec(shape, lambda i, j: (0,) * len(shape), pipeline_mode=pl.Buffered(1))

    tok = pl.BlockSpec((None, rows, d), lambda i, j: (i, j, 0))
    return pl.pallas_call(
        functools.partial(_ffn_kernel, final_norm=final_norm),
        grid=grid,
        in_specs=[tok, const((1, d)), const((d, D_FF)), const((D_FF, d)), const((1, d))],
        out_specs=tok,
        out_shape=jax.ShapeDtypeStruct((b, s, d), F32),
        compiler_params=pltpu.CompilerParams(
            dimension_semantics=("arbitrary", "arbitrary"), vmem_limit_bytes=V7X_VMEM_LIMIT_BYTES),
        name="ffn",
    )(h, gain, w1, w2, fgain)


def _decay_terms():
    h, c = RET_HEADS, RET_CHUNK
    log_gamma = jnp.log1p(-jnp.exp2(-5.0 - jnp.arange(h, dtype=F32)))
    idx = jnp.arange(c, dtype=F32)
    diff = idx[:, None] - idx[None, :]
    intra = jnp.where(diff >= 0, jnp.exp(jnp.maximum(diff, 0.0) * log_gamma[:, None, None]), 0.0)
    q_decay = jnp.exp((idx + 1.0)[None, :] * log_gamma[:, None])
    k_decay = jnp.exp((c - 1.0 - idx)[None, :] * log_gamma[:, None])
    chunk_decay = jnp.exp(c * log_gamma)
    return intra, q_decay, k_decay, chunk_decay


def _position_tables(seq_len, q_decay, k_decay):
    pos = jnp.arange(seq_len, dtype=F32)

    def cos_sin(dim, theta):
        inv_freq = theta ** (-jnp.arange(0, dim, 2, dtype=F32) / dim)
        ang = pos[:, None] * inv_freq[None, :]
        return jnp.cos(ang), jnp.sin(ang)

    cos, sin = cos_sin(ROPE_DIM, ROPE_THETA)
    pad = ATT_HEAD_DIM - ROPE_DIM
    zeros_half = jnp.zeros_like(sin)
    one_head = lambda parts: jnp.tile(jnp.concatenate(parts, axis=-1), (1, LANES // ATT_HEAD_DIM))
    ca = one_head([cos, cos, jnp.ones((seq_len, pad), F32)])
    s1 = one_head([-sin, zeros_half, jnp.zeros((seq_len, pad), F32)])
    s2 = one_head([zeros_half, sin, jnp.zeros((seq_len, pad), F32)])
    cos_r, sin_r = cos_sin(RET_KEY_DIM, RET_ROT_BASE)
    cr = jnp.concatenate([cos_r, cos_r], axis=-1)
    sr = jnp.concatenate([-sin_r, sin_r], axis=-1)

    def per_row(decay):
        return jnp.repeat(decay.T, RET_KEY_DIM, axis=1)

    return cos.T, sin.T, ca, s1, s2, cr, sr, per_row(q_decay), per_row(k_decay)


def kernel(x, norm_mix_gain, w_in, b_gates, attn_sinks, ret_gn_gain, w_att_up, w_ret_up, w_out,
           norm_mlp_gain, w_ff1, w_ff2, norm_final_gain):
    b, s, d = x.shape
    depth = w_in.shape[0]
    assert d == D_MODEL and s % MIX_ROWS == 0 and s % FFN_ROWS == 0
    intra, q_decay, k_decay, cdec = _decay_terms()
    tables = _position_tables(s, q_decay, k_decay)
    row = lambda v: v.reshape(1, -1).astype(F32)
    att_perm = np.arange(ATT_Q_W).reshape(ATT_KV_HEADS, ATT_GROUP, ATT_HEAD_DIM).transpose(1, 0, 2).reshape(-1)
    h = x
    for l in range(depth):
        w = w_in[l].astype(BF16)
        sinks = jnp.repeat(attn_sinks[l].astype(F32).reshape(ATT_KV_HEADS, 1, ATT_GROUP), BLOCK, axis=-1)
        h = _mixer_call(h, row(norm_mix_gain[l]), w, row(b_gates[l]), tables, cdec, sinks, row(ret_gn_gain[l]), intra,
                        w_att_up[l][att_perm].astype(BF16), w_ret_up[l].astype(BF16), w_out[l].astype(BF16))
        h = _ffn_call(h, row(norm_mlp_gain[l]), w_ff1[l].astype(BF16), w_ff2[l].astype(BF16),
                      row(norm_final_gain), final_norm=(l == depth - 1))
    return h
```

```python
import functools

import jax
import jax.numpy as jnp
import numpy as np
from jax import lax
from jax.experimental import pallas as pl
from jax.experimental.pallas import tpu as pltpu

D_MODEL = 1024
ATT_HEADS = 8
ATT_KV_HEADS = 2
ATT_GROUP = ATT_HEADS // ATT_KV_HEADS
ATT_HEAD_DIM = 64
WINDOW = 128
ATT_BLOCK = 128
ROPE_DIM = ATT_HEAD_DIM // 4
ROPE_HALF = ROPE_DIM // 2
ROPE_THETA = 500000.0
RET_HEADS = 4
RET_KEY_DIM = 128
RET_VAL_DIM = 256
RET_CHUNK = 128
RET_ROT_BASE = 10000.0
D_FF = 4 * D_MODEL
NORM_EPS = 1e-6
GN_EPS = 1e-6
NEG_INF = -1e30
LOG2E = 1.4426950408889634

ATT_Q_W = ATT_HEADS * ATT_HEAD_DIM
ATT_KV_W = ATT_KV_HEADS * ATT_HEAD_DIM
RET_QK_W = RET_HEADS * RET_KEY_DIM
RET_V_W = RET_HEADS * RET_VAL_DIM
IN_SPLITS = (ATT_Q_W, ATT_KV_W, ATT_KV_W, RET_QK_W, RET_QK_W, RET_V_W, RET_V_W, D_MODEL, D_MODEL)
IN_WIDTH = sum(IN_SPLITS)
(OFF_QA, OFF_KA, OFF_VA, OFF_QR, OFF_KR, OFF_VR, OFF_GR, OFF_GA, OFF_GB) = (
    int(v) for v in np.concatenate([[0], np.cumsum(IN_SPLITS)[:-1]]))

LANES = 128
SUBLANES = 8
V7X_VMEM_LIMIT_BYTES = 56 * 1024 * 1024

BLOCK = 128
PROJ_ROWS = 512
PROJ_COLS = 512
MIX_ROWS = 512
FFN_ROWS = 1024
FFN_SUB_ROWS = 512
FFN_COLS = 512

BF16 = jnp.bfloat16
F32 = jnp.float32

assert WINDOW == ATT_BLOCK == RET_CHUNK == BLOCK == LANES
assert RET_KEY_DIM == LANES and ATT_KV_W == LANES and ROPE_HALF == SUBLANES
assert PROJ_ROWS % BLOCK == 0 and MIX_ROWS % BLOCK == 0

NT = (((1,), (1,)), ((), ()))


def _sigmoid(x):
    return 0.5 * jnp.tanh(0.5 * x) + 0.5


def _silu(x):
    return x * _sigmoid(x)


def _rms_scale(x, gain):
    ms = jnp.mean(x * x, axis=-1, keepdims=True)
    return (x * lax.rsqrt(ms + NORM_EPS)) * gain


def _proj_kernel(x_ref, gain_ref, w_ref, bg_ref, gn_ref,
                 cat_ref, sat_ref, ca_ref, s1_ref, s2_ref, cr_ref, sr_ref, qdec_ref, kdec_ref,
                 qat_ref, ka_ref, vat_ref, qr_ref, qdr_ref, kr_ref, kdt_ref, vr_ref, sg_ref, ga_ref, gb_ref):
    rows = x_ref.shape[0]
    nblk = rows // BLOCK
    xb = _rms_scale(x_ref[...], gain_ref[...]).astype(BF16)

    def proj(off, width):
        return jnp.dot(xb, w_ref[:, off:off + width], preferred_element_type=F32)

    def store_blocks(ref, feat, val):
        for c in range(nblk):
            ref[c, feat, :] = val[:, c * BLOCK:(c + 1) * BLOCK].astype(BF16)

    zt = proj(OFF_QA, ATT_Q_W).T
    cat, sat = cat_ref[...], sat_ref[...]
    for hd in range(ATT_HEADS):
        b0 = hd * ATT_HEAD_DIM
        x1, x2 = zt[b0:b0 + ROPE_HALF], zt[b0 + ROPE_HALF:b0 + ROPE_DIM]
        head = jnp.concatenate([x1 * cat - x2 * sat, x2 * cat + x1 * sat, zt[b0 + ROPE_DIM:b0 + ATT_HEAD_DIM]], axis=0)
        store_blocks(qat_ref, slice(b0, b0 + ATT_HEAD_DIM), head * (ATT_HEAD_DIM ** -0.5 * LOG2E))
    store_blocks(vat_ref, slice(None), proj(OFF_VA, ATT_KV_W).T)
    z = proj(OFF_KA, ATT_KV_W)
    ka_ref[...] = (z * ca_ref[...] + pltpu.roll(z, LANES - ROPE_HALF, 1) * s1_ref[...]
                   + pltpu.roll(z, ROPE_HALF, 1) * s2_ref[...]).astype(BF16)

    cr, sr = cr_ref[...], sr_ref[...]

    def rope_ret(zh):
        return zh * cr + pltpu.roll(zh, RET_KEY_DIM // 2, 1) * sr

    z = proj(OFF_QR, RET_QK_W)
    for h in range(RET_HEADS):
        sl = slice(h * RET_KEY_DIM, (h + 1) * RET_KEY_DIM)
        q = rope_ret(z[:, sl])
        qr_ref[:, sl] = q.astype(BF16)
        qdr_ref[:, sl] = (q * qdec_ref[:, sl]).astype(BF16)
    z = proj(OFF_KR, RET_QK_W)
    for h in range(RET_HEADS):
        sl = slice(h * RET_KEY_DIM, (h + 1) * RET_KEY_DIM)
        k = rope_ret(z[:, sl]) * (RET_KEY_DIM ** -0.5)
        kr_ref[:, sl] = k.astype(BF16)
        store_blocks(kdt_ref, sl, (k * kdec_ref[:, sl]).T)
    for c in range(RET_V_W // PROJ_COLS):
        sl = slice(c * PROJ_COLS, (c + 1) * PROJ_COLS)
        vr_ref[:, sl] = proj(OFF_VR + c * PROJ_COLS, PROJ_COLS).astype(BF16)
    for c in range(RET_V_W // PROJ_COLS):
        sl = slice(c * PROJ_COLS, (c + 1) * PROJ_COLS)
        sg_ref[:, sl] = (_silu(proj(OFF_GR + c * PROJ_COLS, PROJ_COLS)) * gn_ref[:, sl]).astype(BF16)
    for c in range(D_MODEL // PROJ_COLS):
        sl = slice(c * PROJ_COLS, (c + 1) * PROJ_COLS)
        g = proj(OFF_GA + c * PROJ_COLS, PROJ_COLS) + bg_ref[:, sl]
        ga_ref[:, sl] = _sigmoid(g).astype(BF16)
    for c in range(D_MODEL // PROJ_COLS):
        sl = slice(c * PROJ_COLS, (c + 1) * PROJ_COLS)
        g = proj(OFF_GB + c * PROJ_COLS, PROJ_COLS) + bg_ref[:, D_MODEL + c * PROJ_COLS:D_MODEL + (c + 1) * PROJ_COLS]
        gb_ref[:, sl] = _sigmoid(g).astype(BF16)


def _proj_call(x, gain, w_in, b_gates, gn_gain, tables):
    b, s, d = x.shape
    rows = PROJ_ROWS
    nblk = rows // BLOCK
    grid = (s // rows, b)

    def tok(width):
        return pl.BlockSpec((None, rows, width), lambda j, i: (i, j, 0))

    def tok_t(feat):
        return pl.BlockSpec((None, nblk, feat, BLOCK), lambda j, i: (i, j, 0, 0))

    def const(shape):
        return pl.BlockSpec(shape, lambda j, i: (0,) * len(shape), pipeline_mode=pl.Buffered(1))

    pos_t = pl.BlockSpec((ROPE_HALF, rows), lambda j, i: (0, j))
    pos = pl.BlockSpec((rows, LANES), lambda j, i: (j, 0))
    tok_shape = lambda w: jax.ShapeDtypeStruct((b, s, w), BF16)
    tok_t_shape = lambda f: jax.ShapeDtypeStruct((b, s // BLOCK, f, BLOCK), BF16)
    return pl.pallas_call(
        _proj_kernel,
        grid=grid,
        in_specs=[tok(d), const((1, d)), const((d, IN_WIDTH)),
                  const((1, 2 * D_MODEL)), const((1, RET_V_W)), pos_t, pos_t, pos, pos, pos, pos, pos,
                  const((rows, RET_QK_W)), const((rows, RET_QK_W))],
        out_specs=[tok_t(ATT_Q_W), tok(ATT_KV_W), tok_t(ATT_KV_W), tok(RET_QK_W), tok(RET_QK_W), tok(RET_QK_W),
                   tok_t(RET_QK_W), tok(RET_V_W), tok(RET_V_W), tok(D_MODEL), tok(D_MODEL)],
        out_shape=[tok_t_shape(ATT_Q_W), tok_shape(ATT_KV_W), tok_t_shape(ATT_KV_W), tok_shape(RET_QK_W),
                   tok_shape(RET_QK_W), tok_shape(RET_QK_W), tok_t_shape(RET_QK_W), tok_shape(RET_V_W),
                   tok_shape(RET_V_W), tok_shape(D_MODEL), tok_shape(D_MODEL)],
        compiler_params=pltpu.CompilerParams(
            dimension_semantics=("arbitrary", "arbitrary"), vmem_limit_bytes=V7X_VMEM_LIMIT_BYTES),
        name="proj",
    )(x, gain, w_in, b_gates, gn_gain, *tables)


def _mix_kernel(cdec_ref,
                qat_ref, ka_ref, vat_ref, qr_ref, qdr_ref, kr_ref, kdt_ref, vr_ref, sg_ref, ga_ref, gb_ref, x_ref,
                sink_ref, intra_ref, watt_ref, wret_ref, wout_ref,
                h_ref,
                kext_ref, vext_ref, state_ref, att_ref, ret_ref, *, tiles_per_seq, ntiles):
    t = pl.program_id(0)
    ws = t & 1
    rs = 1 - ws
    nblk = x_ref.shape[0] // BLOCK
    first = (jnp.minimum(t, ntiles - 1) % tiles_per_seq) == 0

    @pl.when(t == 0)
    def _():
        att_ref[1] = jnp.zeros(att_ref.shape[1:], BF16)
        ret_ref[1] = jnp.zeros(ret_ref.shape[1:], BF16)

    @pl.when(first)
    def _():
        kext_ref[:BLOCK, :] = jnp.zeros((BLOCK, ATT_KV_W), BF16)
        vext_ref[0] = jnp.zeros((ATT_KV_W, BLOCK), BF16)
        state_ref[...] = jnp.zeros_like(state_ref)

    kext_ref[BLOCK:, :] = ka_ref[...]
    vext_ref[1:] = vat_ref[...]

    t_idx = lax.broadcasted_iota(jnp.int32, (BLOCK, ATT_GROUP * BLOCK), 0)
    i_idx = lax.broadcasted_iota(jnp.int32, (BLOCK, ATT_GROUP * BLOCK), 1) & (BLOCK - 1)
    from_prev = t_idx > i_idx
    zeros_q = jnp.zeros((ATT_HEAD_DIM, ATT_GROUP * BLOCK), BF16)

    ksl = [slice(h * RET_KEY_DIM, (h + 1) * RET_KEY_DIM) for h in range(RET_HEADS)]
    vsl = [slice(h * RET_VAL_DIM, (h + 1) * RET_VAL_DIM) for h in range(RET_HEADS)]
    rows_of = lambda c: pl.ds(c * BLOCK, BLOCK)

    def first_matmuls(c):
        rows = rows_of(c)
        qt = qat_ref[c]
        kc = kext_ref[pl.ds(c * BLOCK, 2 * BLOCK), :]
        scores = []
        for kv in range(ATT_KV_HEADS):
            qg = jnp.concatenate(
                [qt[(kv * ATT_GROUP + g) * ATT_HEAD_DIM:(kv * ATT_GROUP + g + 1) * ATT_HEAD_DIM] for g in range(ATT_GROUP)],
                axis=1)
            rhs = jnp.concatenate([qg, zeros_q] if kv == 0 else [zeros_q, qg], axis=0)
            scores.append(jnp.dot(kc, rhs, preferred_element_type=F32))
        att = [lax.dot_general(qr_ref[rows, ksl[h]], kr_ref[rows, ksl[h]], NT, preferred_element_type=F32)
               for h in range(RET_HEADS)]
        upd = [jnp.dot(kdt_ref[c, ksl[h], :], vr_ref[rows, vsl[h]], preferred_element_type=F32)
               for h in range(RET_HEADS)]
        return scores, att, upd

    def softmax_and_decay(c, stage1, state):
        scores, att, upd = stage1
        probs, inv_den = [], []
        for kv in range(ATT_KV_HEADS):
            sc = scores[kv]
            prev = sc[:BLOCK]
            if c == 0:
                prev = prev + jnp.where(first, NEG_INF, 0.0).astype(F32)
            f = jnp.where(from_prev, prev, sc[BLOCK:])
            sink = sink_ref[kv]
            m = jnp.maximum(jnp.max(f, axis=0, keepdims=True), sink)
            e = jnp.exp2(f - m)
            inv_den.append(1.0 / (jnp.sum(e, axis=0, keepdims=True) + jnp.exp2(sink - m)))
            probs.append(jnp.concatenate([jnp.where(from_prev, e, 0.0), jnp.where(from_prev, 0.0, e)],
                                         axis=0).astype(BF16))
        att_b = [(att[h] * intra_ref[h]).astype(BF16) for h in range(RET_HEADS)]
        state_b = [state[h].astype(BF16) for h in range(RET_HEADS)]
        new_state = [state[h] * cdec_ref[h] + upd[h] for h in range(RET_HEADS)]
        return (probs, inv_den, att_b, state_b), new_state

    def second_matmuls(c, ops):
        probs, inv_den, att_b, state_b = ops
        rows = rows_of(c)
        vt_prev, vt_cur = vext_ref[c], vext_ref[c + 1]
        outs = []
        for kv in range(ATT_KV_HEADS):
            hs = slice(kv * ATT_HEAD_DIM, (kv + 1) * ATT_HEAD_DIM)
            vt = jnp.concatenate([vt_prev[hs], vt_cur[hs]], axis=1)
            outs.append(jnp.dot(vt, probs[kv], preferred_element_type=F32) * inv_den[kv])
        ret = [jnp.dot(jnp.concatenate([att_b[h], qdr_ref[rows, ksl[h]]], axis=1),
                       jnp.concatenate([vr_ref[rows, vsl[h]], state_b[h]], axis=0), preferred_element_type=F32)
               for h in range(RET_HEADS)]
        return outs, ret

    def finish(c, stage2):
        outs, ret = stage2
        rows = rows_of(c)
        ot = jnp.concatenate(outs, axis=0)
        for g in range(ATT_GROUP):
            att_ref[ws, rows, g * LANES:(g + 1) * LANES] = ot[:, g * BLOCK:(g + 1) * BLOCK].T.astype(BF16)
        for h in range(RET_HEADS):
            out = ret[h]
            mu = jnp.mean(out, axis=-1, keepdims=True)
            dev = out - mu
            var = jnp.mean(dev * dev, axis=-1, keepdims=True)
            ret_ref[ws, rows, vsl[h]] = (dev * lax.rsqrt(var + GN_EPS)).astype(BF16) * sg_ref[rows, vsl[h]]

    def chunk_tail(c, stage1, state):
        ops, state = softmax_and_decay(c, stage1, state)
        finish(c, second_matmuls(c, ops))
        return state

    half = D_MODEL // 2
    cols = [slice(0, half), slice(half, D_MODEL)]

    assert nblk == 4
    state = [state_ref[h] for h in range(RET_HEADS)]
    s0 = first_matmuls(0)
    s1 = first_matmuls(1)
    ya = jnp.dot(att_ref[rs], watt_ref[...], preferred_element_type=F32)
    yr = [jnp.dot(ret_ref[rs], wret_ref[:, cols[0]], preferred_element_type=F32)]
    state = chunk_tail(0, s0, state)
    s2 = first_matmuls(2)
    yr.append(jnp.dot(ret_ref[rs], wret_ref[:, cols[1]], preferred_element_type=F32))
    state = chunk_tail(1, s1, state)
    s3 = first_matmuls(3)
    state = chunk_tail(2, s2, state)
    merged = jnp.concatenate(
        [ga_ref[:, cols[n]] * ya[:, cols[n]].astype(BF16) + gb_ref[:, cols[n]] * yr[n].astype(BF16)
         for n in range(2)], axis=1)
    h_ref[:, cols[0]] = x_ref[:, cols[0]] + jnp.dot(merged, wout_ref[:, cols[0]], preferred_element_type=F32)
    state = chunk_tail(3, s3, state)
    h_ref[:, cols[1]] = x_ref[:, cols[1]] + jnp.dot(merged, wout_ref[:, cols[1]], preferred_element_type=F32)
    for h in range(RET_HEADS):
        state_ref[h] = state[h]

    kext_ref[:BLOCK, :] = kext_ref[nblk * BLOCK:, :]
    vext_ref[0] = vext_ref[nblk]


def _mix_call(proj_out, x, cdec, sinks, intra, w_att_up, w_ret_up, w_out):
    b, s, d = x.shape
    rows = MIX_ROWS
    nblk = rows // BLOCK
    tps = s // rows
    ntiles = b * tps
    grid = (ntiles + 1,)

    chunk_tile = lambda t: jnp.minimum(t, ntiles - 1)
    proj_tile = lambda t: jnp.maximum(t - 1, 0)

    def tok(width, tile):
        return pl.BlockSpec((None, rows, width), lambda t: (tile(t) // tps, tile(t) % tps, 0))

    def tok_t(feat):
        return pl.BlockSpec((None, nblk, feat, BLOCK), lambda t: (chunk_tile(t) // tps, chunk_tile(t) % tps, 0, 0))

    def const(shape):
        return pl.BlockSpec(shape, lambda t: (0,) * len(shape), pipeline_mode=pl.Buffered(1))

    smem = pl.BlockSpec(memory_space=pltpu.SMEM)
    ctok = lambda width: tok(width, chunk_tile)
    return pl.pallas_call(
        functools.partial(_mix_kernel, tiles_per_seq=tps, ntiles=ntiles),
        grid=grid,
        in_specs=[smem,
                  tok_t(ATT_Q_W), ctok(ATT_KV_W), tok_t(ATT_KV_W), ctok(RET_QK_W), ctok(RET_QK_W), ctok(RET_QK_W),
                  tok_t(RET_QK_W), ctok(RET_V_W), ctok(RET_V_W),
                  tok(D_MODEL, proj_tile), tok(D_MODEL, proj_tile), tok(d, proj_tile),
                  const((ATT_KV_HEADS, 1, ATT_GROUP * BLOCK)),
                  const((RET_HEADS, RET_CHUNK, RET_CHUNK)),
                  const((ATT_Q_W, d)), const((RET_V_W, d)), const((d, d))],
        out_specs=tok(d, proj_tile),
        out_shape=jax.ShapeDtypeStruct((b, s, d), F32),
        scratch_shapes=[
            pltpu.VMEM((BLOCK + rows, ATT_KV_W), BF16),
            pltpu.VMEM((1 + nblk, ATT_KV_W, BLOCK), BF16),
            pltpu.VMEM((RET_HEADS, RET_KEY_DIM, RET_VAL_DIM), F32),
            pltpu.VMEM((2, rows, ATT_Q_W), BF16),
            pltpu.VMEM((2, rows, RET_V_W), BF16),
        ],
        compiler_params=pltpu.CompilerParams(
            dimension_semantics=("arbitrary",), vmem_limit_bytes=V7X_VMEM_LIMIT_BYTES),
        name="mix",
    )(cdec, *proj_out, x, sinks, intra, w_att_up, w_ret_up, w_out)


def _ffn_kernel(h_ref, gain_ref, w1_ref, w2_ref, fgain_ref, o_ref, *, final_norm):
    for s in range(h_ref.shape[0] // FFN_SUB_ROWS):
        rs = slice(s * FFN_SUB_ROWS, (s + 1) * FFN_SUB_ROWS)
        h = h_ref[rs, :]
        xb = _rms_scale(h, gain_ref[...]).astype(BF16)
        acc = jnp.zeros(h.shape, F32)
        for c in range(D_FF // FFN_COLS):
            sl = slice(c * FFN_COLS, (c + 1) * FFN_COLS)
            a = jnp.maximum(jnp.dot(xb, w1_ref[:, sl], preferred_element_type=F32), 0.0)
            acc = acc + jnp.dot((a * a).astype(BF16), w2_ref[sl, :], preferred_element_type=F32)
        y = h + acc
        if final_norm:
            y = _rms_scale(y, fgain_ref[...])
        o_ref[rs, :] = y


def _ffn_call(h, gain, w1, w2, fgain, final_norm):
    b, s, d = h.shape
    rows = FFN_ROWS
    grid = (b, s // rows)

    def const(shape):
        return pl.BlockSpec(shape, lambda i, j: (0,) * len(shape), pipeline_mode=pl.Buffered(1))

    tok = pl.BlockSpec((None, rows, d), lambda i, j: (i, j, 0))
    return pl.pallas_call(
        functools.partial(_ffn_kernel, final_norm=final_norm),
        grid=grid,
        in_specs=[tok, const((1, d)), const((d, D_FF)), const((D_FF, d)), const((1, d))],
        out_specs=tok,
        out_shape=jax.ShapeDtypeStruct((b, s, d), F32),
        compiler_params=pltpu.CompilerParams(
            dimension_semantics=("arbitrary", "arbitrary"), vmem_limit_bytes=V7X_VMEM_LIMIT_BYTES),
        name="ffn",
    )(h, gain, w1, w2, fgain)


def _decay_terms():
    h, c = RET_HEADS, RET_CHUNK
    log_gamma = jnp.log1p(-jnp.exp2(-5.0 - jnp.arange(h, dtype=F32)))
    idx = jnp.arange(c, dtype=F32)
    diff = idx[:, None] - idx[None, :]
    intra = jnp.where(diff >= 0, jnp.exp(jnp.maximum(diff, 0.0) * log_gamma[:, None, None]), 0.0)
    q_decay = jnp.exp((idx + 1.0)[None, :] * log_gamma[:, None])
    k_decay = jnp.exp((c - 1.0 - idx)[None, :] * log_gamma[:, None])
    chunk_decay = jnp.exp(c * log_gamma)
    return intra, q_decay, k_decay, chunk_decay


def _position_tables(seq_len, q_decay, k_decay):
    pos = jnp.arange(seq_len, dtype=F32)

    def cos_sin(dim, theta):
        inv_freq = theta ** (-jnp.arange(0, dim, 2, dtype=F32) / dim)
        ang = pos[:, None] * inv_freq[None, :]
        return jnp.cos(ang), jnp.sin(ang)

    cos, sin = cos_sin(ROPE_DIM, ROPE_THETA)
    pad = ATT_HEAD_DIM - ROPE_DIM
    zeros_half = jnp.zeros_like(sin)
    one_head = lambda parts: jnp.tile(jnp.concatenate(parts, axis=-1), (1, LANES // ATT_HEAD_DIM))
    ca = one_head([cos, cos, jnp.ones((seq_len, pad), F32)])
    s1 = one_head([-sin, zeros_half, jnp.zeros((seq_len, pad), F32)])
    s2 = one_head([zeros_half, sin, jnp.zeros((seq_len, pad), F32)])
    cos_r, sin_r = cos_sin(RET_KEY_DIM, RET_ROT_BASE)
    cr = jnp.concatenate([cos_r, cos_r], axis=-1)
    sr = jnp.concatenate([-sin_r, sin_r], axis=-1)

    def per_row(decay):
        t = jnp.repeat(decay.T, RET_KEY_DIM, axis=1)
        return jnp.tile(t, (PROJ_ROWS // RET_CHUNK, 1))

    return cos.T, sin.T, ca, s1, s2, cr, sr, per_row(q_decay), per_row(k_decay)


def kernel(x, norm_mix_gain, w_in, b_gates, attn_sinks, ret_gn_gain, w_att_up, w_ret_up, w_out,
           norm_mlp_gain, w_ff1, w_ff2, norm_final_gain):
    b, s, d = x.shape
    depth = w_in.shape[0]
    assert d == D_MODEL and s % MIX_ROWS == 0 and s % PROJ_ROWS == 0 and s % FFN_ROWS == 0
    intra, q_decay, k_decay, cdec = _decay_terms()
    tables = _position_tables(s, q_decay, k_decay)
    row = lambda v: v.reshape(1, -1).astype(F32)
    att_perm = np.arange(ATT_Q_W).reshape(ATT_KV_HEADS, ATT_GROUP, ATT_HEAD_DIM).transpose(1, 0, 2).reshape(-1)
    h = x
    for l in range(depth):
        w = w_in[l].astype(BF16)
        proj_out = _proj_call(h, row(norm_mix_gain[l]), w, row(b_gates[l]), row(ret_gn_gain[l]), tables)
        sinks = jnp.repeat((attn_sinks[l].astype(F32) * LOG2E).reshape(ATT_KV_HEADS, 1, ATT_GROUP), BLOCK, axis=-1)
        h = _mix_call(proj_out, h, cdec, sinks, intra,
                      w_att_up[l][att_perm].astype(BF16), w_ret_up[l].astype(BF16), w_out[l].astype(BF16))
        h = _ffn_call(h, row(norm_mlp_gain[l]), w_ff1[l].astype(BF16), w_ff2[l].astype(BF16),
                      row(norm_final_gain), final_norm=(l == depth - 1))
    return h
```

```python
import functools

import jax
import jax.numpy as jnp
import numpy as np
from jax import lax
from jax.experimental import pallas as pl
from jax.experimental.pallas import tpu as pltpu

D_MODEL = 1024
ATT_HEADS = 8
ATT_KV_HEADS = 2
ATT_GROUP = ATT_HEADS // ATT_KV_HEADS
ATT_HEAD_DIM = 64
WINDOW = 128
ATT_BLOCK = 128
ROPE_DIM = ATT_HEAD_DIM // 4
ROPE_HALF = ROPE_DIM // 2
ROPE_THETA = 500000.0
RET_HEADS = 4
RET_KEY_DIM = 128
RET_VAL_DIM = 256
RET_CHUNK = 128
RET_ROT_BASE = 10000.0
D_FF = 4 * D_MODEL
NORM_EPS = 1e-6
GN_EPS = 1e-6
NEG_INF = -1e30
LOG2E = 1.4426950408889634

ATT_Q_W = ATT_HEADS * ATT_HEAD_DIM
ATT_KV_W = ATT_KV_HEADS * ATT_HEAD_DIM
RET_QK_W = RET_HEADS * RET_KEY_DIM
RET_V_W = RET_HEADS * RET_VAL_DIM
IN_SPLITS = (ATT_Q_W, ATT_KV_W, ATT_KV_W, RET_QK_W, RET_QK_W, RET_V_W, RET_V_W, D_MODEL, D_MODEL)
IN_WIDTH = sum(IN_SPLITS)
(OFF_QA, OFF_KA, OFF_VA, OFF_QR, OFF_KR, OFF_VR, OFF_GR, OFF_GA, OFF_GB) = (
    int(v) for v in np.concatenate([[0], np.cumsum(IN_SPLITS)[:-1]]))

LANES = 128
SUBLANES = 8
V7X_VMEM_LIMIT_BYTES = 56 * 1024 * 1024

BLOCK = 128
PROJ_ROWS = 512
PROJ_COLS = 512
MIX_ROWS = 512
FFN_ROWS = 1024
FFN_SUB_ROWS = 512
FFN_COLS = 512

BF16 = jnp.bfloat16
F32 = jnp.float32

assert WINDOW == ATT_BLOCK == RET_CHUNK == BLOCK == LANES
assert RET_KEY_DIM == LANES and ATT_KV_W == LANES and ROPE_HALF == SUBLANES
assert PROJ_ROWS % BLOCK == 0 and MIX_ROWS % BLOCK == 0

NT = (((1,), (1,)), ((), ()))


def _sigmoid(x):
    return 0.5 * jnp.tanh(0.5 * x) + 0.5


def _silu(x):
    return x * _sigmoid(x)


def _rms_scale(x, gain):
    ms = jnp.mean(x * x, axis=-1, keepdims=True)
    return (x * lax.rsqrt(ms + NORM_EPS)) * gain


def _proj_kernel(x_ref, gain_ref, w_ref, bg_ref, gn_ref,
                 cat_ref, sat_ref, ca_ref, s1_ref, s2_ref, cr_ref, sr_ref, qdec_ref, kdec_ref,
                 qat_ref, ka_ref, vat_ref, qr_ref, qdr_ref, kr_ref, kdt_ref, vr_ref, sg_ref, ga_ref, gb_ref):
    rows = x_ref.shape[0]
    nblk = rows // BLOCK
    xb = _rms_scale(x_ref[...], gain_ref[...]).astype(BF16)

    def proj(off, width):
        return jnp.dot(xb, w_ref[:, off:off + width], preferred_element_type=F32)

    def store_blocks(ref, feat, val):
        for c in range(nblk):
            ref[c, feat, :] = val[:, c * BLOCK:(c + 1) * BLOCK].astype(BF16)

    zt = proj(OFF_QA, ATT_Q_W).T
    cat, sat = cat_ref[...], sat_ref[...]
    for hd in range(ATT_HEADS):
        b0 = hd * ATT_HEAD_DIM
        x1, x2 = zt[b0:b0 + ROPE_HALF], zt[b0 + ROPE_HALF:b0 + ROPE_DIM]
        head = jnp.concatenate([x1 * cat - x2 * sat, x2 * cat + x1 * sat, zt[b0 + ROPE_DIM:b0 + ATT_HEAD_DIM]], axis=0)
        store_blocks(qat_ref, slice(b0, b0 + ATT_HEAD_DIM), head * (ATT_HEAD_DIM ** -0.5 * LOG2E))
    store_blocks(vat_ref, slice(None), proj(OFF_VA, ATT_KV_W).T)
    z = proj(OFF_KA, ATT_KV_W)
    ka_ref[...] = (z * ca_ref[...] + pltpu.roll(z, LANES - ROPE_HALF, 1) * s1_ref[...]
                   + pltpu.roll(z, ROPE_HALF, 1) * s2_ref[...]).astype(BF16)

    cr, sr = cr_ref[...], sr_ref[...]

    def rope_ret(zh):
        return zh * cr + pltpu.roll(zh, RET_KEY_DIM // 2, 1) * sr

    z = proj(OFF_QR, RET_QK_W)
    for h in range(RET_HEADS):
        sl = slice(h * RET_KEY_DIM, (h + 1) * RET_KEY_DIM)
        q = rope_ret(z[:, sl])
        qr_ref[:, sl] = q.astype(BF16)
        qdr_ref[:, sl] = (q * qdec_ref[:, sl]).astype(BF16)
    z = proj(OFF_KR, RET_QK_W)
    for h in range(RET_HEADS):
        sl = slice(h * RET_KEY_DIM, (h + 1) * RET_KEY_DIM)
        k = rope_ret(z[:, sl]) * (RET_KEY_DIM ** -0.5)
        kr_ref[:, sl] = k.astype(BF16)
        store_blocks(kdt_ref, sl, (k * kdec_ref[:, sl]).T)
    for c in range(RET_V_W // PROJ_COLS):
        sl = slice(c * PROJ_COLS, (c + 1) * PROJ_COLS)
        vr_ref[:, sl] = proj(OFF_VR + c * PROJ_COLS, PROJ_COLS).astype(BF16)
    for c in range(RET_V_W // PROJ_COLS):
        sl = slice(c * PROJ_COLS, (c + 1) * PROJ_COLS)
        sg_ref[:, sl] = (_silu(proj(OFF_GR + c * PROJ_COLS, PROJ_COLS)) * gn_ref[:, sl]).astype(BF16)
    for c in range(D_MODEL // PROJ_COLS):
        sl = slice(c * PROJ_COLS, (c + 1) * PROJ_COLS)
        g = proj(OFF_GA + c * PROJ_COLS, PROJ_COLS) + bg_ref[:, sl]
        ga_ref[:, sl] = _sigmoid(g).astype(BF16)
    for c in range(D_MODEL // PROJ_COLS):
        sl = slice(c * PROJ_COLS, (c + 1) * PROJ_COLS)
        g = proj(OFF_GB + c * PROJ_COLS, PROJ_COLS) + bg_ref[:, D_MODEL + c * PROJ_COLS:D_MODEL + (c + 1) * PROJ_COLS]
        gb_ref[:, sl] = _sigmoid(g).astype(BF16)


def _proj_call(x, gain, w_in, b_gates, gn_gain, tables):
    b, s, d = x.shape
    rows = PROJ_ROWS
    nblk = rows // BLOCK
    grid = (s // rows, b)

    def tok(width):
        return pl.BlockSpec((None, rows, width), lambda j, i: (i, j, 0))

    def tok_t(feat):
        return pl.BlockSpec((None, nblk, feat, BLOCK), lambda j, i: (i, j, 0, 0))

    def const(shape):
        return pl.BlockSpec(shape, lambda j, i: (0,) * len(shape), pipeline_mode=pl.Buffered(1))

    pos_t = pl.BlockSpec((ROPE_HALF, rows), lambda j, i: (0, j))
    pos = pl.BlockSpec((rows, LANES), lambda j, i: (j, 0))
    tok_shape = lambda w: jax.ShapeDtypeStruct((b, s, w), BF16)
    tok_t_shape = lambda f: jax.ShapeDtypeStruct((b, s // BLOCK, f, BLOCK), BF16)
    return pl.pallas_call(
        _proj_kernel,
        grid=grid,
        in_specs=[tok(d), const((1, d)), const((d, IN_WIDTH)),
                  const((1, 2 * D_MODEL)), const((1, RET_V_W)), pos_t, pos_t, pos, pos, pos, pos, pos,
                  const((rows, RET_QK_W)), const((rows, RET_QK_W))],
        out_specs=[tok_t(ATT_Q_W), tok(ATT_KV_W), tok_t(ATT_KV_W), tok(RET_QK_W), tok(RET_QK_W), tok(RET_QK_W),
                   tok_t(RET_QK_W), tok(RET_V_W), tok(RET_V_W), tok(D_MODEL), tok(D_MODEL)],
        out_shape=[tok_t_shape(ATT_Q_W), tok_shape(ATT_KV_W), tok_t_shape(ATT_KV_W), tok_shape(RET_QK_W),
                   tok_shape(RET_QK_W), tok_shape(RET_QK_W), tok_t_shape(RET_QK_W), tok_shape(RET_V_W),
                   tok_shape(RET_V_W), tok_shape(D_MODEL), tok_shape(D_MODEL)],
        compiler_params=pltpu.CompilerParams(
            dimension_semantics=("arbitrary", "arbitrary"), vmem_limit_bytes=V7X_VMEM_LIMIT_BYTES),
        name="proj",
    )(x, gain, w_in, b_gates, gn_gain, *tables)


def _mix_kernel(cdec_ref,
                qat_ref, ka_ref, vat_ref, qr_ref, qdr_ref, kr_ref, kdt_ref, vr_ref, sg_ref, ga_ref, gb_ref,
                sink_ref, intra_ref, watt_ref, wret_ref, wout_ref,
                h_ref,
                kext_ref, vext_ref, state_ref, att_ref, ret_ref, *, tiles_per_seq, ntiles):
    t = pl.program_id(0)
    ws = t & 1
    rs = 1 - ws
    nblk = ka_ref.shape[0] // BLOCK
    first = (jnp.minimum(t, ntiles - 1) % tiles_per_seq) == 0

    @pl.when(t == 0)
    def _():
        att_ref[1] = jnp.zeros(att_ref.shape[1:], BF16)
        ret_ref[1] = jnp.zeros(ret_ref.shape[1:], BF16)

    @pl.when(first)
    def _():
        kext_ref[:BLOCK, :] = jnp.zeros((BLOCK, ATT_KV_W), BF16)
        vext_ref[0] = jnp.zeros((ATT_KV_W, BLOCK), BF16)
        state_ref[...] = jnp.zeros_like(state_ref)

    kext_ref[BLOCK:, :] = ka_ref[...]
    vext_ref[1:] = vat_ref[...]

    t_idx = lax.broadcasted_iota(jnp.int32, (BLOCK, ATT_GROUP * BLOCK), 0)
    i_idx = lax.broadcasted_iota(jnp.int32, (BLOCK, ATT_GROUP * BLOCK), 1) & (BLOCK - 1)
    from_prev = t_idx > i_idx
    zeros_q = jnp.zeros((ATT_HEAD_DIM, ATT_GROUP * BLOCK), BF16)

    ksl = [slice(h * RET_KEY_DIM, (h + 1) * RET_KEY_DIM) for h in range(RET_HEADS)]
    vsl = [slice(h * RET_VAL_DIM, (h + 1) * RET_VAL_DIM) for h in range(RET_HEADS)]
    rows_of = lambda c: pl.ds(c * BLOCK, BLOCK)

    def first_matmuls(c):
        rows = rows_of(c)
        qt = qat_ref[c]
        kc = kext_ref[pl.ds(c * BLOCK, 2 * BLOCK), :]
        scores = []
        for kv in range(ATT_KV_HEADS):
            qg = jnp.concatenate(
                [qt[(kv * ATT_GROUP + g) * ATT_HEAD_DIM:(kv * ATT_GROUP + g + 1) * ATT_HEAD_DIM] for g in range(ATT_GROUP)],
                axis=1)
            rhs = jnp.concatenate([qg, zeros_q] if kv == 0 else [zeros_q, qg], axis=0)
            scores.append(jnp.dot(kc, rhs, preferred_element_type=F32))
        att = [lax.dot_general(qr_ref[rows, ksl[h]], kr_ref[rows, ksl[h]], NT, preferred_element_type=F32)
               for h in range(RET_HEADS)]
        upd = [jnp.dot(kdt_ref[c, ksl[h], :], vr_ref[rows, vsl[h]], preferred_element_type=F32)
               for h in range(RET_HEADS)]
        return scores, att, upd

    def softmax_and_decay(c, stage1, state):
        scores, att, upd = stage1
        probs, inv_den = [], []
        for kv in range(ATT_KV_HEADS):
            sc = scores[kv]
            prev = sc[:BLOCK]
            if c == 0:
                prev = prev + jnp.where(first, NEG_INF, 0.0).astype(F32)
            f = jnp.where(from_prev, prev, sc[BLOCK:])
            sink = sink_ref[kv]
            m = jnp.maximum(jnp.max(f, axis=0, keepdims=True), sink)
            e = jnp.exp2(f - m)
            inv_den.append(1.0 / (jnp.sum(e, axis=0, keepdims=True) + jnp.exp2(sink - m)))
            probs.append(jnp.concatenate([jnp.where(from_prev, e, 0.0), jnp.where(from_prev, 0.0, e)],
                                         axis=0).astype(BF16))
        att_b = [(att[h] * intra_ref[h]).astype(BF16) for h in range(RET_HEADS)]
        state_b = [state[h].astype(BF16) for h in range(RET_HEADS)]
        new_state = [state[h] * cdec_ref[h] + upd[h] for h in range(RET_HEADS)]
        return (probs, inv_den, att_b, state_b), new_state

    def second_matmuls(c, ops):
        probs, inv_den, att_b, state_b = ops
        rows = rows_of(c)
        vt_prev, vt_cur = vext_ref[c], vext_ref[c + 1]
        outs = []
        for kv in range(ATT_KV_HEADS):
            hs = slice(kv * ATT_HEAD_DIM, (kv + 1) * ATT_HEAD_DIM)
            vt = jnp.concatenate([vt_prev[hs], vt_cur[hs]], axis=1)
            outs.append(jnp.dot(vt, probs[kv], preferred_element_type=F32) * inv_den[kv])
        ret = [jnp.dot(jnp.concatenate([att_b[h], qdr_ref[rows, ksl[h]]], axis=1),
                       jnp.concatenate([vr_ref[rows, vsl[h]], state_b[h]], axis=0), preferred_element_type=F32)
               for h in range(RET_HEADS)]
        return outs, ret

    def finish(c, stage2):
        outs, ret = stage2
        rows = rows_of(c)
        ot = jnp.concatenate(outs, axis=0)
        for g in range(ATT_GROUP):
            att_ref[ws, rows, g * LANES:(g + 1) * LANES] = ot[:, g * BLOCK:(g + 1) * BLOCK].T.astype(BF16)
        for h in range(RET_HEADS):
            out = ret[h]
            mu = jnp.mean(out, axis=-1, keepdims=True)
            dev = out - mu
            var = jnp.mean(dev * dev, axis=-1, keepdims=True)
            ret_ref[ws, rows, vsl[h]] = (dev * lax.rsqrt(var + GN_EPS)).astype(BF16) * sg_ref[rows, vsl[h]]

    def chunk_tail(c, stage1, state):
        ops, state = softmax_and_decay(c, stage1, state)
        finish(c, second_matmuls(c, ops))
        return state

    half = D_MODEL // 2
    cols = [slice(0, half), slice(half, D_MODEL)]

    assert nblk == 4
    state = [state_ref[h] for h in range(RET_HEADS)]
    s0 = first_matmuls(0)
    s1 = first_matmuls(1)
    ya = jnp.dot(att_ref[rs], watt_ref[...], preferred_element_type=F32)
    yr = [jnp.dot(ret_ref[rs], wret_ref[:, cols[0]], preferred_element_type=F32)]
    state = chunk_tail(0, s0, state)
    s2 = first_matmuls(2)
    yr.append(jnp.dot(ret_ref[rs], wret_ref[:, cols[1]], preferred_element_type=F32))
    state = chunk_tail(1, s1, state)
    s3 = first_matmuls(3)
    state = chunk_tail(2, s2, state)
    merged = jnp.concatenate(
        [ga_ref[:, cols[n]] * ya[:, cols[n]].astype(BF16) + gb_ref[:, cols[n]] * yr[n].astype(BF16)
         for n in range(2)], axis=1)
    h_ref[:, cols[0]] = jnp.dot(merged, wout_ref[:, cols[0]], preferred_element_type=F32)
    state = chunk_tail(3, s3, state)
    h_ref[:, cols[1]] = jnp.dot(merged, wout_ref[:, cols[1]], preferred_element_type=F32)
    for h in range(RET_HEADS):
        state_ref[h] = state[h]

    kext_ref[:BLOCK, :] = kext_ref[nblk * BLOCK:, :]
    vext_ref[0] = vext_ref[nblk]


def _mix_call(proj_out, cdec, sinks, intra, w_att_up, w_ret_up, w_out):
    b, s, d = proj_out[-1].shape
    rows = MIX_ROWS
    nblk = rows // BLOCK
    tps = s // rows
    ntiles = b * tps
    grid = (ntiles + 1,)

    chunk_tile = lambda t: jnp.minimum(t, ntiles - 1)
    proj_tile = lambda t: jnp.maximum(t - 1, 0)

    def tok(width, tile):
        return pl.BlockSpec((None, rows, width), lambda t: (tile(t) // tps, tile(t) % tps, 0))

    def tok_t(feat):
        return pl.BlockSpec((None, nblk, feat, BLOCK), lambda t: (chunk_tile(t) // tps, chunk_tile(t) % tps, 0, 0))

    def const(shape):
        return pl.BlockSpec(shape, lambda t: (0,) * len(shape), pipeline_mode=pl.Buffered(1))

    smem = pl.BlockSpec(memory_space=pltpu.SMEM)
    ctok = lambda width: tok(width, chunk_tile)
    return pl.pallas_call(
        functools.partial(_mix_kernel, tiles_per_seq=tps, ntiles=ntiles),
        grid=grid,
        in_specs=[smem,
                  tok_t(ATT_Q_W), ctok(ATT_KV_W), tok_t(ATT_KV_W), ctok(RET_QK_W), ctok(RET_QK_W), ctok(RET_QK_W),
                  tok_t(RET_QK_W), ctok(RET_V_W), ctok(RET_V_W),
                  tok(D_MODEL, proj_tile), tok(D_MODEL, proj_tile),
                  const((ATT_KV_HEADS, 1, ATT_GROUP * BLOCK)),
                  const((RET_HEADS, RET_CHUNK, RET_CHUNK)),
                  const((ATT_Q_W, d)), const((RET_V_W, d)), const((d, d))],
        out_specs=tok(d, proj_tile),
        out_shape=jax.ShapeDtypeStruct((b, s, d), F32),
        scratch_shapes=[
            pltpu.VMEM((BLOCK + rows, ATT_KV_W), BF16),
            pltpu.VMEM((1 + nblk, ATT_KV_W, BLOCK), BF16),
            pltpu.VMEM((RET_HEADS, RET_KEY_DIM, RET_VAL_DIM), F32),
            pltpu.VMEM((2, rows, ATT_Q_W), BF16),
            pltpu.VMEM((2, rows, RET_V_W), BF16),
        ],
        compiler_params=pltpu.CompilerParams(
            dimension_semantics=("arbitrary",), vmem_limit_bytes=V7X_VMEM_LIMIT_BYTES),
        name="mix",
    )(cdec, *proj_out, sinks, intra, w_att_up, w_ret_up, w_out)


def _ffn_kernel(x_ref, mixed_ref, gain_ref, w1_ref, w2_ref, fgain_ref, o_ref, *, final_norm):
    for s in range(x_ref.shape[0] // FFN_SUB_ROWS):
        rs = slice(s * FFN_SUB_ROWS, (s + 1) * FFN_SUB_ROWS)
        h = x_ref[rs, :] + mixed_ref[rs, :]
        xb = _rms_scale(h, gain_ref[...]).astype(BF16)
        acc = jnp.zeros(h.shape, F32)
        for c in range(D_FF // FFN_COLS):
            sl = slice(c * FFN_COLS, (c + 1) * FFN_COLS)
            a = jnp.maximum(jnp.dot(xb, w1_ref[:, sl], preferred_element_type=F32), 0.0)
            acc = acc + jnp.dot((a * a).astype(BF16), w2_ref[sl, :], preferred_element_type=F32)
        y = h + acc
        if final_norm:
            y = _rms_scale(y, fgain_ref[...])
        o_ref[rs, :] = y


def _ffn_call(x, mixed, gain, w1, w2, fgain, final_norm):
    b, s, d = x.shape
    rows = FFN_ROWS
    grid = (b, s // rows)

    def const(shape):
        return pl.BlockSpec(shape, lambda i, j: (0,) * len(shape), pipeline_mode=pl.Buffered(1))

    tok = pl.BlockSpec((None, rows, d), lambda i, j: (i, j, 0))
    return pl.pallas_call(
        functools.partial(_ffn_kernel, final_norm=final_norm),
        grid=grid,
        in_specs=[tok, tok, const((1, d)), const((d, D_FF)), const((D_FF, d)), const((1, d))],
        out_specs=tok,
        out_shape=jax.ShapeDtypeStruct((b, s, d), F32),
        compiler_params=pltpu.CompilerParams(
            dimension_semantics=("arbitrary", "arbitrary"), vmem_limit_bytes=V7X_VMEM_LIMIT_BYTES),
        name="ffn",
    )(x, mixed, gain, w1, w2, fgain)


def _decay_terms():
    h, c = RET_HEADS, RET_CHUNK
    log_gamma = jnp.log1p(-jnp.exp2(-5.0 - jnp.arange(h, dtype=F32)))
    idx = jnp.arange(c, dtype=F32)
    diff = idx[:, None] - idx[None, :]
    intra = jnp.where(diff >= 0, jnp.exp(jnp.maximum(diff, 0.0) * log_gamma[:, None, None]), 0.0)
    q_decay = jnp.exp((idx + 1.0)[None, :] * log_gamma[:, None])
    k_decay = jnp.exp((c - 1.0 - idx)[None, :] * log_gamma[:, None])
    chunk_decay = jnp.exp(c * log_gamma)
    return intra, q_decay, k_decay, chunk_decay


def _position_tables(seq_len, q_decay, k_decay):
    pos = jnp.arange(seq_len, dtype=F32)

    def cos_sin(dim, theta):
        inv_freq = theta ** (-jnp.arange(0, dim, 2, dtype=F32) / dim)
        ang = pos[:, None] * inv_freq[None, :]
        return jnp.cos(ang), jnp.sin(ang)

    cos, sin = cos_sin(ROPE_DIM, ROPE_THETA)
    pad = ATT_HEAD_DIM - ROPE_DIM
    zeros_half = jnp.zeros_like(sin)
    one_head = lambda parts: jnp.tile(jnp.concatenate(parts, axis=-1), (1, LANES // ATT_HEAD_DIM))
    ca = one_head([cos, cos, jnp.ones((seq_len, pad), F32)])
    s1 = one_head([-sin, zeros_half, jnp.zeros((seq_len, pad), F32)])
    s2 = one_head([zeros_half, sin, jnp.zeros((seq_len, pad), F32)])
    cos_r, sin_r = cos_sin(RET_KEY_DIM, RET_ROT_BASE)
    cr = jnp.concatenate([cos_r, cos_r], axis=-1)
    sr = jnp.concatenate([-sin_r, sin_r], axis=-1)

    def per_row(decay):
        t = jnp.repeat(decay.T, RET_KEY_DIM, axis=1)
        return jnp.tile(t, (PROJ_ROWS // RET_CHUNK, 1))

    return cos.T, sin.T, ca, s1, s2, cr, sr, per_row(q_decay), per_row(k_decay)


def kernel(x, norm_mix_gain, w_in, b_gates, attn_sinks, ret_gn_gain, w_att_up, w_ret_up, w_out,
           norm_mlp_gain, w_ff1, w_ff2, norm_final_gain):
    b, s, d = x.shape
    depth = w_in.shape[0]
    assert d == D_MODEL and s % MIX_ROWS == 0 and s % PROJ_ROWS == 0 and s % FFN_ROWS == 0
    intra, q_decay, k_decay, cdec = _decay_terms()
    tables = _position_tables(s, q_decay, k_decay)
    row = lambda v: v.reshape(1, -1).astype(F32)
    att_perm = np.arange(ATT_Q_W).reshape(ATT_KV_HEADS, ATT_GROUP, ATT_HEAD_DIM).transpose(1, 0, 2).reshape(-1)
    h = x
    for l in range(depth):
        w = w_in[l].astype(BF16)
        proj_out = _proj_call(h, row(norm_mix_gain[l]), w, row(b_gates[l]), row(ret_gn_gain[l]), tables)
        sinks = jnp.repeat((attn_sinks[l].astype(F32) * LOG2E).reshape(ATT_KV_HEADS, 1, ATT_GROUP), BLOCK, axis=-1)
        mixed = _mix_call(proj_out, cdec, sinks, intra,
                      w_att_up[l][att_perm].astype(BF16), w_ret_up[l].astype(BF16), w_out[l].astype(BF16))
        h = _ffn_call(h, mixed, row(norm_mlp_gain[l]), w_ff1[l].astype(BF16), w_ff2[l].astype(BF16),
                      row(norm_final_gain), final_norm=(l == depth - 1))
    return h
```

```python
import functools

import jax
import jax.numpy as jnp
import numpy as np
from jax import lax
from jax.experimental import pallas as pl
from jax.experimental.pallas import tpu as pltpu

D_MODEL = 1024
ATT_HEADS = 8
ATT_KV_HEADS = 2
ATT_GROUP = ATT_HEADS // ATT_KV_HEADS
ATT_HEAD_DIM = 64
WINDOW = 128
ATT_BLOCK = 128
ROPE_DIM = ATT_HEAD_DIM // 4
ROPE_HALF = ROPE_DIM // 2
ROPE_THETA = 500000.0
RET_HEADS = 4
RET_KEY_DIM = 128
RET_VAL_DIM = 256
RET_CHUNK = 128
RET_ROT_BASE = 10000.0
D_FF = 4 * D_MODEL
NORM_EPS = 1e-6
GN_EPS = 1e-6
NEG_INF = -1e30
LOG2E = 1.4426950408889634

ATT_Q_W = ATT_HEADS * ATT_HEAD_DIM
ATT_KV_W = ATT_KV_HEADS * ATT_HEAD_DIM
RET_QK_W = RET_HEADS * RET_KEY_DIM
RET_V_W = RET_HEADS * RET_VAL_DIM
IN_SPLITS = (ATT_Q_W, ATT_KV_W, ATT_KV_W, RET_QK_W, RET_QK_W, RET_V_W, RET_V_W, D_MODEL, D_MODEL)
IN_WIDTH = sum(IN_SPLITS)
(OFF_QA, OFF_KA, OFF_VA, OFF_QR, OFF_KR, OFF_VR, OFF_GR, OFF_GA, OFF_GB) = (
    int(v) for v in np.concatenate([[0], np.cumsum(IN_SPLITS)[:-1]]))

LANES = 128
SUBLANES = 8
V7X_VMEM_LIMIT_BYTES = 56 * 1024 * 1024

BLOCK = 128
PROJ_ROWS = 512
PROJ_COLS = 512
MIX_ROWS = 512
FFN_ROWS = 1024
FFN_SUB_ROWS = 512
FFN_COLS = 512

BF16 = jnp.bfloat16
F32 = jnp.float32

assert WINDOW == ATT_BLOCK == RET_CHUNK == BLOCK == LANES
assert RET_KEY_DIM == LANES and ATT_KV_W == LANES and ROPE_HALF == SUBLANES
assert PROJ_ROWS % BLOCK == 0 and MIX_ROWS % BLOCK == 0

NT = (((1,), (1,)), ((), ()))


def _sigmoid(x):
    return 0.5 * jnp.tanh(0.5 * x) + 0.5


def _silu(x):
    return x * _sigmoid(x)


def _rms_scale(x, gain):
    ms = jnp.mean(x * x, axis=-1, keepdims=True)
    return (x * lax.rsqrt(ms + NORM_EPS)) * gain


def _proj_kernel(x_ref, gain_ref, w_ref, bg_ref,
                 cat_ref, sat_ref, ca_ref, s1_ref, s2_ref, cr_ref, sr_ref, qdec_ref, kdec_ref,
                 qat_ref, ka_ref, vat_ref, qr_ref, qdr_ref, kr_ref, kdt_ref, vr_ref, sg_ref, ga_ref, gb_ref):
    rows = x_ref.shape[0]
    nblk = rows // BLOCK
    xb = _rms_scale(x_ref[...], gain_ref[...]).astype(BF16)

    def proj(off, width):
        return jnp.dot(xb, w_ref[:, off:off + width], preferred_element_type=F32)

    def store_blocks(ref, feat, val):
        for c in range(nblk):
            ref[c, feat, :] = val[:, c * BLOCK:(c + 1) * BLOCK].astype(BF16)

    zt = proj(OFF_QA, ATT_Q_W).T
    cat, sat = cat_ref[...], sat_ref[...]
    for hd in range(ATT_HEADS):
        b0 = hd * ATT_HEAD_DIM
        x1, x2 = zt[b0:b0 + ROPE_HALF], zt[b0 + ROPE_HALF:b0 + ROPE_DIM]
        head = jnp.concatenate([x1 * cat - x2 * sat, x2 * cat + x1 * sat, zt[b0 + ROPE_DIM:b0 + ATT_HEAD_DIM]], axis=0)
        store_blocks(qat_ref, slice(b0, b0 + ATT_HEAD_DIM), head * (ATT_HEAD_DIM ** -0.5 * LOG2E))
    store_blocks(vat_ref, slice(None), proj(OFF_VA, ATT_KV_W).T)
    z = proj(OFF_KA, ATT_KV_W)
    ka_ref[...] = (z * ca_ref[...] + pltpu.roll(z, LANES - ROPE_HALF, 1) * s1_ref[...]
                   + pltpu.roll(z, ROPE_HALF, 1) * s2_ref[...]).astype(BF16)

    cr, sr = cr_ref[...], sr_ref[...]

    def rope_ret(zh):
        return zh * cr + pltpu.roll(zh, RET_KEY_DIM // 2, 1) * sr

    z = proj(OFF_QR, RET_QK_W)
    for h in range(RET_HEADS):
        sl = slice(h * RET_KEY_DIM, (h + 1) * RET_KEY_DIM)
        q = rope_ret(z[:, sl])
        qr_ref[:, sl] = q.astype(BF16)
        qdr_ref[:, sl] = (q * qdec_ref[:, sl]).astype(BF16)
    z = proj(OFF_KR, RET_QK_W)
    for h in range(RET_HEADS):
        sl = slice(h * RET_KEY_DIM, (h + 1) * RET_KEY_DIM)
        k = rope_ret(z[:, sl]) * (RET_KEY_DIM ** -0.5)
        kr_ref[:, sl] = k.astype(BF16)
        store_blocks(kdt_ref, sl, (k * kdec_ref[:, sl]).T)
    for c in range(RET_V_W // PROJ_COLS):
        sl = slice(c * PROJ_COLS, (c + 1) * PROJ_COLS)
        sg_ref[:, sl] = _silu(proj(OFF_GR + c * PROJ_COLS, PROJ_COLS)).astype(BF16)
    for c in range(D_MODEL // PROJ_COLS):
        sl = slice(c * PROJ_COLS, (c + 1) * PROJ_COLS)
        g = proj(OFF_GA + c * PROJ_COLS, PROJ_COLS) + bg_ref[:, sl]
        ga_ref[:, sl] = _sigmoid(g).astype(BF16)
    for c in range(D_MODEL // PROJ_COLS):
        sl = slice(c * PROJ_COLS, (c + 1) * PROJ_COLS)
        g = proj(OFF_GB + c * PROJ_COLS, PROJ_COLS) + bg_ref[:, D_MODEL + c * PROJ_COLS:D_MODEL + (c + 1) * PROJ_COLS]
        gb_ref[:, sl] = _sigmoid(g).astype(BF16)
    for c in range(RET_V_W // PROJ_COLS):
        sl = slice(c * PROJ_COLS, (c + 1) * PROJ_COLS)
        vr_ref[:, sl] = proj(OFF_VR + c * PROJ_COLS, PROJ_COLS).astype(BF16)


def _proj_call(x, gain, w_in, b_gates, tables):
    b, s, d = x.shape
    rows = PROJ_ROWS
    nblk = rows // BLOCK
    grid = (s // rows, b)

    def tok(width):
        return pl.BlockSpec((None, rows, width), lambda j, i: (i, j, 0))

    def tok_t(feat):
        return pl.BlockSpec((None, nblk, feat, BLOCK), lambda j, i: (i, j, 0, 0))

    def const(shape):
        return pl.BlockSpec(shape, lambda j, i: (0,) * len(shape), pipeline_mode=pl.Buffered(1))

    pos_t = pl.BlockSpec((ROPE_HALF, rows), lambda j, i: (0, j))
    pos = pl.BlockSpec((rows, LANES), lambda j, i: (j, 0))
    tok_shape = lambda w: jax.ShapeDtypeStruct((b, s, w), BF16)
    tok_t_shape = lambda f: jax.ShapeDtypeStruct((b, s // BLOCK, f, BLOCK), BF16)
    return pl.pallas_call(
        _proj_kernel,
        grid=grid,
        in_specs=[tok(d), const((1, d)), const((d, IN_WIDTH)),
                  const((1, 2 * D_MODEL)), pos_t, pos_t, pos, pos, pos, pos, pos,
                  const((rows, RET_QK_W)), const((rows, RET_QK_W))],
        out_specs=[tok_t(ATT_Q_W), tok(ATT_KV_W), tok_t(ATT_KV_W), tok(RET_QK_W), tok(RET_QK_W), tok(RET_QK_W),
                   tok_t(RET_QK_W), tok(RET_V_W), tok(RET_V_W), tok(D_MODEL), tok(D_MODEL)],
        out_shape=[tok_t_shape(ATT_Q_W), tok_shape(ATT_KV_W), tok_t_shape(ATT_KV_W), tok_shape(RET_QK_W),
                   tok_shape(RET_QK_W), tok_shape(RET_QK_W), tok_t_shape(RET_QK_W), tok_shape(RET_V_W),
                   tok_shape(RET_V_W), tok_shape(D_MODEL), tok_shape(D_MODEL)],
        compiler_params=pltpu.CompilerParams(
            dimension_semantics=("arbitrary", "arbitrary"), vmem_limit_bytes=V7X_VMEM_LIMIT_BYTES),
        name="proj",
    )(x, gain, w_in, b_gates, *tables)


def _mix_kernel(cdec_ref,
                qat_ref, ka_ref, vat_ref, qr_ref, qdr_ref, kr_ref, kdt_ref, vr_ref, sg_ref, ga_ref, gb_ref, x_ref,
                sink_ref, gn_ref, intra_ref, watt_ref, wret_ref, wout_ref,
                h_ref,
                kext_ref, vext_ref, state_ref, att_ref, ret_ref, merged_ref, *, tiles_per_seq, ntiles):
    t = pl.program_id(0)
    ws = t & 1
    rs = 1 - ws
    nblk = x_ref.shape[0] // BLOCK
    first = (jnp.minimum(t, ntiles - 1) % tiles_per_seq) == 0

    @pl.when(t == 0)
    def _():
        att_ref[1] = jnp.zeros(att_ref.shape[1:], BF16)
        ret_ref[1] = jnp.zeros(ret_ref.shape[1:], BF16)

    @pl.when(first)
    def _():
        kext_ref[:BLOCK, :] = jnp.zeros((BLOCK, ATT_KV_W), BF16)
        vext_ref[0] = jnp.zeros((ATT_KV_W, BLOCK), BF16)
        state_ref[...] = jnp.zeros_like(state_ref)

    kext_ref[BLOCK:, :] = ka_ref[...]
    vext_ref[1:] = vat_ref[...]

    from_prev = (lax.broadcasted_iota(jnp.int32, (BLOCK, BLOCK), 0)
                 > lax.broadcasted_iota(jnp.int32, (BLOCK, BLOCK), 1))

    def band_select(a, b):
        pick = lambda v, g: v[:, g * BLOCK:(g + 1) * BLOCK] if hasattr(v, "shape") else v
        return jnp.concatenate([jnp.where(from_prev, pick(a, g), pick(b, g)) for g in range(ATT_GROUP)], axis=1)
    zeros_q = jnp.zeros((ATT_HEAD_DIM, ATT_GROUP * BLOCK), BF16)

    ksl = [slice(h * RET_KEY_DIM, (h + 1) * RET_KEY_DIM) for h in range(RET_HEADS)]
    vsl = [slice(h * RET_VAL_DIM, (h + 1) * RET_VAL_DIM) for h in range(RET_HEADS)]
    rows_of = lambda c: pl.ds(c * BLOCK, BLOCK)

    def first_matmuls(c):
        rows = rows_of(c)
        qt = qat_ref[c]
        kc = kext_ref[pl.ds(c * BLOCK, 2 * BLOCK), :]
        scores = []
        for kv in range(ATT_KV_HEADS):
            qg = jnp.concatenate(
                [qt[(kv * ATT_GROUP + g) * ATT_HEAD_DIM:(kv * ATT_GROUP + g + 1) * ATT_HEAD_DIM] for g in range(ATT_GROUP)],
                axis=1)
            rhs = jnp.concatenate([qg, zeros_q] if kv == 0 else [zeros_q, qg], axis=0)
            scores.append(jnp.dot(kc, rhs, preferred_element_type=F32))
        att = [lax.dot_general(qr_ref[rows, ksl[h]], kr_ref[rows, ksl[h]], NT, preferred_element_type=F32)
               for h in range(RET_HEADS)]
        upd = [jnp.dot(kdt_ref[c, ksl[h], :], vr_ref[rows, vsl[h]], preferred_element_type=F32)
               for h in range(RET_HEADS)]
        return scores, att, upd

    def softmax_and_decay(c, stage1, state):
        scores, att, upd = stage1
        probs, inv_den = [], []
        for kv in range(ATT_KV_HEADS):
            sc = scores[kv]
            prev = sc[:BLOCK]
            if c == 0:
                prev = prev + jnp.where(first, NEG_INF, 0.0).astype(F32)
            f = band_select(prev, sc[BLOCK:])
            sink = sink_ref[kv]
            m = jnp.maximum(jnp.max(f, axis=0, keepdims=True), sink)
            e = jnp.exp2(f - m)
            inv_den.append(1.0 / (jnp.sum(e, axis=0, keepdims=True) + jnp.exp2(sink - m)))
            probs.append(jnp.concatenate([band_select(e, 0.0), band_select(0.0, e)], axis=0).astype(BF16))
        att_b = [(att[h] * intra_ref[h]).astype(BF16) for h in range(RET_HEADS)]
        state_b = [state[h].astype(BF16) for h in range(RET_HEADS)]
        new_state = [state[h] * cdec_ref[h] + upd[h] for h in range(RET_HEADS)]
        return (probs, inv_den, att_b, state_b), new_state

    def second_matmuls(c, ops):
        probs, inv_den, att_b, state_b = ops
        rows = rows_of(c)
        vt_prev, vt_cur = vext_ref[c], vext_ref[c + 1]
        outs = []
        for kv in range(ATT_KV_HEADS):
            hs = slice(kv * ATT_HEAD_DIM, (kv + 1) * ATT_HEAD_DIM)
            vt = jnp.concatenate([vt_prev[hs], vt_cur[hs]], axis=1)
            outs.append(jnp.dot(vt, probs[kv], preferred_element_type=F32) * inv_den[kv])
        ret = [jnp.dot(jnp.concatenate([att_b[h], qdr_ref[rows, ksl[h]]], axis=1),
                       jnp.concatenate([vr_ref[rows, vsl[h]], state_b[h]], axis=0), preferred_element_type=F32)
               for h in range(RET_HEADS)]
        return outs, ret

    def finish(c, stage2):
        outs, ret = stage2
        rows = rows_of(c)
        ot = jnp.concatenate(outs, axis=0)
        for g in range(ATT_GROUP):
            att_ref[ws, rows, g * LANES:(g + 1) * LANES] = ot[:, g * BLOCK:(g + 1) * BLOCK].T.astype(BF16)
        for h in range(RET_HEADS):
            out = ret[h]
            mu = jnp.mean(out, axis=-1, keepdims=True)
            dev = out - mu
            var = jnp.mean(dev * dev, axis=-1, keepdims=True)
            yn = dev * lax.rsqrt(var + GN_EPS) * gn_ref[:, vsl[h]]
            ret_ref[ws, rows, vsl[h]] = yn.astype(BF16) * sg_ref[rows, vsl[h]]

    def chunk_tail(c, stage1, state):
        ops, state = softmax_and_decay(c, stage1, state)
        finish(c, second_matmuls(c, ops))
        return state

    half = D_MODEL // 2
    cols = [slice(0, half), slice(half, D_MODEL)]

    assert nblk == 4
    state = [state_ref[h] for h in range(RET_HEADS)]
    def merged_half(n):
        ya = jnp.dot(att_ref[rs], watt_ref[:, cols[n]], preferred_element_type=F32)
        yr = jnp.dot(ret_ref[rs], wret_ref[:, cols[n]], preferred_element_type=F32)
        merged_ref[:, cols[n]] = ga_ref[:, cols[n]] * ya.astype(BF16) + gb_ref[:, cols[n]] * yr.astype(BF16)

    def out_half(n):
        h_ref[:, cols[n]] = x_ref[:, cols[n]] + jnp.dot(merged_ref[...], wout_ref[:, cols[n]],
                                                        preferred_element_type=F32)

    s0 = first_matmuls(0)
    s1 = first_matmuls(1)
    merged_half(0)
    state = chunk_tail(0, s0, state)
    s2 = first_matmuls(2)
    merged_half(1)
    state = chunk_tail(1, s1, state)
    s3 = first_matmuls(3)
    state = chunk_tail(2, s2, state)
    out_half(0)
    state = chunk_tail(3, s3, state)
    out_half(1)
    for h in range(RET_HEADS):
        state_ref[h] = state[h]

    kext_ref[:BLOCK, :] = kext_ref[nblk * BLOCK:, :]
    vext_ref[0] = vext_ref[nblk]


def _mix_call(proj_out, x, cdec, sinks, gn_gain, intra, w_att_up, w_ret_up, w_out):
    b, s, d = x.shape
    rows = MIX_ROWS
    nblk = rows // BLOCK
    tps = s // rows
    ntiles = b * tps
    grid = (ntiles + 1,)

    chunk_tile = lambda t: jnp.minimum(t, ntiles - 1)
    proj_tile = lambda t: jnp.maximum(t - 1, 0)

    def tok(width, tile):
        return pl.BlockSpec((None, rows, width), lambda t: (tile(t) // tps, tile(t) % tps, 0))

    def tok_t(feat):
        return pl.BlockSpec((None, nblk, feat, BLOCK), lambda t: (chunk_tile(t) // tps, chunk_tile(t) % tps, 0, 0))

    def const(shape):
        return pl.BlockSpec(shape, lambda t: (0,) * len(shape), pipeline_mode=pl.Buffered(1))

    smem = pl.BlockSpec(memory_space=pltpu.SMEM)
    ctok = lambda width: tok(width, chunk_tile)
    return pl.pallas_call(
        functools.partial(_mix_kernel, tiles_per_seq=tps, ntiles=ntiles),
        grid=grid,
        in_specs=[smem,
                  tok_t(ATT_Q_W), ctok(ATT_KV_W), tok_t(ATT_KV_W), ctok(RET_QK_W), ctok(RET_QK_W), ctok(RET_QK_W),
                  tok_t(RET_QK_W), ctok(RET_V_W), ctok(RET_V_W),
                  tok(D_MODEL, proj_tile), tok(D_MODEL, proj_tile), tok(d, proj_tile),
                  const((ATT_KV_HEADS, 1, ATT_GROUP * BLOCK)), const((1, RET_V_W)),
                  const((RET_HEADS, RET_CHUNK, RET_CHUNK)),
                  const((ATT_Q_W, d)), const((RET_V_W, d)), const((d, d))],
        out_specs=tok(d, proj_tile),
        out_shape=jax.ShapeDtypeStruct((b, s, d), F32),
        scratch_shapes=[
            pltpu.VMEM((BLOCK + rows, ATT_KV_W), BF16),
            pltpu.VMEM((1 + nblk, ATT_KV_W, BLOCK), BF16),
            pltpu.VMEM((RET_HEADS, RET_KEY_DIM, RET_VAL_DIM), F32),
            pltpu.VMEM((2, rows, ATT_Q_W), BF16),
            pltpu.VMEM((2, rows, RET_V_W), BF16),
            pltpu.VMEM((rows, d), BF16),
        ],
        compiler_params=pltpu.CompilerParams(
            dimension_semantics=("arbitrary",), vmem_limit_bytes=V7X_VMEM_LIMIT_BYTES),
        name="mix",
    )(cdec, *proj_out, x, sinks, gn_gain, intra, w_att_up, w_ret_up, w_out)


def _ffn_kernel(h_ref, gain_ref, w1_ref, w2_ref, fgain_ref, o_ref, *, final_norm):
    for s in range(h_ref.shape[0] // FFN_SUB_ROWS):
        rs = slice(s * FFN_SUB_ROWS, (s + 1) * FFN_SUB_ROWS)
        h = h_ref[rs, :]
        xb = _rms_scale(h, gain_ref[...]).astype(BF16)
        acc = jnp.zeros(h.shape, F32)
        for c in range(D_FF // FFN_COLS):
            sl = slice(c * FFN_COLS, (c + 1) * FFN_COLS)
            a = jnp.maximum(jnp.dot(xb, w1_ref[:, sl], preferred_element_type=F32), 0.0)
            acc = acc + jnp.dot((a * a).astype(BF16), w2_ref[sl, :], preferred_element_type=F32)
        y = h + acc
        if final_norm:
            y = _rms_scale(y, fgain_ref[...])
        o_ref[rs, :] = y


def _ffn_call(h, gain, w1, w2, fgain, final_norm):
    b, s, d = h.shape
    rows = FFN_ROWS
    grid = (b, s // rows)

    def const(shape):
        return pl.BlockSpec(shape, lambda i, j: (0,) * len(shape), pipeline_mode=pl.Buffered(1))

    tok = pl.BlockSpec((None, rows, d), lambda i, j: (i, j, 0))
    return pl.pallas_call(
        functools.partial(_ffn_kernel, final_norm=final_norm),
        grid=grid,
        in_specs=[tok, const((1, d)), const((d, D_FF)), const((D_FF, d)), const((1, d))],
        out_specs=tok,
        out_shape=jax.ShapeDtypeStruct((b, s, d), F32),
        compiler_params=pltpu.CompilerParams(
            dimension_semantics=("arbitrary", "arbitrary"), vmem_limit_bytes=V7X_VMEM_LIMIT_BYTES),
        name="ffn",
    )(h, gain, w1, w2, fgain)


def _decay_terms():
    h, c = RET_HEADS, RET_CHUNK
    log_gamma = jnp.log1p(-jnp.exp2(-5.0 - jnp.arange(h, dtype=F32)))
    idx = jnp.arange(c, dtype=F32)
    diff = idx[:, None] - idx[None, :]
    intra = jnp.where(diff >= 0, jnp.exp(jnp.maximum(diff, 0.0) * log_gamma[:, None, None]), 0.0)
    q_decay = jnp.exp((idx + 1.0)[None, :] * log_gamma[:, None])
    k_decay = jnp.exp((c - 1.0 - idx)[None, :] * log_gamma[:, None])
    chunk_decay = jnp.exp(c * log_gamma)
    return intra, q_decay, k_decay, chunk_decay


def _position_tables(seq_len, q_decay, k_decay):
    pos = jnp.arange(seq_len, dtype=F32)

    def cos_sin(dim, theta):
        inv_freq = theta ** (-jnp.arange(0, dim, 2, dtype=F32) / dim)
        ang = pos[:, None] * inv_freq[None, :]
        return jnp.cos(ang), jnp.sin(ang)

    cos, sin = cos_sin(ROPE_DIM, ROPE_THETA)
    pad = ATT_HEAD_DIM - ROPE_DIM
    zeros_half = jnp.zeros_like(sin)
    one_head = lambda parts: jnp.tile(jnp.concatenate(parts, axis=-1), (1, LANES // ATT_HEAD_DIM))
    ca = one_head([cos, cos, jnp.ones((seq_len, pad), F32)])
    s1 = one_head([-sin, zeros_half, jnp.zeros((seq_len, pad), F32)])
    s2 = one_head([zeros_half, sin, jnp.zeros((seq_len, pad), F32)])
    cos_r, sin_r = cos_sin(RET_KEY_DIM, RET_ROT_BASE)
    cr = jnp.concatenate([cos_r, cos_r], axis=-1)
    sr = jnp.concatenate([-sin_r, sin_r], axis=-1)

    def per_row(decay):
        t = jnp.repeat(decay.T, RET_KEY_DIM, axis=1)
        return jnp.tile(t, (PROJ_ROWS // RET_CHUNK, 1))

    return cos.T, sin.T, ca, s1, s2, cr, sr, per_row(q_decay), per_row(k_decay)


def kernel(x, norm_mix_gain, w_in, b_gates, attn_sinks, ret_gn_gain, w_att_up, w_ret_up, w_out,
           norm_mlp_gain, w_ff1, w_ff2, norm_final_gain):
    b, s, d = x.shape
    depth = w_in.shape[0]
    assert d == D_MODEL and s % MIX_ROWS == 0 and s % PROJ_ROWS == 0 and s % FFN_ROWS == 0
    intra, q_decay, k_decay, cdec = _decay_terms()
    tables = _position_tables(s, q_decay, k_decay)
    row = lambda v: v.reshape(1, -1).astype(F32)
    att_perm = np.arange(ATT_Q_W).reshape(ATT_KV_HEADS, ATT_GROUP, ATT_HEAD_DIM).transpose(1, 0, 2).reshape(-1)
    h = x
    for l in range(depth):
        w = w_in[l].astype(BF16)
        proj_out = _proj_call(h, row(norm_mix_gain[l]), w, row(b_gates[l]), tables)
        sinks = jnp.repeat((attn_sinks[l].astype(F32) * LOG2E).reshape(ATT_KV_HEADS, 1, ATT_GROUP), BLOCK, axis=-1)
        h = _mix_call(proj_out, h, cdec, sinks, row(ret_gn_gain[l]), intra,
                      w_att_up[l][att_perm].astype(BF16), w_ret_up[l].astype(BF16), w_out[l].astype(BF16))
        h = _ffn_call(h, row(norm_mlp_gain[l]), w_ff1[l].astype(BF16), w_ff2[l].astype(BF16),
                      row(norm_final_gain), final_norm=(l == depth - 1))
    return h
```

```python
import functools

import jax
import jax.numpy as jnp
import numpy as np
from jax import lax
from jax.experimental import pallas as pl
from jax.experimental.pallas import tpu as pltpu

D_MODEL = 1024
ATT_HEADS = 8
ATT_KV_HEADS = 2
ATT_GROUP = ATT_HEADS // ATT_KV_HEADS
ATT_HEAD_DIM = 64
WINDOW = 128
ATT_BLOCK = 128
ROPE_DIM = ATT_HEAD_DIM // 4
ROPE_HALF = ROPE_DIM // 2
ROPE_THETA = 500000.0
RET_HEADS = 4
RET_KEY_DIM = 128
RET_VAL_DIM = 256
RET_CHUNK = 128
RET_ROT_BASE = 10000.0
D_FF = 4 * D_MODEL
NORM_EPS = 1e-6
GN_EPS = 1e-6
NEG_INF = -1e30
LOG2E = 1.4426950408889634

ATT_Q_W = ATT_HEADS * ATT_HEAD_DIM
ATT_KV_W = ATT_KV_HEADS * ATT_HEAD_DIM
RET_QK_W = RET_HEADS * RET_KEY_DIM
RET_V_W = RET_HEADS * RET_VAL_DIM
IN_SPLITS = (ATT_Q_W, ATT_KV_W, ATT_KV_W, RET_QK_W, RET_QK_W, RET_V_W, RET_V_W, D_MODEL, D_MODEL)
IN_WIDTH = sum(IN_SPLITS)
(OFF_QA, OFF_KA, OFF_VA, OFF_QR, OFF_KR, OFF_VR, OFF_GR, OFF_GA, OFF_GB) = (
    int(v) for v in np.concatenate([[0], np.cumsum(IN_SPLITS)[:-1]]))

LANES = 128
SUBLANES = 8
V7X_VMEM_LIMIT_BYTES = 56 * 1024 * 1024

BLOCK = 128
PROJ_ROWS = 512
PROJ_COLS = 512
MIX_ROWS = 512
FFN_ROWS = 1024
FFN_SUB_ROWS = 512
FFN_COLS = 512

BF16 = jnp.bfloat16
F32 = jnp.float32

assert WINDOW == ATT_BLOCK == RET_CHUNK == BLOCK == LANES
assert RET_KEY_DIM == LANES and ATT_KV_W == LANES and ROPE_HALF == SUBLANES
assert PROJ_ROWS % BLOCK == 0 and MIX_ROWS % BLOCK == 0

NT = (((1,), (1,)), ((), ()))


def _sigmoid(x):
    return 0.5 * jnp.tanh(0.5 * x) + 0.5


def _silu(x):
    return x * _sigmoid(x)


def _rms_scale(x, gain):
    ms = jnp.mean(x * x, axis=-1, keepdims=True)
    return (x * lax.rsqrt(ms + NORM_EPS)) * gain


def _proj_kernel(x_ref, gain_ref, w_ref, bg_ref,
                 cat_ref, sat_ref, ca_ref, s1_ref, s2_ref, cr_ref, sr_ref, qdec_ref, kdec_ref,
                 qat_ref, ka_ref, vat_ref, qr_ref, qdr_ref, kr_ref, kdt_ref, vr_ref, sg_ref, ga_ref, gb_ref):
    rows = x_ref.shape[0]
    nblk = rows // BLOCK
    xb = _rms_scale(x_ref[...], gain_ref[...]).astype(BF16)

    def proj(off, width):
        return jnp.dot(xb, w_ref[:, off:off + width], preferred_element_type=F32)

    def store_blocks(ref, feat, val):
        for c in range(nblk):
            ref[c, feat, :] = val[:, c * BLOCK:(c + 1) * BLOCK].astype(BF16)

    zt = proj(OFF_QA, ATT_Q_W).T
    cat, sat = cat_ref[...], sat_ref[...]
    for hd in range(ATT_HEADS):
        b0 = hd * ATT_HEAD_DIM
        x1, x2 = zt[b0:b0 + ROPE_HALF], zt[b0 + ROPE_HALF:b0 + ROPE_DIM]
        head = jnp.concatenate([x1 * cat - x2 * sat, x2 * cat + x1 * sat, zt[b0 + ROPE_DIM:b0 + ATT_HEAD_DIM]], axis=0)
        store_blocks(qat_ref, slice(b0, b0 + ATT_HEAD_DIM), head * (ATT_HEAD_DIM ** -0.5 * LOG2E))
    store_blocks(vat_ref, slice(None), proj(OFF_VA, ATT_KV_W).T)
    z = proj(OFF_KA, ATT_KV_W)
    ka_ref[...] = (z * ca_ref[...] + pltpu.roll(z, LANES - ROPE_HALF, 1) * s1_ref[...]
                   + pltpu.roll(z, ROPE_HALF, 1) * s2_ref[...]).astype(BF16)

    cr, sr = cr_ref[...], sr_ref[...]

    def rope_ret(zh):
        return zh * cr + pltpu.roll(zh, RET_KEY_DIM // 2, 1) * sr

    z = proj(OFF_QR, RET_QK_W)
    for h in range(RET_HEADS):
        sl = slice(h * RET_KEY_DIM, (h + 1) * RET_KEY_DIM)
        q = rope_ret(z[:, sl])
        qr_ref[:, sl] = q.astype(BF16)
        qdr_ref[:, sl] = (q * qdec_ref[:, sl]).astype(BF16)
    z = proj(OFF_KR, RET_QK_W)
    for h in range(RET_HEADS):
        sl = slice(h * RET_KEY_DIM, (h + 1) * RET_KEY_DIM)
        k = rope_ret(z[:, sl]) * (RET_KEY_DIM ** -0.5)
        kr_ref[:, sl] = k.astype(BF16)
        store_blocks(kdt_ref, sl, (k * kdec_ref[:, sl]).T)
    for c in range(RET_V_W // PROJ_COLS):
        sl = slice(c * PROJ_COLS, (c + 1) * PROJ_COLS)
        sg_ref[:, sl] = _silu(proj(OFF_GR + c * PROJ_COLS, PROJ_COLS)).astype(BF16)
    for c in range(D_MODEL // PROJ_COLS):
        sl = slice(c * PROJ_COLS, (c + 1) * PROJ_COLS)
        g = proj(OFF_GA + c * PROJ_COLS, PROJ_COLS) + bg_ref[:, sl]
        ga_ref[:, sl] = _sigmoid(g).astype(BF16)
    for c in range(D_MODEL // PROJ_COLS):
        sl = slice(c * PROJ_COLS, (c + 1) * PROJ_COLS)
        g = proj(OFF_GB + c * PROJ_COLS, PROJ_COLS) + bg_ref[:, D_MODEL + c * PROJ_COLS:D_MODEL + (c + 1) * PROJ_COLS]
        gb_ref[:, sl] = _sigmoid(g).astype(BF16)
    for c in range(RET_V_W // PROJ_COLS):
        sl = slice(c * PROJ_COLS, (c + 1) * PROJ_COLS)
        vr_ref[:, sl] = proj(OFF_VR + c * PROJ_COLS, PROJ_COLS).astype(BF16)


def _proj_call(x, gain, w_in, b_gates, tables):
    b, s, d = x.shape
    rows = PROJ_ROWS
    nblk = rows // BLOCK
    grid = (s // rows, b)

    def tok(width):
        return pl.BlockSpec((None, rows, width), lambda j, i: (i, j, 0))

    def tok_t(feat):
        return pl.BlockSpec((None, nblk, feat, BLOCK), lambda j, i: (i, j, 0, 0))

    def const(shape):
        return pl.BlockSpec(shape, lambda j, i: (0,) * len(shape), pipeline_mode=pl.Buffered(1))

    pos_t = pl.BlockSpec((ROPE_HALF, rows), lambda j, i: (0, j))
    pos = pl.BlockSpec((rows, LANES), lambda j, i: (j, 0))
    tok_shape = lambda w: jax.ShapeDtypeStruct((b, s, w), BF16)
    tok_t_shape = lambda f: jax.ShapeDtypeStruct((b, s // BLOCK, f, BLOCK), BF16)
    return pl.pallas_call(
        _proj_kernel,
        grid=grid,
        in_specs=[tok(d), const((1, d)), const((d, IN_WIDTH)),
                  const((1, 2 * D_MODEL)), pos_t, pos_t, pos, pos, pos, pos, pos,
                  const((rows, RET_QK_W)), const((rows, RET_QK_W))],
        out_specs=[tok_t(ATT_Q_W), tok(ATT_KV_W), tok_t(ATT_KV_W), tok(RET_QK_W), tok(RET_QK_W), tok(RET_QK_W),
                   tok_t(RET_QK_W), tok(RET_V_W), tok(RET_V_W), tok(D_MODEL), tok(D_MODEL)],
        out_shape=[tok_t_shape(ATT_Q_W), tok_shape(ATT_KV_W), tok_t_shape(ATT_KV_W), tok_shape(RET_QK_W),
                   tok_shape(RET_QK_W), tok_shape(RET_QK_W), tok_t_shape(RET_QK_W), tok_shape(RET_V_W),
                   tok_shape(RET_V_W), tok_shape(D_MODEL), tok_shape(D_MODEL)],
        compiler_params=pltpu.CompilerParams(
            dimension_semantics=("arbitrary", "arbitrary"), vmem_limit_bytes=V7X_VMEM_LIMIT_BYTES),
        name="proj",
    )(x, gain, w_in, b_gates, *tables)


def _mix_kernel(cdec_ref,
                qat_ref, ka_ref, vat_ref, qr_ref, qdr_ref, kr_ref, kdt_ref, vr_ref, sg_ref, ga_ref, gb_ref, x_ref,
                sink_ref, gn_ref, intra_ref, watt_ref, wret_ref, wout_ref,
                h_ref,
                kext_ref, vext_ref, state_ref, att_ref, ret_ref, merged_ref, *, tiles_per_seq, ntiles):
    t = pl.program_id(0)
    ws = t & 1
    rs = 1 - ws
    nblk = x_ref.shape[0] // BLOCK
    first = (jnp.minimum(t, ntiles - 1) % tiles_per_seq) == 0

    @pl.when(t == 0)
    def _():
        att_ref[1] = jnp.zeros(att_ref.shape[1:], BF16)
        ret_ref[1] = jnp.zeros(ret_ref.shape[1:], BF16)

    @pl.when(first)
    def _():
        kext_ref[:BLOCK, :] = jnp.zeros((BLOCK, ATT_KV_W), BF16)
        vext_ref[0] = jnp.zeros((ATT_KV_W, BLOCK), BF16)
        state_ref[...] = jnp.zeros_like(state_ref)

    kext_ref[BLOCK:, :] = ka_ref[...]
    vext_ref[1:] = vat_ref[...]

    from_prev = (lax.broadcasted_iota(jnp.int32, (BLOCK, BLOCK), 0)
                 > lax.broadcasted_iota(jnp.int32, (BLOCK, BLOCK), 1))

    def band_select(a, b):
        pick = lambda v, g: v[:, g * BLOCK:(g + 1) * BLOCK] if hasattr(v, "shape") else v
        return jnp.concatenate([jnp.where(from_prev, pick(a, g), pick(b, g)) for g in range(ATT_GROUP)], axis=1)
    zeros_q = jnp.zeros((ATT_HEAD_DIM, ATT_GROUP * BLOCK), BF16)

    ksl = [slice(h * RET_KEY_DIM, (h + 1) * RET_KEY_DIM) for h in range(RET_HEADS)]
    vsl = [slice(h * RET_VAL_DIM, (h + 1) * RET_VAL_DIM) for h in range(RET_HEADS)]
    rows_of = lambda c: pl.ds(c * BLOCK, BLOCK)

    def first_matmuls(c):
        rows = rows_of(c)
        qt = qat_ref[c]
        kc = kext_ref[pl.ds(c * BLOCK, 2 * BLOCK), :]
        scores = []
        for kv in range(ATT_KV_HEADS):
            qg = jnp.concatenate(
                [qt[(kv * ATT_GROUP + g) * ATT_HEAD_DIM:(kv * ATT_GROUP + g + 1) * ATT_HEAD_DIM] for g in range(ATT_GROUP)],
                axis=1)
            rhs = jnp.concatenate([qg, zeros_q] if kv == 0 else [zeros_q, qg], axis=0)
            scores.append(jnp.dot(kc, rhs, preferred_element_type=F32))
        att = [lax.dot_general(qr_ref[rows, ksl[h]], kr_ref[rows, ksl[h]], NT, preferred_element_type=F32)
               for h in range(RET_HEADS)]
        return scores, att

    def softmax_and_decay(c, stage1, state):
        scores, att = stage1
        probs, inv_den = [], []
        for kv in range(ATT_KV_HEADS):
            sc = scores[kv]
            prev = sc[:BLOCK]
            if c == 0:
                prev = prev + jnp.where(first, NEG_INF, 0.0).astype(F32)
            f = band_select(prev, sc[BLOCK:])
            sink = sink_ref[kv]
            m = jnp.maximum(jnp.max(f, axis=0, keepdims=True), sink)
            e = jnp.exp2(f - m)
            inv_den.append(1.0 / (jnp.sum(e, axis=0, keepdims=True) + jnp.exp2(sink - m)))
            probs.append(jnp.concatenate([band_select(e, 0.0), band_select(0.0, e)], axis=0).astype(BF16))
        att_b = [(att[h] * intra_ref[h]).astype(BF16) for h in range(RET_HEADS)]
        state_b = [state[h].astype(BF16) for h in range(RET_HEADS)]
        return probs, inv_den, att_b, state_b

    def second_matmuls(c, ops):
        probs, inv_den, att_b, state_b = ops
        rows = rows_of(c)
        vt_prev, vt_cur = vext_ref[c], vext_ref[c + 1]
        outs = []
        for kv in range(ATT_KV_HEADS):
            hs = slice(kv * ATT_HEAD_DIM, (kv + 1) * ATT_HEAD_DIM)
            vt = jnp.concatenate([vt_prev[hs], vt_cur[hs]], axis=1)
            outs.append(jnp.dot(vt, probs[kv], preferred_element_type=F32) * inv_den[kv])
        zeros_k = jnp.zeros((RET_KEY_DIM, RET_KEY_DIM), BF16)
        both = [jnp.dot(jnp.concatenate([jnp.concatenate([att_b[h], qdr_ref[rows, ksl[h]]], axis=1),
                                         jnp.concatenate([kdt_ref[c, ksl[h], :], zeros_k], axis=1)], axis=0),
                        jnp.concatenate([vr_ref[rows, vsl[h]], state_b[h]], axis=0), preferred_element_type=F32)
                for h in range(RET_HEADS)]
        ret = [bh[:BLOCK] for bh in both]
        upd = [bh[BLOCK:] for bh in both]
        return outs, ret, upd

    def finish(c, stage2):
        outs, ret, _ = stage2
        rows = rows_of(c)
        ot = jnp.concatenate(outs, axis=0)
        for g in range(ATT_GROUP):
            att_ref[ws, rows, g * LANES:(g + 1) * LANES] = ot[:, g * BLOCK:(g + 1) * BLOCK].T.astype(BF16)
        for h in range(RET_HEADS):
            out = ret[h]
            mu = jnp.mean(out, axis=-1, keepdims=True)
            dev = out - mu
            var = jnp.mean(dev * dev, axis=-1, keepdims=True)
            yn = dev * lax.rsqrt(var + GN_EPS) * gn_ref[:, vsl[h]]
            ret_ref[ws, rows, vsl[h]] = yn.astype(BF16) * sg_ref[rows, vsl[h]]

    def chunk_tail(c, stage1, state):
        stage2 = second_matmuls(c, softmax_and_decay(c, stage1, state))
        finish(c, stage2)
        return [state[h] * cdec_ref[h] + stage2[2][h] for h in range(RET_HEADS)]

    half = D_MODEL // 2
    cols = [slice(0, half), slice(half, D_MODEL)]

    assert nblk == 4
    state = [state_ref[h] for h in range(RET_HEADS)]
    def merged_half(n):
        ya = jnp.dot(att_ref[rs], watt_ref[:, cols[n]], preferred_element_type=F32)
        yr = jnp.dot(ret_ref[rs], wret_ref[:, cols[n]], preferred_element_type=F32)
        merged_ref[:, cols[n]] = ga_ref[:, cols[n]] * ya.astype(BF16) + gb_ref[:, cols[n]] * yr.astype(BF16)

    def out_half(n):
        h_ref[:, cols[n]] = x_ref[:, cols[n]] + jnp.dot(merged_ref[...], wout_ref[:, cols[n]],
                                                        preferred_element_type=F32)

    s0 = first_matmuls(0)
    s1 = first_matmuls(1)
    merged_half(0)
    state = chunk_tail(0, s0, state)
    s2 = first_matmuls(2)
    merged_half(1)
    state = chunk_tail(1, s1, state)
    s3 = first_matmuls(3)
    state = chunk_tail(2, s2, state)
    out_half(0)
    state = chunk_tail(3, s3, state)
    out_half(1)
    for h in range(RET_HEADS):
        state_ref[h] = state[h]

    kext_ref[:BLOCK, :] = kext_ref[nblk * BLOCK:, :]
    vext_ref[0] = vext_ref[nblk]


def _mix_call(proj_out, x, cdec, sinks, gn_gain, intra, w_att_up, w_ret_up, w_out):
    b, s, d = x.shape
    rows = MIX_ROWS
    nblk = rows // BLOCK
    tps = s // rows
    ntiles = b * tps
    grid = (ntiles + 1,)

    chunk_tile = lambda t: jnp.minimum(t, ntiles - 1)
    proj_tile = lambda t: jnp.maximum(t - 1, 0)

    def tok(width, tile):
        return pl.BlockSpec((None, rows, width), lambda t: (tile(t) // tps, tile(t) % tps, 0))

    def tok_t(feat):
        return pl.BlockSpec((None, nblk, feat, BLOCK), lambda t: (chunk_tile(t) // tps, chunk_tile(t) % tps, 0, 0))

    def const(shape):
        return pl.BlockSpec(shape, lambda t: (0,) * len(shape), pipeline_mode=pl.Buffered(1))

    smem = pl.BlockSpec(memory_space=pltpu.SMEM)
    ctok = lambda width: tok(width, chunk_tile)
    return pl.pallas_call(
        functools.partial(_mix_kernel, tiles_per_seq=tps, ntiles=ntiles),
        grid=grid,
        in_specs=[smem,
                  tok_t(ATT_Q_W), ctok(ATT_KV_W), tok_t(ATT_KV_W), ctok(RET_QK_W), ctok(RET_QK_W), ctok(RET_QK_W),
                  tok_t(RET_QK_W), ctok(RET_V_W), ctok(RET_V_W),
                  tok(D_MODEL, proj_tile), tok(D_MODEL, proj_tile), tok(d, proj_tile),
                  const((ATT_KV_HEADS, 1, ATT_GROUP * BLOCK)), const((1, RET_V_W)),
                  const((RET_HEADS, RET_CHUNK, RET_CHUNK)),
                  const((ATT_Q_W, d)), const((RET_V_W, d)), const((d, d))],
        out_specs=tok(d, proj_tile),
        out_shape=jax.ShapeDtypeStruct((b, s, d), F32),
        scratch_shapes=[
            pltpu.VMEM((BLOCK + rows, ATT_KV_W), BF16),
            pltpu.VMEM((1 + nblk, ATT_KV_W, BLOCK), BF16),
            pltpu.VMEM((RET_HEADS, RET_KEY_DIM, RET_VAL_DIM), F32),
            pltpu.VMEM((2, rows, ATT_Q_W), BF16),
            pltpu.VMEM((2, rows, RET_V_W), BF16),
            pltpu.VMEM((rows, d), BF16),
        ],
        compiler_params=pltpu.CompilerParams(
            dimension_semantics=("arbitrary",), vmem_limit_bytes=V7X_VMEM_LIMIT_BYTES),
        name="mix",
    )(cdec, *proj_out, x, sinks, gn_gain, intra, w_att_up, w_ret_up, w_out)


def _ffn_kernel(h_ref, gain_ref, w1_ref, w2_ref, fgain_ref, o_ref, *, final_norm):
    for s in range(h_ref.shape[0] // FFN_SUB_ROWS):
        rs = slice(s * FFN_SUB_ROWS, (s + 1) * FFN_SUB_ROWS)
        h = h_ref[rs, :]
        xb = _rms_scale(h, gain_ref[...]).astype(BF16)
        acc = jnp.zeros(h.shape, F32)
        for c in range(D_FF // FFN_COLS):
            sl = slice(c * FFN_COLS, (c + 1) * FFN_COLS)
            a = jnp.maximum(jnp.dot(xb, w1_ref[:, sl], preferred_element_type=F32), 0.0)
            acc = acc + jnp.dot((a * a).astype(BF16), w2_ref[sl, :], preferred_element_type=F32)
        y = h + acc
        if final_norm:
            y = _rms_scale(y, fgain_ref[...])
        o_ref[rs, :] = y


def _ffn_call(h, gain, w1, w2, fgain, final_norm):
    b, s, d = h.shape
    rows = FFN_ROWS
    grid = (b, s // rows)

    def const(shape):
        return pl.BlockSpec(shape, lambda i, j: (0,) * len(shape), pipeline_mode=pl.Buffered(1))

    tok = pl.BlockSpec((None, rows, d), lambda i, j: (i, j, 0))
    return pl.pallas_call(
        functools.partial(_ffn_kernel, final_norm=final_norm),
        grid=grid,
        in_specs=[tok, const((1, d)), const((d, D_FF)), const((D_FF, d)), const((1, d))],
        out_specs=tok,
        out_shape=jax.ShapeDtypeStruct((b, s, d), F32),
        compiler_params=pltpu.CompilerParams(
            dimension_semantics=("arbitrary", "arbitrary"), vmem_limit_bytes=V7X_VMEM_LIMIT_BYTES),
        name="ffn",
    )(h, gain, w1, w2, fgain)


def _decay_terms():
    h, c = RET_HEADS, RET_CHUNK
    log_gamma = jnp.log1p(-jnp.exp2(-5.0 - jnp.arange(h, dtype=F32)))
    idx = jnp.arange(c, dtype=F32)
    diff = idx[:, None] - idx[None, :]
    intra = jnp.where(diff >= 0, jnp.exp(jnp.maximum(diff, 0.0) * log_gamma[:, None, None]), 0.0)
    q_decay = jnp.exp((idx + 1.0)[None, :] * log_gamma[:, None])
    k_decay = jnp.exp((c - 1.0 - idx)[None, :] * log_gamma[:, None])
    chunk_decay = jnp.exp(c * log_gamma)
    return intra, q_decay, k_decay, chunk_decay


def _position_tables(seq_len, q_decay, k_decay):
    pos = jnp.arange(seq_len, dtype=F32)

    def cos_sin(dim, theta):
        inv_freq = theta ** (-jnp.arange(0, dim, 2, dtype=F32) / dim)
        ang = pos[:, None] * inv_freq[None, :]
        return jnp.cos(ang), jnp.sin(ang)

    cos, sin = cos_sin(ROPE_DIM, ROPE_THETA)
    pad = ATT_HEAD_DIM - ROPE_DIM
    zeros_half = jnp.zeros_like(sin)
    one_head = lambda parts: jnp.tile(jnp.concatenate(parts, axis=-1), (1, LANES // ATT_HEAD_DIM))
    ca = one_head([cos, cos, jnp.ones((seq_len, pad), F32)])
    s1 = one_head([-sin, zeros_half, jnp.zeros((seq_len, pad), F32)])
    s2 = one_head([zeros_half, sin, jnp.zeros((seq_len, pad), F32)])
    cos_r, sin_r = cos_sin(RET_KEY_DIM, RET_ROT_BASE)
    cr = jnp.concatenate([cos_r, cos_r], axis=-1)
    sr = jnp.concatenate([-sin_r, sin_r], axis=-1)

    def per_row(decay):
        t = jnp.repeat(decay.T, RET_KEY_DIM, axis=1)
        return jnp.tile(t, (PROJ_ROWS // RET_CHUNK, 1))

    return cos.T, sin.T, ca, s1, s2, cr, sr, per_row(q_decay), per_row(k_decay)


def kernel(x, norm_mix_gain, w_in, b_gates, attn_sinks, ret_gn_gain, w_att_up, w_ret_up, w_out,
           norm_mlp_gain, w_ff1, w_ff2, norm_final_gain):
    b, s, d = x.shape
    depth = w_in.shape[0]
    assert d == D_MODEL and s % MIX_ROWS == 0 and s % PROJ_ROWS == 0 and s % FFN_ROWS == 0
    intra, q_decay, k_decay, cdec = _decay_terms()
    tables = _position_tables(s, q_decay, k_decay)
    row = lambda v: v.reshape(1, -1).astype(F32)
    att_perm = np.arange(ATT_Q_W).reshape(ATT_KV_HEADS, ATT_GROUP, ATT_HEAD_DIM).transpose(1, 0, 2).reshape(-1)
    h = x
    for l in range(depth):
        w = w_in[l].astype(BF16)
        proj_out = _proj_call(h, row(norm_mix_gain[l]), w, row(b_gates[l]), tables)
        sinks = jnp.repeat((attn_sinks[l].astype(F32) * LOG2E).reshape(ATT_KV_HEADS, 1, ATT_GROUP), BLOCK, axis=-1)
        h = _mix_call(proj_out, h, cdec, sinks, row(ret_gn_gain[l]), intra,
                      w_att_up[l][att_perm].astype(BF16), w_ret_up[l].astype(BF16), w_out[l].astype(BF16))
        h = _ffn_call(h, row(norm_mlp_gain[l]), w_ff1[l].astype(BF16), w_ff2[l].astype(BF16),
                      row(norm_final_gain), final_norm=(l == depth - 1))
    return h
```

```python
import functools

import jax
import jax.numpy as jnp
import numpy as np
from jax import lax
from jax.experimental import pallas as pl
from jax.experimental.pallas import tpu as pltpu

D_MODEL = 1024
ATT_HEADS = 8
ATT_KV_HEADS = 2
ATT_GROUP = ATT_HEADS // ATT_KV_HEADS
ATT_HEAD_DIM = 64
WINDOW = 128
ATT_BLOCK = 128
ROPE_DIM = ATT_HEAD_DIM // 4
ROPE_HALF = ROPE_DIM // 2
ROPE_THETA = 500000.0
RET_HEADS = 4
RET_KEY_DIM = 128
RET_VAL_DIM = 256
RET_CHUNK = 128
RET_ROT_BASE = 10000.0
D_FF = 4 * D_MODEL
NORM_EPS = 1e-6
GN_EPS = 1e-6
NEG_INF = -1e30
LOG2E = 1.4426950408889634

ATT_Q_W = ATT_HEADS * ATT_HEAD_DIM
ATT_KV_W = ATT_KV_HEADS * ATT_HEAD_DIM
RET_QK_W = RET_HEADS * RET_KEY_DIM
RET_V_W = RET_HEADS * RET_VAL_DIM
IN_SPLITS = (ATT_Q_W, ATT_KV_W, ATT_KV_W, RET_QK_W, RET_QK_W, RET_V_W, RET_V_W, D_MODEL, D_MODEL)
IN_WIDTH = sum(IN_SPLITS)
(OFF_QA, OFF_KA, OFF_VA, OFF_QR, OFF_KR, OFF_VR, OFF_GR, OFF_GA, OFF_GB) = (
    int(v) for v in np.concatenate([[0], np.cumsum(IN_SPLITS)[:-1]]))

LANES = 128
SUBLANES = 8
V7X_VMEM_LIMIT_BYTES = 56 * 1024 * 1024

BLOCK = 128
PROJ_ROWS = 512
PROJ_COLS = 512
MIX_ROWS = 512
FFN_ROWS = 1024
FFN_SUB_ROWS = 512
FFN_COLS = 512

BF16 = jnp.bfloat16
F32 = jnp.float32

assert WINDOW == ATT_BLOCK == RET_CHUNK == BLOCK == LANES
assert RET_KEY_DIM == LANES and ATT_KV_W == LANES and ROPE_HALF == SUBLANES
assert PROJ_ROWS % BLOCK == 0 and MIX_ROWS % BLOCK == 0

NT = (((1,), (1,)), ((), ()))


def _sigmoid(x):
    return 0.5 * jnp.tanh(0.5 * x) + 0.5


def _silu(x):
    return x * _sigmoid(x)


def _rms_scale(x, gain):
    ms = jnp.mean(x * x, axis=-1, keepdims=True)
    return (x * lax.rsqrt(ms + NORM_EPS)) * gain


def _proj_kernel(x_ref, gain_ref, w_ref, bg_ref,
                 cat_ref, sat_ref, ca_ref, s1_ref, s2_ref, cr_ref, sr_ref, qdec_ref, kdec_ref,
                 qat_ref, ka_ref, vat_ref, qr_ref, qdr_ref, kr_ref, kdt_ref, vr_ref, sg_ref, ga_ref, gb_ref):
    rows = x_ref.shape[0]
    nblk = rows // BLOCK
    xb = _rms_scale(x_ref[...], gain_ref[...]).astype(BF16)

    def proj(off, width):
        return jnp.dot(xb, w_ref[:, off:off + width], preferred_element_type=F32)

    def store_blocks(ref, feat, val):
        for c in range(nblk):
            ref[c, feat, :] = val[:, c * BLOCK:(c + 1) * BLOCK].astype(BF16)

    zt = proj(OFF_QA, ATT_Q_W).T
    cat, sat = cat_ref[...], sat_ref[...]
    for hd in range(ATT_HEADS):
        b0 = hd * ATT_HEAD_DIM
        x1, x2 = zt[b0:b0 + ROPE_HALF], zt[b0 + ROPE_HALF:b0 + ROPE_DIM]
        head = jnp.concatenate([x1 * cat - x2 * sat, x2 * cat + x1 * sat, zt[b0 + ROPE_DIM:b0 + ATT_HEAD_DIM]], axis=0)
        store_blocks(qat_ref, slice(b0, b0 + ATT_HEAD_DIM), head * (ATT_HEAD_DIM ** -0.5 * LOG2E))
    store_blocks(vat_ref, slice(None), proj(OFF_VA, ATT_KV_W).T)
    z = proj(OFF_KA, ATT_KV_W)
    ka_ref[...] = (z * ca_ref[...] + pltpu.roll(z, LANES - ROPE_HALF, 1) * s1_ref[...]
                   + pltpu.roll(z, ROPE_HALF, 1) * s2_ref[...]).astype(BF16)

    cr, sr = cr_ref[...], sr_ref[...]

    def rope_ret(zh):
        return zh * cr + pltpu.roll(zh, RET_KEY_DIM // 2, 1) * sr

    z = proj(OFF_QR, RET_QK_W)
    for h in range(RET_HEADS):
        sl = slice(h * RET_KEY_DIM, (h + 1) * RET_KEY_DIM)
        q = rope_ret(z[:, sl])
        qr_ref[:, sl] = q.astype(BF16)
        qdr_ref[:, sl] = (q * qdec_ref[:, sl]).astype(BF16)
    z = proj(OFF_KR, RET_QK_W)
    for h in range(RET_HEADS):
        sl = slice(h * RET_KEY_DIM, (h + 1) * RET_KEY_DIM)
        k = rope_ret(z[:, sl]) * (RET_KEY_DIM ** -0.5)
        kr_ref[:, sl] = k.astype(BF16)
        store_blocks(kdt_ref, sl, (k * kdec_ref[:, sl]).T)
    for c in range(RET_V_W // PROJ_COLS):
        sl = slice(c * PROJ_COLS, (c + 1) * PROJ_COLS)
        sg_ref[:, sl] = _silu(proj(OFF_GR + c * PROJ_COLS, PROJ_COLS)).astype(BF16)
    for c in range(D_MODEL // PROJ_COLS):
        sl = slice(c * PROJ_COLS, (c + 1) * PROJ_COLS)
        g = proj(OFF_GA + c * PROJ_COLS, PROJ_COLS) + bg_ref[:, sl]
        ga_ref[:, sl] = _sigmoid(g).astype(BF16)
    for c in range(D_MODEL // PROJ_COLS):
        sl = slice(c * PROJ_COLS, (c + 1) * PROJ_COLS)
        g = proj(OFF_GB + c * PROJ_COLS, PROJ_COLS) + bg_ref[:, D_MODEL + c * PROJ_COLS:D_MODEL + (c + 1) * PROJ_COLS]
        gb_ref[:, sl] = _sigmoid(g).astype(BF16)
    for c in range(RET_V_W // PROJ_COLS):
        sl = slice(c * PROJ_COLS, (c + 1) * PROJ_COLS)
        vr_ref[:, sl] = proj(OFF_VR + c * PROJ_COLS, PROJ_COLS).astype(BF16)


def _proj_call(x, gain, w_in, b_gates, tables):
    b, s, d = x.shape
    rows = PROJ_ROWS
    nblk = rows // BLOCK
    grid = (s // rows, b)

    def tok(width):
        return pl.BlockSpec((None, rows, width), lambda j, i: (i, j, 0))

    def tok_t(feat):
        return pl.BlockSpec((None, nblk, feat, BLOCK), lambda j, i: (i, j, 0, 0))

    def const(shape):
        return pl.BlockSpec(shape, lambda j, i: (0,) * len(shape), pipeline_mode=pl.Buffered(1))

    pos_t = pl.BlockSpec((ROPE_HALF, rows), lambda j, i: (0, j))
    pos = pl.BlockSpec((rows, LANES), lambda j, i: (j, 0))
    tok_shape = lambda w: jax.ShapeDtypeStruct((b, s, w), BF16)
    tok_t_shape = lambda f: jax.ShapeDtypeStruct((b, s // BLOCK, f, BLOCK), BF16)
    return pl.pallas_call(
        _proj_kernel,
        grid=grid,
        in_specs=[tok(d), const((1, d)), const((d, IN_WIDTH)),
                  const((1, 2 * D_MODEL)), pos_t, pos_t, pos, pos, pos, pos, pos,
                  const((rows, RET_QK_W)), const((rows, RET_QK_W))],
        out_specs=[tok_t(ATT_Q_W), tok(ATT_KV_W), tok_t(ATT_KV_W), tok(RET_QK_W), tok(RET_QK_W), tok(RET_QK_W),
                   tok_t(RET_QK_W), tok(RET_V_W), tok(RET_V_W), tok(D_MODEL), tok(D_MODEL)],
        out_shape=[tok_t_shape(ATT_Q_W), tok_shape(ATT_KV_W), tok_t_shape(ATT_KV_W), tok_shape(RET_QK_W),
                   tok_shape(RET_QK_W), tok_shape(RET_QK_W), tok_t_shape(RET_QK_W), tok_shape(RET_V_W),
                   tok_shape(RET_V_W), tok_shape(D_MODEL), tok_shape(D_MODEL)],
        compiler_params=pltpu.CompilerParams(
            dimension_semantics=("arbitrary", "arbitrary"), vmem_limit_bytes=V7X_VMEM_LIMIT_BYTES),
        name="proj",
    )(x, gain, w_in, b_gates, *tables)


def _mix_kernel(cdec_ref,
                qat_ref, ka_ref, vat_ref, qr_ref, qdr_ref, kr_ref, kdt_ref, vr_ref, sg_ref, ga_ref, gb_ref, x_ref,
                sink_ref, gn_ref, intra_ref, watt_ref, wret_ref, wout_ref,
                h_ref,
                kext_ref, vext_ref, state_ref, att_ref, ret_ref, merged_ref, *, tiles_per_seq, ntiles):
    t = pl.program_id(0)
    ws = t & 1
    rs = 1 - ws
    nblk = x_ref.shape[0] // BLOCK
    first = (jnp.minimum(t, ntiles - 1) % tiles_per_seq) == 0

    @pl.when(t == 0)
    def _():
        att_ref[1] = jnp.zeros(att_ref.shape[1:], BF16)
        ret_ref[1] = jnp.zeros(ret_ref.shape[1:], BF16)

    @pl.when(first)
    def _():
        kext_ref[:BLOCK, :] = jnp.zeros((BLOCK, ATT_KV_W), BF16)
        vext_ref[0] = jnp.zeros((ATT_KV_W, BLOCK), BF16)
        state_ref[...] = jnp.zeros_like(state_ref)

    kext_ref[BLOCK:, :] = ka_ref[...]
    vext_ref[1:] = vat_ref[...]

    from_prev = (lax.broadcasted_iota(jnp.int32, (BLOCK, BLOCK), 0)
                 > lax.broadcasted_iota(jnp.int32, (BLOCK, BLOCK), 1))

    def band_select(a, b):
        pick = lambda v, g: v[:, g * BLOCK:(g + 1) * BLOCK] if hasattr(v, "shape") else v
        return jnp.concatenate([jnp.where(from_prev, pick(a, g), pick(b, g)) for g in range(ATT_GROUP)], axis=1)
    zeros_q = jnp.zeros((ATT_HEAD_DIM, ATT_GROUP * BLOCK), BF16)

    ksl = [slice(h * RET_KEY_DIM, (h + 1) * RET_KEY_DIM) for h in range(RET_HEADS)]
    vsl = [slice(h * RET_VAL_DIM, (h + 1) * RET_VAL_DIM) for h in range(RET_HEADS)]
    rows_of = lambda c: pl.ds(c * BLOCK, BLOCK)

    def first_matmuls(c):
        rows = rows_of(c)
        qt = qat_ref[c]
        kc = kext_ref[pl.ds(c * BLOCK, 2 * BLOCK), :]
        scores = []
        for kv in range(ATT_KV_HEADS):
            qg = jnp.concatenate(
                [qt[(kv * ATT_GROUP + g) * ATT_HEAD_DIM:(kv * ATT_GROUP + g + 1) * ATT_HEAD_DIM] for g in range(ATT_GROUP)],
                axis=1)
            rhs = jnp.concatenate([qg, zeros_q] if kv == 0 else [zeros_q, qg], axis=0)
            scores.append(jnp.dot(kc, rhs, preferred_element_type=F32))
        zeros_k = jnp.zeros((BLOCK, RET_KEY_DIM), BF16)
        att = []
        for h in range(0, RET_HEADS, 2):
            pair = slice(h * RET_KEY_DIM, (h + 2) * RET_KEY_DIM)
            kdiag = jnp.concatenate([jnp.concatenate([kr_ref[rows, ksl[h]], zeros_k], axis=1),
                                     jnp.concatenate([zeros_k, kr_ref[rows, ksl[h + 1]]], axis=1)], axis=0)
            both = lax.dot_general(qr_ref[rows, pair], kdiag, NT, preferred_element_type=F32)
            att += [both[:, :BLOCK], both[:, BLOCK:]]
        return scores, att

    def softmax_and_decay(c, stage1, state):
        scores, att = stage1
        probs, inv_den = [], []
        for kv in range(ATT_KV_HEADS):
            sc = scores[kv]
            prev = sc[:BLOCK]
            if c == 0:
                prev = prev + jnp.where(first, NEG_INF, 0.0).astype(F32)
            f = band_select(prev, sc[BLOCK:])
            sink = sink_ref[kv]
            m = jnp.maximum(jnp.max(f, axis=0, keepdims=True), sink)
            e = jnp.exp2(f - m)
            inv_den.append(1.0 / (jnp.sum(e, axis=0, keepdims=True) + jnp.exp2(sink - m)))
            probs.append(jnp.concatenate([band_select(e, 0.0), band_select(0.0, e)], axis=0).astype(BF16))
        att_b = [(att[h] * intra_ref[h]).astype(BF16) for h in range(RET_HEADS)]
        state_b = [state[h].astype(BF16) for h in range(RET_HEADS)]
        return probs, inv_den, att_b, state_b

    def second_matmuls(c, ops):
        probs, inv_den, att_b, state_b = ops
        rows = rows_of(c)
        vt_prev, vt_cur = vext_ref[c], vext_ref[c + 1]
        outs = []
        for kv in range(ATT_KV_HEADS):
            hs = slice(kv * ATT_HEAD_DIM, (kv + 1) * ATT_HEAD_DIM)
            vt = jnp.concatenate([vt_prev[hs], vt_cur[hs]], axis=1)
            outs.append(jnp.dot(vt, probs[kv], preferred_element_type=F32) * inv_den[kv])
        zeros_k = jnp.zeros((RET_KEY_DIM, RET_KEY_DIM), BF16)
        both = [jnp.dot(jnp.concatenate([jnp.concatenate([att_b[h], qdr_ref[rows, ksl[h]]], axis=1),
                                         jnp.concatenate([kdt_ref[c, ksl[h], :], zeros_k], axis=1)], axis=0),
                        jnp.concatenate([vr_ref[rows, vsl[h]], state_b[h]], axis=0), preferred_element_type=F32)
                for h in range(RET_HEADS)]
        ret = [bh[:BLOCK] for bh in both]
        upd = [bh[BLOCK:] for bh in both]
        return outs, ret, upd

    def finish(c, stage2):
        outs, ret, _ = stage2
        rows = rows_of(c)
        ot = jnp.concatenate(outs, axis=0)
        for g in range(ATT_GROUP):
            att_ref[ws, rows, g * LANES:(g + 1) * LANES] = ot[:, g * BLOCK:(g + 1) * BLOCK].T.astype(BF16)
        for h in range(RET_HEADS):
            out = ret[h]
            mu = jnp.mean(out, axis=-1, keepdims=True)
            dev = out - mu
            var = jnp.mean(dev * dev, axis=-1, keepdims=True)
            yn = dev * lax.rsqrt(var + GN_EPS) * gn_ref[:, vsl[h]]
            ret_ref[ws, rows, vsl[h]] = yn.astype(BF16) * sg_ref[rows, vsl[h]]

    def chunk_tail(c, stage1, state):
        stage2 = second_matmuls(c, softmax_and_decay(c, stage1, state))
        finish(c, stage2)
        return [state[h] * cdec_ref[h] + stage2[2][h] for h in range(RET_HEADS)]

    half = D_MODEL // 2
    cols = [slice(0, half), slice(half, D_MODEL)]

    assert nblk == 4
    state = [state_ref[h] for h in range(RET_HEADS)]
    def merged_half(n):
        ya = jnp.dot(att_ref[rs], watt_ref[:, cols[n]], preferred_element_type=F32)
        yr = jnp.dot(ret_ref[rs], wret_ref[:, cols[n]], preferred_element_type=F32)
        merged_ref[:, cols[n]] = ga_ref[:, cols[n]] * ya.astype(BF16) + gb_ref[:, cols[n]] * yr.astype(BF16)

    def out_half(n):
        h_ref[:, cols[n]] = x_ref[:, cols[n]] + jnp.dot(merged_ref[...], wout_ref[:, cols[n]],
                                                        preferred_element_type=F32)

    s0 = first_matmuls(0)
    s1 = first_matmuls(1)
    merged_half(0)
    state = chunk_tail(0, s0, state)
    s2 = first_matmuls(2)
    merged_half(1)
    state = chunk_tail(1, s1, state)
    s3 = first_matmuls(3)
    state = chunk_tail(2, s2, state)
    out_half(0)
    state = chunk_tail(3, s3, state)
    out_half(1)
    for h in range(RET_HEADS):
        state_ref[h] = state[h]

    kext_ref[:BLOCK, :] = kext_ref[nblk * BLOCK:, :]
    vext_ref[0] = vext_ref[nblk]


def _mix_call(proj_out, x, cdec, sinks, gn_gain, intra, w_att_up, w_ret_up, w_out):
    b, s, d = x.shape
    rows = MIX_ROWS
    nblk = rows // BLOCK
    tps = s // rows
    ntiles = b * tps
    grid = (ntiles + 1,)

    chunk_tile = lambda t: jnp.minimum(t, ntiles - 1)
    proj_tile = lambda t: jnp.maximum(t - 1, 0)

    def tok(width, tile):
        return pl.BlockSpec((None, rows, width), lambda t: (tile(t) // tps, tile(t) % tps, 0))

    def tok_t(feat):
        return pl.BlockSpec((None, nblk, feat, BLOCK), lambda t: (chunk_tile(t) // tps, chunk_tile(t) % tps, 0, 0))

    def const(shape):
        return pl.BlockSpec(shape, lambda t: (0,) * len(shape), pipeline_mode=pl.Buffered(1))

    smem = pl.BlockSpec(memory_space=pltpu.SMEM)
    ctok = lambda width: tok(width, chunk_tile)
    return pl.pallas_call(
        functools.partial(_mix_kernel, tiles_per_seq=tps, ntiles=ntiles),
        grid=grid,
        in_specs=[smem,
                  tok_t(ATT_Q_W), ctok(ATT_KV_W), tok_t(ATT_KV_W), ctok(RET_QK_W), ctok(RET_QK_W), ctok(RET_QK_W),
                  tok_t(RET_QK_W), ctok(RET_V_W), ctok(RET_V_W),
                  tok(D_MODEL, proj_tile), tok(D_MODEL, proj_tile), tok(d, proj_tile),
                  const((ATT_KV_HEADS, 1, ATT_GROUP * BLOCK)), const((1, RET_V_W)),
                  const((RET_HEADS, RET_CHUNK, RET_CHUNK)),
                  const((ATT_Q_W, d)), const((RET_V_W, d)), const((d, d))],
        out_specs=tok(d, proj_tile),
        out_shape=jax.ShapeDtypeStruct((b, s, d), F32),
        scratch_shapes=[
            pltpu.VMEM((BLOCK + rows, ATT_KV_W), BF16),
            pltpu.VMEM((1 + nblk, ATT_KV_W, BLOCK), BF16),
            pltpu.VMEM((RET_HEADS, RET_KEY_DIM, RET_VAL_DIM), F32),
            pltpu.VMEM((2, rows, ATT_Q_W), BF16),
            pltpu.VMEM((2, rows, RET_V_W), BF16),
            pltpu.VMEM((rows, d), BF16),
        ],
        compiler_params=pltpu.CompilerParams(
            dimension_semantics=("arbitrary",), vmem_limit_bytes=V7X_VMEM_LIMIT_BYTES),
        name="mix",
    )(cdec, *proj_out, x, sinks, gn_gain, intra, w_att_up, w_ret_up, w_out)


def _ffn_kernel(h_ref, gain_ref, w1_ref, w2_ref, fgain_ref, o_ref, *, final_norm):
    for s in range(h_ref.shape[0] // FFN_SUB_ROWS):
        rs = slice(s * FFN_SUB_ROWS, (s + 1) * FFN_SUB_ROWS)
        h = h_ref[rs, :]
        xb = _rms_scale(h, gain_ref[...]).astype(BF16)
        acc = jnp.zeros(h.shape, F32)
        for c in range(D_FF // FFN_COLS):
            sl = slice(c * FFN_COLS, (c + 1) * FFN_COLS)
            a = jnp.maximum(jnp.dot(xb, w1_ref[:, sl], preferred_element_type=F32), 0.0)
            acc = acc + jnp.dot((a * a).astype(BF16), w2_ref[sl, :], preferred_element_type=F32)
        y = h + acc
        if final_norm:
            y = _rms_scale(y, fgain_ref[...])
        o_ref[rs, :] = y


def _ffn_call(h, gain, w1, w2, fgain, final_norm):
    b, s, d = h.shape
    rows = FFN_ROWS
    grid = (b, s // rows)

    def const(shape):
        return pl.BlockSpec(shape, lambda i, j: (0,) * len(shape), pipeline_mode=pl.Buffered(1))

    tok = pl.BlockSpec((None, rows, d), lambda i, j: (i, j, 0))
    return pl.pallas_call(
        functools.partial(_ffn_kernel, final_norm=final_norm),
        grid=grid,
        in_specs=[tok, const((1, d)), const((d, D_FF)), const((D_FF, d)), const((1, d))],
        out_specs=tok,
        out_shape=jax.ShapeDtypeStruct((b, s, d), F32),
        compiler_params=pltpu.CompilerParams(
            dimension_semantics=("arbitrary", "arbitrary"), vmem_limit_bytes=V7X_VMEM_LIMIT_BYTES),
        name="ffn",
    )(h, gain, w1, w2, fgain)


def _decay_terms():
    h, c = RET_HEADS, RET_CHUNK
    log_gamma = jnp.log1p(-jnp.exp2(-5.0 - jnp.arange(h, dtype=F32)))
    idx = jnp.arange(c, dtype=F32)
    diff = idx[:, None] - idx[None, :]
    intra = jnp.where(diff >= 0, jnp.exp(jnp.maximum(diff, 0.0) * log_gamma[:, None, None]), 0.0)
    q_decay = jnp.exp((idx + 1.0)[None, :] * log_gamma[:, None])
    k_decay = jnp.exp((c - 1.0 - idx)[None, :] * log_gamma[:, None])
    chunk_decay = jnp.exp(c * log_gamma)
    return intra, q_decay, k_decay, chunk_decay


def _position_tables(seq_len, q_decay, k_decay):
    pos = jnp.arange(seq_len, dtype=F32)

    def cos_sin(dim, theta):
        inv_freq = theta ** (-jnp.arange(0, dim, 2, dtype=F32) / dim)
        ang = pos[:, None] * inv_freq[None, :]
        return jnp.cos(ang), jnp.sin(ang)

    cos, sin = cos_sin(ROPE_DIM, ROPE_THETA)
    pad = ATT_HEAD_DIM - ROPE_DIM
    zeros_half = jnp.zeros_like(sin)
    one_head = lambda parts: jnp.tile(jnp.concatenate(parts, axis=-1), (1, LANES // ATT_HEAD_DIM))
    ca = one_head([cos, cos, jnp.ones((seq_len, pad), F32)])
    s1 = one_head([-sin, zeros_half, jnp.zeros((seq_len, pad), F32)])
    s2 = one_head([zeros_half, sin, jnp.zeros((seq_len, pad), F32)])
    cos_r, sin_r = cos_sin(RET_KEY_DIM, RET_ROT_BASE)
    cr = jnp.concatenate([cos_r, cos_r], axis=-1)
    sr = jnp.concatenate([-sin_r, sin_r], axis=-1)

    def per_row(decay):
        t = jnp.repeat(decay.T, RET_KEY_DIM, axis=1)
        return jnp.tile(t, (PROJ_ROWS // RET_CHUNK, 1))

    return cos.T, sin.T, ca, s1, s2, cr, sr, per_row(q_decay), per_row(k_decay)


def kernel(x, norm_mix_gain, w_in, b_gates, attn_sinks, ret_gn_gain, w_att_up, w_ret_up, w_out,
           norm_mlp_gain, w_ff1, w_ff2, norm_final_gain):
    b, s, d = x.shape
    depth = w_in.shape[0]
    assert d == D_MODEL and s % MIX_ROWS == 0 and s % PROJ_ROWS == 0 and s % FFN_ROWS == 0
    intra, q_decay, k_decay, cdec = _decay_terms()
    tables = _position_tables(s, q_decay, k_decay)
    row = lambda v: v.reshape(1, -1).astype(F32)
    att_perm = np.arange(ATT_Q_W).reshape(ATT_KV_HEADS, ATT_GROUP, ATT_HEAD_DIM).transpose(1, 0, 2).reshape(-1)
    h = x
    for l in range(depth):
        w = w_in[l].astype(BF16)
        proj_out = _proj_call(h, row(norm_mix_gain[l]), w, row(b_gates[l]), tables)
        sinks = jnp.repeat((attn_sinks[l].astype(F32) * LOG2E).reshape(ATT_KV_HEADS, 1, ATT_GROUP), BLOCK, axis=-1)
        h = _mix_call(proj_out, h, cdec, sinks, row(ret_gn_gain[l]), intra,
                      w_att_up[l][att_perm].astype(BF16), w_ret_up[l].astype(BF16), w_out[l].astype(BF16))
        h = _ffn_call(h, row(norm_mlp_gain[l]), w_ff1[l].astype(BF16), w_ff2[l].astype(BF16),
                      row(norm_final_gain), final_norm=(l == depth - 1))
    return h
```

```python
import functools

import jax
import jax.numpy as jnp
import numpy as np
from jax import lax
from jax.experimental import pallas as pl
from jax.experimental.pallas import tpu as pltpu

D_MODEL = 1024
ATT_HEADS = 8
ATT_KV_HEADS = 2
ATT_GROUP = ATT_HEADS // ATT_KV_HEADS
ATT_HEAD_DIM = 64
WINDOW = 128
ATT_BLOCK = 128
ROPE_DIM = ATT_HEAD_DIM // 4
ROPE_HALF = ROPE_DIM // 2
ROPE_THETA = 500000.0
RET_HEADS = 4
RET_KEY_DIM = 128
RET_VAL_DIM = 256
RET_CHUNK = 128
RET_ROT_BASE = 10000.0
D_FF = 4 * D_MODEL
NORM_EPS = 1e-6
GN_EPS = 1e-6
NEG_INF = -1e30
LOG2E = 1.4426950408889634

ATT_Q_W = ATT_HEADS * ATT_HEAD_DIM
ATT_KV_W = ATT_KV_HEADS * ATT_HEAD_DIM
RET_QK_W = RET_HEADS * RET_KEY_DIM
RET_V_W = RET_HEADS * RET_VAL_DIM
IN_SPLITS = (ATT_Q_W, ATT_KV_W, ATT_KV_W, RET_QK_W, RET_QK_W, RET_V_W, RET_V_W, D_MODEL, D_MODEL)
IN_WIDTH = sum(IN_SPLITS)
(OFF_QA, OFF_KA, OFF_VA, OFF_QR, OFF_KR, OFF_VR, OFF_GR, OFF_GA, OFF_GB) = (
    int(v) for v in np.concatenate([[0], np.cumsum(IN_SPLITS)[:-1]]))

LANES = 128
SUBLANES = 8
V7X_VMEM_LIMIT_BYTES = 56 * 1024 * 1024

BLOCK = 128
PROJ_ROWS = 512
PROJ_COLS = 512
MIX_ROWS = 512
FFN_ROWS = 1024
FFN_SUB_ROWS = 512
FFN_COLS = 512

BF16 = jnp.bfloat16
F32 = jnp.float32

assert WINDOW == ATT_BLOCK == RET_CHUNK == BLOCK == LANES
assert RET_KEY_DIM == LANES and ATT_KV_W == LANES and ROPE_HALF == SUBLANES
assert PROJ_ROWS % BLOCK == 0 and MIX_ROWS % BLOCK == 0


def _sigmoid(x):
    return 0.5 * jnp.tanh(0.5 * x) + 0.5


def _silu(x):
    return x * _sigmoid(x)


def _rms_scale(x, gain):
    ms = jnp.mean(x * x, axis=-1, keepdims=True)
    return (x * lax.rsqrt(ms + NORM_EPS)) * gain


def _proj_kernel(x_ref, gain_ref, w_ref, bg_ref,
                 cat_ref, sat_ref, ca_ref, s1_ref, s2_ref, cr_ref, sr_ref, qdec_ref, kdec_ref,
                 qat_ref, ka_ref, vat_ref, qdr_ref, kdt_ref, vr_ref, sg_ref, ga_ref, gb_ref):
    rows = x_ref.shape[0]
    nblk = rows // BLOCK
    xb = _rms_scale(x_ref[...], gain_ref[...]).astype(BF16)

    def proj(off, width):
        return jnp.dot(xb, w_ref[:, off:off + width], preferred_element_type=F32)

    def store_blocks(ref, feat, val):
        for c in range(nblk):
            ref[c, feat, :] = val[:, c * BLOCK:(c + 1) * BLOCK].astype(BF16)

    zt = proj(OFF_QA, ATT_Q_W).T
    cat, sat = cat_ref[...], sat_ref[...]
    for hd in range(ATT_HEADS):
        b0 = hd * ATT_HEAD_DIM
        x1, x2 = zt[b0:b0 + ROPE_HALF], zt[b0 + ROPE_HALF:b0 + ROPE_DIM]
        head = jnp.concatenate([x1 * cat - x2 * sat, x2 * cat + x1 * sat, zt[b0 + ROPE_DIM:b0 + ATT_HEAD_DIM]], axis=0)
        store_blocks(qat_ref, slice(b0, b0 + ATT_HEAD_DIM), head * (ATT_HEAD_DIM ** -0.5 * LOG2E))
    store_blocks(vat_ref, slice(None), proj(OFF_VA, ATT_KV_W).T)
    z = proj(OFF_KA, ATT_KV_W)
    ka_ref[...] = (z * ca_ref[...] + pltpu.roll(z, LANES - ROPE_HALF, 1) * s1_ref[...]
                   + pltpu.roll(z, ROPE_HALF, 1) * s2_ref[...]).astype(BF16)

    cr, sr = cr_ref[...], sr_ref[...]

    def rope_ret(zh):
        return zh * cr + pltpu.roll(zh, RET_KEY_DIM // 2, 1) * sr

    z = proj(OFF_QR, RET_QK_W)
    for h in range(RET_HEADS):
        sl = slice(h * RET_KEY_DIM, (h + 1) * RET_KEY_DIM)
        qdr_ref[:, sl] = (rope_ret(z[:, sl]) * qdec_ref[:, sl]).astype(BF16)
    z = proj(OFF_KR, RET_QK_W)
    for h in range(RET_HEADS):
        sl = slice(h * RET_KEY_DIM, (h + 1) * RET_KEY_DIM)
        k = rope_ret(z[:, sl]) * (RET_KEY_DIM ** -0.5)
        store_blocks(kdt_ref, sl, (k * kdec_ref[:, sl]).T)
    for c in range(RET_V_W // PROJ_COLS):
        sl = slice(c * PROJ_COLS, (c + 1) * PROJ_COLS)
        sg_ref[:, sl] = _silu(proj(OFF_GR + c * PROJ_COLS, PROJ_COLS)).astype(BF16)
    for c in range(D_MODEL // PROJ_COLS):
        sl = slice(c * PROJ_COLS, (c + 1) * PROJ_COLS)
        g = proj(OFF_GA + c * PROJ_COLS, PROJ_COLS) + bg_ref[:, sl]
        ga_ref[:, sl] = _sigmoid(g).astype(BF16)
    for c in range(D_MODEL // PROJ_COLS):
        sl = slice(c * PROJ_COLS, (c + 1) * PROJ_COLS)
        g = proj(OFF_GB + c * PROJ_COLS, PROJ_COLS) + bg_ref[:, D_MODEL + c * PROJ_COLS:D_MODEL + (c + 1) * PROJ_COLS]
        gb_ref[:, sl] = _sigmoid(g).astype(BF16)
    for c in range(RET_V_W // PROJ_COLS):
        sl = slice(c * PROJ_COLS, (c + 1) * PROJ_COLS)
        vr_ref[:, sl] = proj(OFF_VR + c * PROJ_COLS, PROJ_COLS).astype(BF16)


def _proj_call(x, gain, w_in, b_gates, tables):
    b, s, d = x.shape
    rows = PROJ_ROWS
    nblk = rows // BLOCK
    grid = (s // rows, b)

    def tok(width):
        return pl.BlockSpec((None, rows, width), lambda j, i: (i, j, 0))

    def tok_t(feat):
        return pl.BlockSpec((None, nblk, feat, BLOCK), lambda j, i: (i, j, 0, 0))

    def const(shape):
        return pl.BlockSpec(shape, lambda j, i: (0,) * len(shape), pipeline_mode=pl.Buffered(1))

    pos_t = pl.BlockSpec((ROPE_HALF, rows), lambda j, i: (0, j))
    pos = pl.BlockSpec((rows, LANES), lambda j, i: (j, 0))
    tok_shape = lambda w: jax.ShapeDtypeStruct((b, s, w), BF16)
    tok_t_shape = lambda f: jax.ShapeDtypeStruct((b, s // BLOCK, f, BLOCK), BF16)
    return pl.pallas_call(
        _proj_kernel,
        grid=grid,
        in_specs=[tok(d), const((1, d)), const((d, IN_WIDTH)),
                  const((1, 2 * D_MODEL)), pos_t, pos_t, pos, pos, pos, pos, pos,
                  const((rows, RET_QK_W)), const((rows, RET_QK_W))],
        out_specs=[tok_t(ATT_Q_W), tok(ATT_KV_W), tok_t(ATT_KV_W), tok(RET_QK_W),
                   tok_t(RET_QK_W), tok(RET_V_W), tok(RET_V_W), tok(D_MODEL), tok(D_MODEL)],
        out_shape=[tok_t_shape(ATT_Q_W), tok_shape(ATT_KV_W), tok_t_shape(ATT_KV_W), tok_shape(RET_QK_W),
                   tok_t_shape(RET_QK_W), tok_shape(RET_V_W),
                   tok_shape(RET_V_W), tok_shape(D_MODEL), tok_shape(D_MODEL)],
        compiler_params=pltpu.CompilerParams(
            dimension_semantics=("arbitrary", "arbitrary"), vmem_limit_bytes=V7X_VMEM_LIMIT_BYTES),
        name="proj",
    )(x, gain, w_in, b_gates, *tables)


def _mix_kernel(cdec_ref,
                qat_ref, ka_ref, vat_ref, qdr_ref, kdt_ref, vr_ref, sg_ref, ga_ref, gb_ref, x_ref,
                sink_ref, gn_ref, tril_ref, watt_ref, wret_ref, wout_ref,
                h_ref,
                kext_ref, vext_ref, state_ref, att_ref, ret_ref, merged_ref, *, tiles_per_seq, ntiles):
    t = pl.program_id(0)
    ws = t & 1
    rs = 1 - ws
    nblk = x_ref.shape[0] // BLOCK
    first = (jnp.minimum(t, ntiles - 1) % tiles_per_seq) == 0

    @pl.when(t == 0)
    def _():
        att_ref[1] = jnp.zeros(att_ref.shape[1:], BF16)
        ret_ref[1] = jnp.zeros(ret_ref.shape[1:], BF16)

    @pl.when(first)
    def _():
        kext_ref[:BLOCK, :] = jnp.zeros((BLOCK, ATT_KV_W), BF16)
        vext_ref[0] = jnp.zeros((ATT_KV_W, BLOCK), BF16)
        state_ref[...] = jnp.zeros_like(state_ref)

    kext_ref[BLOCK:, :] = ka_ref[...]
    vext_ref[1:] = vat_ref[...]

    from_prev = (lax.broadcasted_iota(jnp.int32, (BLOCK, BLOCK), 0)
                 > lax.broadcasted_iota(jnp.int32, (BLOCK, BLOCK), 1))

    def band_select(a, b):
        pick = lambda v, g: v[:, g * BLOCK:(g + 1) * BLOCK] if hasattr(v, "shape") else v
        return jnp.concatenate([jnp.where(from_prev, pick(a, g), pick(b, g)) for g in range(ATT_GROUP)], axis=1)
    zeros_q = jnp.zeros((ATT_HEAD_DIM, ATT_GROUP * BLOCK), BF16)

    ksl = [slice(h * RET_KEY_DIM, (h + 1) * RET_KEY_DIM) for h in range(RET_HEADS)]
    vsl = [slice(h * RET_VAL_DIM, (h + 1) * RET_VAL_DIM) for h in range(RET_HEADS)]
    rows_of = lambda c: pl.ds(c * BLOCK, BLOCK)

    def first_matmuls(c):
        rows = rows_of(c)
        qt = qat_ref[c]
        kc = kext_ref[pl.ds(c * BLOCK, 2 * BLOCK), :]
        scores = []
        for kv in range(ATT_KV_HEADS):
            qg = jnp.concatenate(
                [qt[(kv * ATT_GROUP + g) * ATT_HEAD_DIM:(kv * ATT_GROUP + g + 1) * ATT_HEAD_DIM] for g in range(ATT_GROUP)],
                axis=1)
            rhs = jnp.concatenate([qg, zeros_q] if kv == 0 else [zeros_q, qg], axis=0)
            scores.append(jnp.dot(kc, rhs, preferred_element_type=F32))
        zeros_k = jnp.zeros((RET_KEY_DIM, BLOCK), BF16)
        att = []
        for h in range(0, RET_HEADS, 2):
            pair = slice(h * RET_KEY_DIM, (h + 2) * RET_KEY_DIM)
            kdiag = jnp.concatenate([jnp.concatenate([kdt_ref[c, ksl[h], :], zeros_k], axis=1),
                                     jnp.concatenate([zeros_k, kdt_ref[c, ksl[h + 1], :]], axis=1)], axis=0)
            both = jnp.dot(qdr_ref[rows, pair], kdiag, preferred_element_type=F32)
            att += [both[:, :BLOCK], both[:, BLOCK:]]
        return scores, att

    def softmax_and_decay(c, stage1, state):
        scores, att = stage1
        probs, inv_den = [], []
        for kv in range(ATT_KV_HEADS):
            sc = scores[kv]
            prev = sc[:BLOCK]
            if c == 0:
                prev = prev + jnp.where(first, NEG_INF, 0.0).astype(F32)
            f = band_select(prev, sc[BLOCK:])
            sink = sink_ref[kv]
            m = jnp.maximum(jnp.max(f, axis=0, keepdims=True), sink)
            e = jnp.exp2(f - m)
            inv_den.append(1.0 / (jnp.sum(e, axis=0, keepdims=True) + jnp.exp2(sink - m)))
            probs.append(jnp.concatenate([band_select(e, 0.0), band_select(0.0, e)], axis=0).astype(BF16))
        att_b = [(att[h] * tril_ref[h]).astype(BF16) for h in range(RET_HEADS)]
        state_b = [state[h].astype(BF16) for h in range(RET_HEADS)]
        return probs, inv_den, att_b, state_b

    def second_matmuls(c, ops):
        probs, inv_den, att_b, state_b = ops
        rows = rows_of(c)
        vt_prev, vt_cur = vext_ref[c], vext_ref[c + 1]
        outs = []
        for kv in range(ATT_KV_HEADS):
            hs = slice(kv * ATT_HEAD_DIM, (kv + 1) * ATT_HEAD_DIM)
            vt = jnp.concatenate([vt_prev[hs], vt_cur[hs]], axis=1)
            outs.append(jnp.dot(vt, probs[kv], preferred_element_type=F32) * inv_den[kv])
        zeros_k = jnp.zeros((RET_KEY_DIM, RET_KEY_DIM), BF16)
        both = [jnp.dot(jnp.concatenate([jnp.concatenate([att_b[h], qdr_ref[rows, ksl[h]]], axis=1),
                                         jnp.concatenate([kdt_ref[c, ksl[h], :], zeros_k], axis=1)], axis=0),
                        jnp.concatenate([vr_ref[rows, vsl[h]], state_b[h]], axis=0), preferred_element_type=F32)
                for h in range(RET_HEADS)]
        ret = [bh[:BLOCK] for bh in both]
        upd = [bh[BLOCK:] for bh in both]
        return outs, ret, upd

    def finish(c, stage2):
        outs, ret, _ = stage2
        rows = rows_of(c)
        ot = jnp.concatenate(outs, axis=0)
        for g in range(ATT_GROUP):
            att_ref[ws, rows, g * LANES:(g + 1) * LANES] = ot[:, g * BLOCK:(g + 1) * BLOCK].T.astype(BF16)
        for h in range(RET_HEADS):
            out = ret[h]
            mu = jnp.mean(out, axis=-1, keepdims=True)
            dev = out - mu
            var = jnp.mean(dev * dev, axis=-1, keepdims=True)
            yn = dev * lax.rsqrt(var + GN_EPS) * gn_ref[:, vsl[h]]
            ret_ref[ws, rows, vsl[h]] = yn.astype(BF16) * sg_ref[rows, vsl[h]]

    def chunk_tail(c, stage1, state):
        stage2 = second_matmuls(c, softmax_and_decay(c, stage1, state))
        finish(c, stage2)
        return [state[h] * cdec_ref[h] + stage2[2][h] for h in range(RET_HEADS)]

    half = D_MODEL // 2
    cols = [slice(0, half), slice(half, D_MODEL)]

    assert nblk == 4
    state = [state_ref[h] for h in range(RET_HEADS)]
    def merged_half(n):
        ya = jnp.dot(att_ref[rs], watt_ref[:, cols[n]], preferred_element_type=F32)
        yr = jnp.dot(ret_ref[rs], wret_ref[:, cols[n]], preferred_element_type=F32)
        merged_ref[:, cols[n]] = ga_ref[:, cols[n]] * ya.astype(BF16) + gb_ref[:, cols[n]] * yr.astype(BF16)

    def out_half(n):
        h_ref[:, cols[n]] = x_ref[:, cols[n]] + jnp.dot(merged_ref[...], wout_ref[:, cols[n]],
                                                        preferred_element_type=F32)

    s0 = first_matmuls(0)
    s1 = first_matmuls(1)
    merged_half(0)
    state = chunk_tail(0, s0, state)
    s2 = first_matmuls(2)
    merged_half(1)
    state = chunk_tail(1, s1, state)
    s3 = first_matmuls(3)
    state = chunk_tail(2, s2, state)
    out_half(0)
    state = chunk_tail(3, s3, state)
    out_half(1)
    for h in range(RET_HEADS):
        state_ref[h] = state[h]

    kext_ref[:BLOCK, :] = kext_ref[nblk * BLOCK:, :]
    vext_ref[0] = vext_ref[nblk]


def _mix_call(proj_out, x, cdec, sinks, gn_gain, tril, w_att_up, w_ret_up, w_out):
    b, s, d = x.shape
    rows = MIX_ROWS
    nblk = rows // BLOCK
    tps = s // rows
    ntiles = b * tps
    grid = (ntiles + 1,)

    chunk_tile = lambda t: jnp.minimum(t, ntiles - 1)
    proj_tile = lambda t: jnp.maximum(t - 1, 0)

    def tok(width, tile):
        return pl.BlockSpec((None, rows, width), lambda t: (tile(t) // tps, tile(t) % tps, 0))

    def tok_t(feat):
        return pl.BlockSpec((None, nblk, feat, BLOCK), lambda t: (chunk_tile(t) // tps, chunk_tile(t) % tps, 0, 0))

    def const(shape):
        return pl.BlockSpec(shape, lambda t: (0,) * len(shape), pipeline_mode=pl.Buffered(1))

    smem = pl.BlockSpec(memory_space=pltpu.SMEM)
    ctok = lambda width: tok(width, chunk_tile)
    return pl.pallas_call(
        functools.partial(_mix_kernel, tiles_per_seq=tps, ntiles=ntiles),
        grid=grid,
        in_specs=[smem,
                  tok_t(ATT_Q_W), ctok(ATT_KV_W), tok_t(ATT_KV_W), ctok(RET_QK_W),
                  tok_t(RET_QK_W), ctok(RET_V_W), ctok(RET_V_W),
                  tok(D_MODEL, proj_tile), tok(D_MODEL, proj_tile), tok(d, proj_tile),
                  const((ATT_KV_HEADS, 1, ATT_GROUP * BLOCK)), const((1, RET_V_W)),
                  const((RET_HEADS, RET_CHUNK, RET_CHUNK)),
                  const((ATT_Q_W, d)), const((RET_V_W, d)), const((d, d))],
        out_specs=tok(d, proj_tile),
        out_shape=jax.ShapeDtypeStruct((b, s, d), F32),
        scratch_shapes=[
            pltpu.VMEM((BLOCK + rows, ATT_KV_W), BF16),
            pltpu.VMEM((1 + nblk, ATT_KV_W, BLOCK), BF16),
            pltpu.VMEM((RET_HEADS, RET_KEY_DIM, RET_VAL_DIM), F32),
            pltpu.VMEM((2, rows, ATT_Q_W), BF16),
            pltpu.VMEM((2, rows, RET_V_W), BF16),
            pltpu.VMEM((rows, d), BF16),
        ],
        compiler_params=pltpu.CompilerParams(
            dimension_semantics=("arbitrary",), vmem_limit_bytes=V7X_VMEM_LIMIT_BYTES),
        name="mix",
    )(cdec, *proj_out, x, sinks, gn_gain, tril, w_att_up, w_ret_up, w_out)


def _ffn_kernel(h_ref, gain_ref, w1_ref, w2_ref, fgain_ref, o_ref, *, final_norm):
    for s in range(h_ref.shape[0] // FFN_SUB_ROWS):
        rs = slice(s * FFN_SUB_ROWS, (s + 1) * FFN_SUB_ROWS)
        h = h_ref[rs, :]
        xb = _rms_scale(h, gain_ref[...]).astype(BF16)
        acc = jnp.zeros(h.shape, F32)
        for c in range(D_FF // FFN_COLS):
            sl = slice(c * FFN_COLS, (c + 1) * FFN_COLS)
            a = jnp.maximum(jnp.dot(xb, w1_ref[:, sl], preferred_element_type=F32), 0.0)
            acc = acc + jnp.dot((a * a).astype(BF16), w2_ref[sl, :], preferred_element_type=F32)
        y = h + acc
        if final_norm:
            y = _rms_scale(y, fgain_ref[...])
        o_ref[rs, :] = y


def _ffn_call(h, gain, w1, w2, fgain, final_norm):
    b, s, d = h.shape
    rows = FFN_ROWS
    grid = (b, s // rows)

    def const(shape):
        return pl.BlockSpec(shape, lambda i, j: (0,) * len(shape), pipeline_mode=pl.Buffered(1))

    tok = pl.BlockSpec((None, rows, d), lambda i, j: (i, j, 0))
    return pl.pallas_call(
        functools.partial(_ffn_kernel, final_norm=final_norm),
        grid=grid,
        in_specs=[tok, const((1, d)), const((d, D_FF)), const((D_FF, d)), const((1, d))],
        out_specs=tok,
        out_shape=jax.ShapeDtypeStruct((b, s, d), F32),
        compiler_params=pltpu.CompilerParams(
            dimension_semantics=("arbitrary", "arbitrary"), vmem_limit_bytes=V7X_VMEM_LIMIT_BYTES),
        name="ffn",
    )(h, gain, w1, w2, fgain)


def _decay_terms():
    h, c = RET_HEADS, RET_CHUNK
    log_gamma = jnp.log1p(-jnp.exp2(-5.0 - jnp.arange(h, dtype=F32)))
    idx = jnp.arange(c, dtype=F32)
    diff = idx[:, None] - idx[None, :]
    tril = jnp.where(diff >= 0, jnp.exp(-c * log_gamma)[:, None, None], 0.0)
    q_decay = jnp.exp((idx + 1.0)[None, :] * log_gamma[:, None])
    k_decay = jnp.exp((c - 1.0 - idx)[None, :] * log_gamma[:, None])
    chunk_decay = jnp.exp(c * log_gamma)
    return tril, q_decay, k_decay, chunk_decay


def _position_tables(seq_len, q_decay, k_decay):
    pos = jnp.arange(seq_len, dtype=F32)

    def cos_sin(dim, theta):
        inv_freq = theta ** (-jnp.arange(0, dim, 2, dtype=F32) / dim)
        ang = pos[:, None] * inv_freq[None, :]
        return jnp.cos(ang), jnp.sin(ang)

    cos, sin = cos_sin(ROPE_DIM, ROPE_THETA)
    pad = ATT_HEAD_DIM - ROPE_DIM
    zeros_half = jnp.zeros_like(sin)
    one_head = lambda parts: jnp.tile(jnp.concatenate(parts, axis=-1), (1, LANES // ATT_HEAD_DIM))
    ca = one_head([cos, cos, jnp.ones((seq_len, pad), F32)])
    s1 = one_head([-sin, zeros_half, jnp.zeros((seq_len, pad), F32)])
    s2 = one_head([zeros_half, sin, jnp.zeros((seq_len, pad), F32)])
    cos_r, sin_r = cos_sin(RET_KEY_DIM, RET_ROT_BASE)
    cr = jnp.concatenate([cos_r, cos_r], axis=-1)
    sr = jnp.concatenate([-sin_r, sin_r], axis=-1)

    def per_row(decay):
        t = jnp.repeat(decay.T, RET_KEY_DIM, axis=1)
        return jnp.tile(t, (PROJ_ROWS // RET_CHUNK, 1))

    return cos.T, sin.T, ca, s1, s2, cr, sr, per_row(q_decay), per_row(k_decay)


def kernel(x, norm_mix_gain, w_in, b_gates, attn_sinks, ret_gn_gain, w_att_up, w_ret_up, w_out,
           norm_mlp_gain, w_ff1, w_ff2, norm_final_gain):
    b, s, d = x.shape
    depth = w_in.shape[0]
    assert d == D_MODEL and s % MIX_ROWS == 0 and s % PROJ_ROWS == 0 and s % FFN_ROWS == 0
    tril, q_decay, k_decay, cdec = _decay_terms()
    tables = _position_tables(s, q_decay, k_decay)
    row = lambda v: v.reshape(1, -1).astype(F32)
    att_perm = np.arange(ATT_Q_W).reshape(ATT_KV_HEADS, ATT_GROUP, ATT_HEAD_DIM).transpose(1, 0, 2).reshape(-1)
    h = x
    for l in range(depth):
        w = w_in[l].astype(BF16)
        proj_out = _proj_call(h, row(norm_mix_gain[l]), w, row(b_gates[l]), tables)
        sinks = jnp.repeat((attn_sinks[l].astype(F32) * LOG2E).reshape(ATT_KV_HEADS, 1, ATT_GROUP), BLOCK, axis=-1)
        h = _mix_call(proj_out, h, cdec, sinks, row(ret_gn_gain[l]), tril,
                      w_att_up[l][att_perm].astype(BF16), w_ret_up[l].astype(BF16), w_out[l].astype(BF16))
        h = _ffn_call(h, row(norm_mlp_gain[l]), w_ff1[l].astype(BF16), w_ff2[l].astype(BF16),
                      row(norm_final_gain), final_norm=(l == depth - 1))
    return h
```

```python
import functools

import jax
import jax.numpy as jnp
import numpy as np
from jax import lax
from jax.experimental import pallas as pl
from jax.experimental.pallas import tpu as pltpu

D_MODEL = 1024
ATT_HEADS = 8
ATT_KV_HEADS = 2
ATT_GROUP = ATT_HEADS // ATT_KV_HEADS
ATT_HEAD_DIM = 64
WINDOW = 128
ATT_BLOCK = 128
ROPE_DIM = ATT_HEAD_DIM // 4
ROPE_HALF = ROPE_DIM // 2
ROPE_THETA = 500000.0
RET_HEADS = 4
RET_KEY_DIM = 128
RET_VAL_DIM = 256
RET_CHUNK = 128
RET_ROT_BASE = 10000.0
D_FF = 4 * D_MODEL
NORM_EPS = 1e-6
GN_EPS = 1e-6
NEG_INF = -1e30
LOG2E = 1.4426950408889634

ATT_Q_W = ATT_HEADS * ATT_HEAD_DIM
ATT_KV_W = ATT_KV_HEADS * ATT_HEAD_DIM
RET_QK_W = RET_HEADS * RET_KEY_DIM
RET_V_W = RET_HEADS * RET_VAL_DIM
IN_SPLITS = (ATT_Q_W, ATT_KV_W, ATT_KV_W, RET_QK_W, RET_QK_W, RET_V_W, RET_V_W, D_MODEL, D_MODEL)
IN_WIDTH = sum(IN_SPLITS)
(OFF_QA, OFF_KA, OFF_VA, OFF_QR, OFF_KR, OFF_VR, OFF_GR, OFF_GA, OFF_GB) = (
    int(v) for v in np.concatenate([[0], np.cumsum(IN_SPLITS)[:-1]]))

LANES = 128
SUBLANES = 8
V7X_VMEM_LIMIT_BYTES = 56 * 1024 * 1024

BLOCK = 128
PROJ_ROWS = 512
PROJ_COLS = 512
MIX_ROWS = 512
FFN_ROWS = 1024
FFN_SUB_ROWS = 1024
FFN_COLS = 512

BF16 = jnp.bfloat16
F32 = jnp.float32

assert WINDOW == ATT_BLOCK == RET_CHUNK == BLOCK == LANES
assert RET_KEY_DIM == LANES and ATT_KV_W == LANES and ROPE_HALF == SUBLANES
assert PROJ_ROWS % BLOCK == 0 and MIX_ROWS % BLOCK == 0


def _sigmoid(x):
    return 0.5 * jnp.tanh(0.5 * x) + 0.5


def _silu(x):
    return x * _sigmoid(x)


def _rms_scale(x, gain):
    ms = jnp.mean(x * x, axis=-1, keepdims=True)
    return (x * lax.rsqrt(ms + NORM_EPS)) * gain


def _proj_kernel(x_ref, gain_ref, w_ref, bg_ref,
                 cat_ref, sat_ref, ca_ref, s1_ref, s2_ref, cr_ref, sr_ref, qdec_ref, kdec_ref,
                 qat_ref, ka_ref, vat_ref, qdr_ref, kdt_ref, vr_ref, sg_ref, ga_ref, gb_ref):
    rows = x_ref.shape[0]
    nblk = rows // BLOCK
    xb = _rms_scale(x_ref[...], gain_ref[...]).astype(BF16)

    def proj(off, width):
        return jnp.dot(xb, w_ref[:, off:off + width], preferred_element_type=F32)

    def store_blocks(ref, feat, val):
        for c in range(nblk):
            ref[c, feat, :] = val[:, c * BLOCK:(c + 1) * BLOCK].astype(BF16)

    zt = proj(OFF_QA, ATT_Q_W).T
    cat, sat = cat_ref[...], sat_ref[...]
    for hd in range(ATT_HEADS):
        b0 = hd * ATT_HEAD_DIM
        x1, x2 = zt[b0:b0 + ROPE_HALF], zt[b0 + ROPE_HALF:b0 + ROPE_DIM]
        head = jnp.concatenate([x1 * cat - x2 * sat, x2 * cat + x1 * sat, zt[b0 + ROPE_DIM:b0 + ATT_HEAD_DIM]], axis=0)
        store_blocks(qat_ref, slice(b0, b0 + ATT_HEAD_DIM), head * (ATT_HEAD_DIM ** -0.5 * LOG2E))
    store_blocks(vat_ref, slice(None), proj(OFF_VA, ATT_KV_W).T)
    z = proj(OFF_KA, ATT_KV_W)
    ka_ref[...] = (z * ca_ref[...] + pltpu.roll(z, LANES - ROPE_HALF, 1) * s1_ref[...]
                   + pltpu.roll(z, ROPE_HALF, 1) * s2_ref[...]).astype(BF16)

    cr, sr = cr_ref[...], sr_ref[...]

    def rope_ret(zh):
        return zh * cr + pltpu.roll(zh, RET_KEY_DIM // 2, 1) * sr

    z = proj(OFF_QR, RET_QK_W)
    for h in range(RET_HEADS):
        sl = slice(h * RET_KEY_DIM, (h + 1) * RET_KEY_DIM)
        qdr_ref[:, sl] = (rope_ret(z[:, sl]) * qdec_ref[:, sl]).astype(BF16)
    z = proj(OFF_KR, RET_QK_W)
    for h in range(RET_HEADS):
        sl = slice(h * RET_KEY_DIM, (h + 1) * RET_KEY_DIM)
        k = rope_ret(z[:, sl]) * (RET_KEY_DIM ** -0.5)
        store_blocks(kdt_ref, sl, (k * kdec_ref[:, sl]).T)
    for c in range(RET_V_W // PROJ_COLS):
        sl = slice(c * PROJ_COLS, (c + 1) * PROJ_COLS)
        sg_ref[:, sl] = _silu(proj(OFF_GR + c * PROJ_COLS, PROJ_COLS)).astype(BF16)
    for c in range(D_MODEL // PROJ_COLS):
        sl = slice(c * PROJ_COLS, (c + 1) * PROJ_COLS)
        g = proj(OFF_GA + c * PROJ_COLS, PROJ_COLS) + bg_ref[:, sl]
        ga_ref[:, sl] = _sigmoid(g).astype(BF16)
    for c in range(D_MODEL // PROJ_COLS):
        sl = slice(c * PROJ_COLS, (c + 1) * PROJ_COLS)
        g = proj(OFF_GB + c * PROJ_COLS, PROJ_COLS) + bg_ref[:, D_MODEL + c * PROJ_COLS:D_MODEL + (c + 1) * PROJ_COLS]
        gb_ref[:, sl] = _sigmoid(g).astype(BF16)
    for c in range(RET_V_W // PROJ_COLS):
        sl = slice(c * PROJ_COLS, (c + 1) * PROJ_COLS)
        vr_ref[:, sl] = proj(OFF_VR + c * PROJ_COLS, PROJ_COLS).astype(BF16)


def _proj_call(x, gain, w_in, b_gates, tables):
    b, s, d = x.shape
    rows = PROJ_ROWS
    nblk = rows // BLOCK
    grid = (s // rows, b)

    def tok(width):
        return pl.BlockSpec((None, rows, width), lambda j, i: (i, j, 0))

    def tok_t(feat):
        return pl.BlockSpec((None, nblk, feat, BLOCK), lambda j, i: (i, j, 0, 0))

    def const(shape):
        return pl.BlockSpec(shape, lambda j, i: (0,) * len(shape), pipeline_mode=pl.Buffered(1))

    pos_t = pl.BlockSpec((ROPE_HALF, rows), lambda j, i: (0, j))
    pos = pl.BlockSpec((rows, LANES), lambda j, i: (j, 0))
    tok_shape = lambda w: jax.ShapeDtypeStruct((b, s, w), BF16)
    tok_t_shape = lambda f: jax.ShapeDtypeStruct((b, s // BLOCK, f, BLOCK), BF16)
    return pl.pallas_call(
        _proj_kernel,
        grid=grid,
        in_specs=[tok(d), const((1, d)), const((d, IN_WIDTH)),
                  const((1, 2 * D_MODEL)), pos_t, pos_t, pos, pos, pos, pos, pos,
                  const((rows, RET_QK_W)), const((rows, RET_QK_W))],
        out_specs=[tok_t(ATT_Q_W), tok(ATT_KV_W), tok_t(ATT_KV_W), tok(RET_QK_W),
                   tok_t(RET_QK_W), tok(RET_V_W), tok(RET_V_W), tok(D_MODEL), tok(D_MODEL)],
        out_shape=[tok_t_shape(ATT_Q_W), tok_shape(ATT_KV_W), tok_t_shape(ATT_KV_W), tok_shape(RET_QK_W),
                   tok_t_shape(RET_QK_W), tok_shape(RET_V_W),
                   tok_shape(RET_V_W), tok_shape(D_MODEL), tok_shape(D_MODEL)],
        compiler_params=pltpu.CompilerParams(
            dimension_semantics=("arbitrary", "arbitrary"), vmem_limit_bytes=V7X_VMEM_LIMIT_BYTES),
        name="proj",
    )(x, gain, w_in, b_gates, *tables)


def _mix_kernel(cdec_ref,
                qat_ref, ka_ref, vat_ref, qdr_ref, kdt_ref, vr_ref, sg_ref, ga_ref, gb_ref, x_ref,
                sink_ref, gn_ref, tril_ref, watt_ref, wret_ref, wout_ref,
                h_ref,
                kext_ref, vext_ref, state_ref, att_ref, ret_ref, merged_ref, *, tiles_per_seq, ntiles):
    t = pl.program_id(0)
    ws = t & 1
    rs = 1 - ws
    nblk = x_ref.shape[0] // BLOCK
    first = (jnp.minimum(t, ntiles - 1) % tiles_per_seq) == 0

    @pl.when(t == 0)
    def _():
        att_ref[1] = jnp.zeros(att_ref.shape[1:], BF16)
        ret_ref[1] = jnp.zeros(ret_ref.shape[1:], BF16)

    @pl.when(first)
    def _():
        kext_ref[:BLOCK, :] = jnp.zeros((BLOCK, ATT_KV_W), BF16)
        vext_ref[0] = jnp.zeros((ATT_KV_W, BLOCK), BF16)
        state_ref[...] = jnp.zeros_like(state_ref)

    kext_ref[BLOCK:, :] = ka_ref[...]
    vext_ref[1:] = vat_ref[...]

    from_prev = (lax.broadcasted_iota(jnp.int32, (BLOCK, BLOCK), 0)
                 > lax.broadcasted_iota(jnp.int32, (BLOCK, BLOCK), 1))

    def band_select(a, b):
        pick = lambda v, g: v[:, g * BLOCK:(g + 1) * BLOCK] if hasattr(v, "shape") else v
        return jnp.concatenate([jnp.where(from_prev, pick(a, g), pick(b, g)) for g in range(ATT_GROUP)], axis=1)
    zeros_q = jnp.zeros((ATT_HEAD_DIM, ATT_GROUP * BLOCK), BF16)

    ksl = [slice(h * RET_KEY_DIM, (h + 1) * RET_KEY_DIM) for h in range(RET_HEADS)]
    vsl = [slice(h * RET_VAL_DIM, (h + 1) * RET_VAL_DIM) for h in range(RET_HEADS)]
    rows_of = lambda c: pl.ds(c * BLOCK, BLOCK)

    def first_matmuls(c):
        rows = rows_of(c)
        qt = qat_ref[c]
        kc = kext_ref[pl.ds(c * BLOCK, 2 * BLOCK), :]
        scores = []
        for kv in range(ATT_KV_HEADS):
            qg = jnp.concatenate(
                [qt[(kv * ATT_GROUP + g) * ATT_HEAD_DIM:(kv * ATT_GROUP + g + 1) * ATT_HEAD_DIM] for g in range(ATT_GROUP)],
                axis=1)
            rhs = jnp.concatenate([qg, zeros_q] if kv == 0 else [zeros_q, qg], axis=0)
            scores.append(jnp.dot(kc, rhs, preferred_element_type=F32))
        zeros_k = jnp.zeros((RET_KEY_DIM, BLOCK), BF16)
        att = []
        for h in range(0, RET_HEADS, 2):
            pair = slice(h * RET_KEY_DIM, (h + 2) * RET_KEY_DIM)
            kdiag = jnp.concatenate([jnp.concatenate([kdt_ref[c, ksl[h], :], zeros_k], axis=1),
                                     jnp.concatenate([zeros_k, kdt_ref[c, ksl[h + 1], :]], axis=1)], axis=0)
            both = jnp.dot(qdr_ref[rows, pair], kdiag, preferred_element_type=F32)
            att += [both[:, :BLOCK], both[:, BLOCK:]]
        return scores, att

    def softmax_and_decay(c, stage1, state):
        scores, att = stage1
        probs, inv_den = [], []
        for kv in range(ATT_KV_HEADS):
            sc = scores[kv]
            prev = sc[:BLOCK]
            if c == 0:
                prev = prev + jnp.where(first, NEG_INF, 0.0).astype(F32)
            f = band_select(prev, sc[BLOCK:])
            sink = sink_ref[kv]
            m = jnp.maximum(jnp.max(f, axis=0, keepdims=True), sink)
            e = jnp.exp2(f - m)
            inv_den.append(1.0 / (jnp.sum(e, axis=0, keepdims=True) + jnp.exp2(sink - m)))
            probs.append(jnp.concatenate([band_select(e, 0.0), band_select(0.0, e)], axis=0).astype(BF16))
        att_b = [(att[h] * tril_ref[h]).astype(BF16) for h in range(RET_HEADS)]
        state_b = [state[h].astype(BF16) for h in range(RET_HEADS)]
        return probs, inv_den, att_b, state_b

    def second_matmuls(c, ops):
        probs, inv_den, att_b, state_b = ops
        rows = rows_of(c)
        vt_prev, vt_cur = vext_ref[c], vext_ref[c + 1]
        outs = []
        for kv in range(ATT_KV_HEADS):
            hs = slice(kv * ATT_HEAD_DIM, (kv + 1) * ATT_HEAD_DIM)
            vt = jnp.concatenate([vt_prev[hs], vt_cur[hs]], axis=1)
            outs.append(jnp.dot(vt, probs[kv], preferred_element_type=F32) * inv_den[kv])
        zeros_k = jnp.zeros((RET_KEY_DIM, RET_KEY_DIM), BF16)
        both = [jnp.dot(jnp.concatenate([jnp.concatenate([att_b[h], qdr_ref[rows, ksl[h]]], axis=1),
                                         jnp.concatenate([kdt_ref[c, ksl[h], :], zeros_k], axis=1)], axis=0),
                        jnp.concatenate([vr_ref[rows, vsl[h]], state_b[h]], axis=0), preferred_element_type=F32)
                for h in range(RET_HEADS)]
        ret = [bh[:BLOCK] for bh in both]
        upd = [bh[BLOCK:] for bh in both]
        return outs, ret, upd

    def finish(c, stage2):
        outs, ret, _ = stage2
        rows = rows_of(c)
        ot = jnp.concatenate(outs, axis=0)
        for g in range(ATT_GROUP):
            att_ref[ws, rows, g * LANES:(g + 1) * LANES] = ot[:, g * BLOCK:(g + 1) * BLOCK].T.astype(BF16)
        for h in range(RET_HEADS):
            out = ret[h]
            mu = jnp.mean(out, axis=-1, keepdims=True)
            dev = out - mu
            var = jnp.mean(dev * dev, axis=-1, keepdims=True)
            yn = dev * lax.rsqrt(var + GN_EPS) * gn_ref[:, vsl[h]]
            ret_ref[ws, rows, vsl[h]] = yn.astype(BF16) * sg_ref[rows, vsl[h]]

    def chunk_tail(c, stage1, state):
        stage2 = second_matmuls(c, softmax_and_decay(c, stage1, state))
        finish(c, stage2)
        return [state[h] * cdec_ref[h] + stage2[2][h] for h in range(RET_HEADS)]

    half = D_MODEL // 2
    cols = [slice(0, half), slice(half, D_MODEL)]

    assert nblk == 4
    state = [state_ref[h] for h in range(RET_HEADS)]
    def merged_half(n):
        ya = jnp.dot(att_ref[rs], watt_ref[:, cols[n]], preferred_element_type=F32)
        yr = jnp.dot(ret_ref[rs], wret_ref[:, cols[n]], preferred_element_type=F32)
        merged_ref[:, cols[n]] = ga_ref[:, cols[n]] * ya.astype(BF16) + gb_ref[:, cols[n]] * yr.astype(BF16)

    def out_half(n):
        h_ref[:, cols[n]] = x_ref[:, cols[n]] + jnp.dot(merged_ref[...], wout_ref[:, cols[n]],
                                                        preferred_element_type=F32)

    s0 = first_matmuls(0)
    s1 = first_matmuls(1)
    merged_half(0)
    state = chunk_tail(0, s0, state)
    s2 = first_matmuls(2)
    merged_half(1)
    state = chunk_tail(1, s1, state)
    s3 = first_matmuls(3)
    state = chunk_tail(2, s2, state)
    out_half(0)
    state = chunk_tail(3, s3, state)
    out_half(1)
    for h in range(RET_HEADS):
        state_ref[h] = state[h]

    kext_ref[:BLOCK, :] = kext_ref[nblk * BLOCK:, :]
    vext_ref[0] = vext_ref[nblk]


def _mix_call(proj_out, x, cdec, sinks, gn_gain, tril, w_att_up, w_ret_up, w_out):
    b, s, d = x.shape
    rows = MIX_ROWS
    nblk = rows // BLOCK
    tps = s // rows
    ntiles = b * tps
    grid = (ntiles + 1,)

    chunk_tile = lambda t: jnp.minimum(t, ntiles - 1)
    proj_tile = lambda t: jnp.maximum(t - 1, 0)

    def tok(width, tile):
        return pl.BlockSpec((None, rows, width), lambda t: (tile(t) // tps, tile(t) % tps, 0))

    def tok_t(feat):
        return pl.BlockSpec((None, nblk, feat, BLOCK), lambda t: (chunk_tile(t) // tps, chunk_tile(t) % tps, 0, 0))

    def const(shape):
        return pl.BlockSpec(shape, lambda t: (0,) * len(shape), pipeline_mode=pl.Buffered(1))

    smem = pl.BlockSpec(memory_space=pltpu.SMEM)
    ctok = lambda width: tok(width, chunk_tile)
    return pl.pallas_call(
        functools.partial(_mix_kernel, tiles_per_seq=tps, ntiles=ntiles),
        grid=grid,
        in_specs=[smem,
                  tok_t(ATT_Q_W), ctok(ATT_KV_W), tok_t(ATT_KV_W), ctok(RET_QK_W),
                  tok_t(RET_QK_W), ctok(RET_V_W), ctok(RET_V_W),
                  tok(D_MODEL, proj_tile), tok(D_MODEL, proj_tile), tok(d, proj_tile),
                  const((ATT_KV_HEADS, 1, ATT_GROUP * BLOCK)), const((1, RET_V_W)),
                  const((RET_HEADS, RET_CHUNK, RET_CHUNK)),
                  const((ATT_Q_W, d)), const((RET_V_W, d)), const((d, d))],
        out_specs=tok(d, proj_tile),
        out_shape=jax.ShapeDtypeStruct((b, s, d), F32),
        scratch_shapes=[
            pltpu.VMEM((BLOCK + rows, ATT_KV_W), BF16),
            pltpu.VMEM((1 + nblk, ATT_KV_W, BLOCK), BF16),
            pltpu.VMEM((RET_HEADS, RET_KEY_DIM, RET_VAL_DIM), F32),
            pltpu.VMEM((2, rows, ATT_Q_W), BF16),
            pltpu.VMEM((2, rows, RET_V_W), BF16),
            pltpu.VMEM((rows, d), BF16),
        ],
        compiler_params=pltpu.CompilerParams(
            dimension_semantics=("arbitrary",), vmem_limit_bytes=V7X_VMEM_LIMIT_BYTES),
        name="mix",
    )(cdec, *proj_out, x, sinks, gn_gain, tril, w_att_up, w_ret_up, w_out)


def _ffn_kernel(h_ref, gain_ref, w1_ref, w2_ref, fgain_ref, o_ref, *, final_norm):
    for s in range(h_ref.shape[0] // FFN_SUB_ROWS):
        rs = slice(s * FFN_SUB_ROWS, (s + 1) * FFN_SUB_ROWS)
        h = h_ref[rs, :]
        xb = _rms_scale(h, gain_ref[...]).astype(BF16)
        acc = jnp.zeros(h.shape, F32)
        for c in range(D_FF // FFN_COLS):
            sl = slice(c * FFN_COLS, (c + 1) * FFN_COLS)
            a = jnp.maximum(jnp.dot(xb, w1_ref[:, sl], preferred_element_type=F32), 0.0)
            acc = acc + jnp.dot((a * a).astype(BF16), w2_ref[sl, :], preferred_element_type=F32)
        y = h + acc
        if final_norm:
            y = _rms_scale(y, fgain_ref[...])
        o_ref[rs, :] = y


def _ffn_call(h, gain, w1, w2, fgain, final_norm):
    b, s, d = h.shape
    rows = FFN_ROWS
    grid = (b, s // rows)

    def const(shape):
        return pl.BlockSpec(shape, lambda i, j: (0,) * len(shape), pipeline_mode=pl.Buffered(1))

    tok = pl.BlockSpec((None, rows, d), lambda i, j: (i, j, 0))
    return pl.pallas_call(
        functools.partial(_ffn_kernel, final_norm=final_norm),
        grid=grid,
        in_specs=[tok, const((1, d)), const((d, D_FF)), const((D_FF, d)), const((1, d))],
        out_specs=tok,
        out_shape=jax.ShapeDtypeStruct((b, s, d), F32),
        compiler_params=pltpu.CompilerParams(
            dimension_semantics=("arbitrary", "arbitrary"), vmem_limit_bytes=V7X_VMEM_LIMIT_BYTES),
        name="ffn",
    )(h, gain, w1, w2, fgain)


def _decay_terms():
    h, c = RET_HEADS, RET_CHUNK
    log_gamma = jnp.log1p(-jnp.exp2(-5.0 - jnp.arange(h, dtype=F32)))
    idx = jnp.arange(c, dtype=F32)
    diff = idx[:, None] - idx[None, :]
    tril = jnp.where(diff >= 0, jnp.exp(-c * log_gamma)[:, None, None], 0.0)
    q_decay = jnp.exp((idx + 1.0)[None, :] * log_gamma[:, None])
    k_decay = jnp.exp((c - 1.0 - idx)[None, :] * log_gamma[:, None])
    chunk_decay = jnp.exp(c * log_gamma)
    return tril, q_decay, k_decay, chunk_decay


def _position_tables(seq_len, q_decay, k_decay):
    pos = jnp.arange(seq_len, dtype=F32)

    def cos_sin(dim, theta):
        inv_freq = theta ** (-jnp.arange(0, dim, 2, dtype=F32) / dim)
        ang = pos[:, None] * inv_freq[None, :]
        return jnp.cos(ang), jnp.sin(ang)

    cos, sin = cos_sin(ROPE_DIM, ROPE_THETA)
    pad = ATT_HEAD_DIM - ROPE_DIM
    zeros_half = jnp.zeros_like(sin)
    one_head = lambda parts: jnp.tile(jnp.concatenate(parts, axis=-1), (1, LANES // ATT_HEAD_DIM))
    ca = one_head([cos, cos, jnp.ones((seq_len, pad), F32)])
    s1 = one_head([-sin, zeros_half, jnp.zeros((seq_len, pad), F32)])
    s2 = one_head([zeros_half, sin, jnp.zeros((seq_len, pad), F32)])
    cos_r, sin_r = cos_sin(RET_KEY_DIM, RET_ROT_BASE)
    cr = jnp.concatenate([cos_r, cos_r], axis=-1)
    sr = jnp.concatenate([-sin_r, sin_r], axis=-1)

    def per_row(decay):
        t = jnp.repeat(decay.T, RET_KEY_DIM, axis=1)
        return jnp.tile(t, (PROJ_ROWS // RET_CHUNK, 1))

    return cos.T, sin.T, ca, s1, s2, cr, sr, per_row(q_decay), per_row(k_decay)


def kernel(x, norm_mix_gain, w_in, b_gates, attn_sinks, ret_gn_gain, w_att_up, w_ret_up, w_out,
           norm_mlp_gain, w_ff1, w_ff2, norm_final_gain):
    b, s, d = x.shape
    depth = w_in.shape[0]
    assert d == D_MODEL and s % MIX_ROWS == 0 and s % PROJ_ROWS == 0 and s % FFN_ROWS == 0
    tril, q_decay, k_decay, cdec = _decay_terms()
    tables = _position_tables(s, q_decay, k_decay)
    row = lambda v: v.reshape(1, -1).astype(F32)
    att_perm = np.arange(ATT_Q_W).reshape(ATT_KV_HEADS, ATT_GROUP, ATT_HEAD_DIM).transpose(1, 0, 2).reshape(-1)
    h = x
    for l in range(depth):
        w = w_in[l].astype(BF16)
        proj_out = _proj_call(h, row(norm_mix_gain[l]), w, row(b_gates[l]), tables)
        sinks = jnp.repeat((attn_sinks[l].astype(F32) * LOG2E).reshape(ATT_KV_HEADS, 1, ATT_GROUP), BLOCK, axis=-1)
        h = _mix_call(proj_out, h, cdec, sinks, row(ret_gn_gain[l]), tril,
                      w_att_up[l][att_perm].astype(BF16), w_ret_up[l].astype(BF16), w_out[l].astype(BF16))
        h = _ffn_call(h, row(norm_mlp_gain[l]), w_ff1[l].astype(BF16), w_ff2[l].astype(BF16),
                      row(norm_final_gain), final_norm=(l == depth - 1))
    return h
```

```python
import functools

import jax
import jax.numpy as jnp
import numpy as np
from jax import lax
from jax.experimental import pallas as pl
from jax.experimental.pallas import tpu as pltpu

D_MODEL = 1024
ATT_HEADS = 8
ATT_KV_HEADS = 2
ATT_GROUP = ATT_HEADS // ATT_KV_HEADS
ATT_HEAD_DIM = 64
WINDOW = 128
ATT_BLOCK = 128
ROPE_DIM = ATT_HEAD_DIM // 4
ROPE_HALF = ROPE_DIM // 2
ROPE_THETA = 500000.0
RET_HEADS = 4
RET_KEY_DIM = 128
RET_VAL_DIM = 256
RET_CHUNK = 128
RET_ROT_BASE = 10000.0
D_FF = 4 * D_MODEL
NORM_EPS = 1e-6
GN_EPS = 1e-6
NEG_INF = -1e30
LOG2E = 1.4426950408889634

ATT_Q_W = ATT_HEADS * ATT_HEAD_DIM
ATT_KV_W = ATT_KV_HEADS * ATT_HEAD_DIM
RET_QK_W = RET_HEADS * RET_KEY_DIM
RET_V_W = RET_HEADS * RET_VAL_DIM
IN_SPLITS = (ATT_Q_W, ATT_KV_W, ATT_KV_W, RET_QK_W, RET_QK_W, RET_V_W, RET_V_W, D_MODEL, D_MODEL)
IN_WIDTH = sum(IN_SPLITS)
(OFF_QA, OFF_KA, OFF_VA, OFF_QR, OFF_KR, OFF_VR, OFF_GR, OFF_GA, OFF_GB) = (
    int(v) for v in np.concatenate([[0], np.cumsum(IN_SPLITS)[:-1]]))

LANES = 128
SUBLANES = 8
V7X_VMEM_LIMIT_BYTES = 56 * 1024 * 1024

BLOCK = 128
PROJ_ROWS = 1024
PROJ_COLS = 512
MIX_ROWS = 512
FFN_ROWS = 1024
FFN_SUB_ROWS = 512
FFN_COLS = 512

BF16 = jnp.bfloat16
F32 = jnp.float32

assert WINDOW == ATT_BLOCK == RET_CHUNK == BLOCK == LANES
assert RET_KEY_DIM == LANES and ATT_KV_W == LANES and ROPE_HALF == SUBLANES
assert PROJ_ROWS % BLOCK == 0 and MIX_ROWS % BLOCK == 0


def _sigmoid(x):
    return 0.5 * jnp.tanh(0.5 * x) + 0.5


def _silu(x):
    return x * _sigmoid(x)


def _rms_scale(x, gain):
    ms = jnp.mean(x * x, axis=-1, keepdims=True)
    return (x * lax.rsqrt(ms + NORM_EPS)) * gain


def _proj_kernel(x_ref, gain_ref, w_ref, bg_ref,
                 cat_ref, sat_ref, ca_ref, s1_ref, s2_ref, cr_ref, sr_ref, qdec_ref, kdec_ref,
                 qat_ref, ka_ref, vat_ref, qdr_ref, kdt_ref, vr_ref, sg_ref, ga_ref, gb_ref):
    rows = x_ref.shape[0]
    nblk = rows // BLOCK
    xb = _rms_scale(x_ref[...], gain_ref[...]).astype(BF16)

    def proj(off, width):
        return jnp.dot(xb, w_ref[:, off:off + width], preferred_element_type=F32)

    def store_blocks(ref, feat, val):
        for c in range(nblk):
            ref[c, feat, :] = val[:, c * BLOCK:(c + 1) * BLOCK].astype(BF16)

    zt = proj(OFF_QA, ATT_Q_W).T
    cat, sat = cat_ref[...], sat_ref[...]
    for hd in range(ATT_HEADS):
        b0 = hd * ATT_HEAD_DIM
        x1, x2 = zt[b0:b0 + ROPE_HALF], zt[b0 + ROPE_HALF:b0 + ROPE_DIM]
        head = jnp.concatenate([x1 * cat - x2 * sat, x2 * cat + x1 * sat, zt[b0 + ROPE_DIM:b0 + ATT_HEAD_DIM]], axis=0)
        store_blocks(qat_ref, slice(b0, b0 + ATT_HEAD_DIM), head * (ATT_HEAD_DIM ** -0.5 * LOG2E))
    store_blocks(vat_ref, slice(None), proj(OFF_VA, ATT_KV_W).T)
    z = proj(OFF_KA, ATT_KV_W)
    ka_ref[...] = (z * ca_ref[...] + pltpu.roll(z, LANES - ROPE_HALF, 1) * s1_ref[...]
                   + pltpu.roll(z, ROPE_HALF, 1) * s2_ref[...]).astype(BF16)

    cr, sr = cr_ref[...], sr_ref[...]

    def rope_ret(zh):
        return zh * cr + pltpu.roll(zh, RET_KEY_DIM // 2, 1) * sr

    z = proj(OFF_QR, RET_QK_W)
    for h in range(RET_HEADS):
        sl = slice(h * RET_KEY_DIM, (h + 1) * RET_KEY_DIM)
        qdr_ref[:, sl] = (rope_ret(z[:, sl]) * qdec_ref[:, sl]).astype(BF16)
    z = proj(OFF_KR, RET_QK_W)
    for h in range(RET_HEADS):
        sl = slice(h * RET_KEY_DIM, (h + 1) * RET_KEY_DIM)
        k = rope_ret(z[:, sl]) * (RET_KEY_DIM ** -0.5)
        store_blocks(kdt_ref, sl, (k * kdec_ref[:, sl]).T)
    for c in range(RET_V_W // PROJ_COLS):
        sl = slice(c * PROJ_COLS, (c + 1) * PROJ_COLS)
        sg_ref[:, sl] = _silu(proj(OFF_GR + c * PROJ_COLS, PROJ_COLS)).astype(BF16)
    for c in range(D_MODEL // PROJ_COLS):
        sl = slice(c * PROJ_COLS, (c + 1) * PROJ_COLS)
        g = proj(OFF_GA + c * PROJ_COLS, PROJ_COLS) + bg_ref[:, sl]
        ga_ref[:, sl] = _sigmoid(g).astype(BF16)
    for c in range(D_MODEL // PROJ_COLS):
        sl = slice(c * PROJ_COLS, (c + 1) * PROJ_COLS)
        g = proj(OFF_GB + c * PROJ_COLS, PROJ_COLS) + bg_ref[:, D_MODEL + c * PROJ_COLS:D_MODEL + (c + 1) * PROJ_COLS]
        gb_ref[:, sl] = _sigmoid(g).astype(BF16)
    for c in range(RET_V_W // PROJ_COLS):
        sl = slice(c * PROJ_COLS, (c + 1) * PROJ_COLS)
        vr_ref[:, sl] = proj(OFF_VR + c * PROJ_COLS, PROJ_COLS).astype(BF16)


def _proj_call(x, gain, w_in, b_gates, tables):
    b, s, d = x.shape
    rows = PROJ_ROWS
    nblk = rows // BLOCK
    grid = (s // rows, b)

    def tok(width):
        return pl.BlockSpec((None, rows, width), lambda j, i: (i, j, 0))

    def tok_t(feat):
        return pl.BlockSpec((None, nblk, feat, BLOCK), lambda j, i: (i, j, 0, 0))

    def const(shape):
        return pl.BlockSpec(shape, lambda j, i: (0,) * len(shape), pipeline_mode=pl.Buffered(1))

    pos_t = pl.BlockSpec((ROPE_HALF, rows), lambda j, i: (0, j))
    pos = pl.BlockSpec((rows, LANES), lambda j, i: (j, 0))
    tok_shape = lambda w: jax.ShapeDtypeStruct((b, s, w), BF16)
    tok_t_shape = lambda f: jax.ShapeDtypeStruct((b, s // BLOCK, f, BLOCK), BF16)
    return pl.pallas_call(
        _proj_kernel,
        grid=grid,
        in_specs=[tok(d), const((1, d)), const((d, IN_WIDTH)),
                  const((1, 2 * D_MODEL)), pos_t, pos_t, pos, pos, pos, pos, pos,
                  const((rows, RET_QK_W)), const((rows, RET_QK_W))],
        out_specs=[tok_t(ATT_Q_W), tok(ATT_KV_W), tok_t(ATT_KV_W), tok(RET_QK_W),
                   tok_t(RET_QK_W), tok(RET_V_W), tok(RET_V_W), tok(D_MODEL), tok(D_MODEL)],
        out_shape=[tok_t_shape(ATT_Q_W), tok_shape(ATT_KV_W), tok_t_shape(ATT_KV_W), tok_shape(RET_QK_W),
                   tok_t_shape(RET_QK_W), tok_shape(RET_V_W),
                   tok_shape(RET_V_W), tok_shape(D_MODEL), tok_shape(D_MODEL)],
        compiler_params=pltpu.CompilerParams(
            dimension_semantics=("arbitrary", "arbitrary"), vmem_limit_bytes=V7X_VMEM_LIMIT_BYTES),
        name="proj",
    )(x, gain, w_in, b_gates, *tables)


def _mix_kernel(cdec_ref,
                qat_ref, ka_ref, vat_ref, qdr_ref, kdt_ref, vr_ref, sg_ref, ga_ref, gb_ref, x_ref,
                sink_ref, gn_ref, tril_ref, watt_ref, wret_ref, wout_ref,
                h_ref,
                kext_ref, vext_ref, state_ref, att_ref, ret_ref, merged_ref, *, tiles_per_seq, ntiles):
    t = pl.program_id(0)
    ws = t & 1
    rs = 1 - ws
    nblk = x_ref.shape[0] // BLOCK
    first = (jnp.minimum(t, ntiles - 1) % tiles_per_seq) == 0

    @pl.when(t == 0)
    def _():
        att_ref[1] = jnp.zeros(att_ref.shape[1:], BF16)
        ret_ref[1] = jnp.zeros(ret_ref.shape[1:], BF16)

    @pl.when(first)
    def _():
        kext_ref[:BLOCK, :] = jnp.zeros((BLOCK, ATT_KV_W), BF16)
        vext_ref[0] = jnp.zeros((ATT_KV_W, BLOCK), BF16)
        state_ref[...] = jnp.zeros_like(state_ref)

    kext_ref[BLOCK:, :] = ka_ref[...]
    vext_ref[1:] = vat_ref[...]

    from_prev = (lax.broadcasted_iota(jnp.int32, (BLOCK, BLOCK), 0)
                 > lax.broadcasted_iota(jnp.int32, (BLOCK, BLOCK), 1))

    def band_select(a, b):
        pick = lambda v, g: v[:, g * BLOCK:(g + 1) * BLOCK] if hasattr(v, "shape") else v
        return jnp.concatenate([jnp.where(from_prev, pick(a, g), pick(b, g)) for g in range(ATT_GROUP)], axis=1)
    zeros_q = jnp.zeros((ATT_HEAD_DIM, ATT_GROUP * BLOCK), BF16)

    ksl = [slice(h * RET_KEY_DIM, (h + 1) * RET_KEY_DIM) for h in range(RET_HEADS)]
    vsl = [slice(h * RET_VAL_DIM, (h + 1) * RET_VAL_DIM) for h in range(RET_HEADS)]
    rows_of = lambda c: pl.ds(c * BLOCK, BLOCK)

    def first_matmuls(c):
        rows = rows_of(c)
        qt = qat_ref[c]
        kc = kext_ref[pl.ds(c * BLOCK, 2 * BLOCK), :]
        scores = []
        for kv in range(ATT_KV_HEADS):
            qg = jnp.concatenate(
                [qt[(kv * ATT_GROUP + g) * ATT_HEAD_DIM:(kv * ATT_GROUP + g + 1) * ATT_HEAD_DIM] for g in range(ATT_GROUP)],
                axis=1)
            rhs = jnp.concatenate([qg, zeros_q] if kv == 0 else [zeros_q, qg], axis=0)
            scores.append(jnp.dot(kc, rhs, preferred_element_type=F32))
        zeros_k = jnp.zeros((RET_KEY_DIM, BLOCK), BF16)
        att = []
        for h in range(0, RET_HEADS, 2):
            pair = slice(h * RET_KEY_DIM, (h + 2) * RET_KEY_DIM)
            kdiag = jnp.concatenate([jnp.concatenate([kdt_ref[c, ksl[h], :], zeros_k], axis=1),
                                     jnp.concatenate([zeros_k, kdt_ref[c, ksl[h + 1], :]], axis=1)], axis=0)
            both = jnp.dot(qdr_ref[rows, pair], kdiag, preferred_element_type=F32)
            att += [both[:, :BLOCK], both[:, BLOCK:]]
        return scores, att

    def softmax_and_decay(c, stage1, state):
        scores, att = stage1
        probs, inv_den = [], []
        for kv in range(ATT_KV_HEADS):
            sc = scores[kv]
            prev = sc[:BLOCK]
            if c == 0:
                prev = prev + jnp.where(first, NEG_INF, 0.0).astype(F32)
            f = band_select(prev, sc[BLOCK:])
            sink = sink_ref[kv]
            m = jnp.maximum(jnp.max(f, axis=0, keepdims=True), sink)
            e = jnp.exp2(f - m)
            inv_den.append(1.0 / (jnp.sum(e, axis=0, keepdims=True) + jnp.exp2(sink - m)))
            probs.append(jnp.concatenate([band_select(e, 0.0), band_select(0.0, e)], axis=0).astype(BF16))
        att_b = [(att[h] * tril_ref[h]).astype(BF16) for h in range(RET_HEADS)]
        state_b = [state[h].astype(BF16) for h in range(RET_HEADS)]
        return probs, inv_den, att_b, state_b

    def second_matmuls(c, ops):
        probs, inv_den, att_b, state_b = ops
        rows = rows_of(c)
        vt_prev, vt_cur = vext_ref[c], vext_ref[c + 1]
        outs = []
        for kv in range(ATT_KV_HEADS):
            hs = slice(kv * ATT_HEAD_DIM, (kv + 1) * ATT_HEAD_DIM)
            vt = jnp.concatenate([vt_prev[hs], vt_cur[hs]], axis=1)
            outs.append(jnp.dot(vt, probs[kv], preferred_element_type=F32) * inv_den[kv])
        zeros_k = jnp.zeros((RET_KEY_DIM, RET_KEY_DIM), BF16)
        both = [jnp.dot(jnp.concatenate([jnp.concatenate([att_b[h], qdr_ref[rows, ksl[h]]], axis=1),
                                         jnp.concatenate([kdt_ref[c, ksl[h], :], zeros_k], axis=1)], axis=0),
                        jnp.concatenate([vr_ref[rows, vsl[h]], state_b[h]], axis=0), preferred_element_type=F32)
                for h in range(RET_HEADS)]
        ret = [bh[:BLOCK] for bh in both]
        upd = [bh[BLOCK:] for bh in both]
        return outs, ret, upd

    def finish(c, stage2):
        outs, ret, _ = stage2
        rows = rows_of(c)
        ot = jnp.concatenate(outs, axis=0)
        for g in range(ATT_GROUP):
            att_ref[ws, rows, g * LANES:(g + 1) * LANES] = ot[:, g * BLOCK:(g + 1) * BLOCK].T.astype(BF16)
        for h in range(RET_HEADS):
            out = ret[h]
            mu = jnp.mean(out, axis=-1, keepdims=True)
            dev = out - mu
            var = jnp.mean(dev * dev, axis=-1, keepdims=True)
            yn = dev * lax.rsqrt(var + GN_EPS) * gn_ref[:, vsl[h]]
            ret_ref[ws, rows, vsl[h]] = yn.astype(BF16) * sg_ref[rows, vsl[h]]

    def chunk_tail(c, stage1, state):
        stage2 = second_matmuls(c, softmax_and_decay(c, stage1, state))
        finish(c, stage2)
        return [state[h] * cdec_ref[h] + stage2[2][h] for h in range(RET_HEADS)]

    half = D_MODEL // 2
    cols = [slice(0, half), slice(half, D_MODEL)]

    assert nblk == 4
    state = [state_ref[h] for h in range(RET_HEADS)]
    def merged_half(n):
        ya = jnp.dot(att_ref[rs], watt_ref[:, cols[n]], preferred_element_type=F32)
        yr = jnp.dot(ret_ref[rs], wret_ref[:, cols[n]], preferred_element_type=F32)
        merged_ref[:, cols[n]] = ga_ref[:, cols[n]] * ya.astype(BF16) + gb_ref[:, cols[n]] * yr.astype(BF16)

    def out_half(n):
        h_ref[:, cols[n]] = x_ref[:, cols[n]] + jnp.dot(merged_ref[...], wout_ref[:, cols[n]],
                                                        preferred_element_type=F32)

    s0 = first_matmuls(0)
    s1 = first_matmuls(1)
    merged_half(0)
    state = chunk_tail(0, s0, state)
    s2 = first_matmuls(2)
    merged_half(1)
    state = chunk_tail(1, s1, state)
    s3 = first_matmuls(3)
    state = chunk_tail(2, s2, state)
    out_half(0)
    state = chunk_tail(3, s3, state)
    out_half(1)
    for h in range(RET_HEADS):
        state_ref[h] = state[h]

    kext_ref[:BLOCK, :] = kext_ref[nblk * BLOCK:, :]
    vext_ref[0] = vext_ref[nblk]


def _mix_call(proj_out, x, cdec, sinks, gn_gain, tril, w_att_up, w_ret_up, w_out):
    b, s, d = x.shape
    rows = MIX_ROWS
    nblk = rows // BLOCK
    tps = s // rows
    ntiles = b * tps
    grid = (ntiles + 1,)

    chunk_tile = lambda t: jnp.minimum(t, ntiles - 1)
    proj_tile = lambda t: jnp.maximum(t - 1, 0)

    def tok(width, tile):
        return pl.BlockSpec((None, rows, width), lambda t: (tile(t) // tps, tile(t) % tps, 0))

    def tok_t(feat):
        return pl.BlockSpec((None, nblk, feat, BLOCK), lambda t: (chunk_tile(t) // tps, chunk_tile(t) % tps, 0, 0))

    def const(shape):
        return pl.BlockSpec(shape, lambda t: (0,) * len(shape), pipeline_mode=pl.Buffered(1))

    smem = pl.BlockSpec(memory_space=pltpu.SMEM)
    ctok = lambda width: tok(width, chunk_tile)
    return pl.pallas_call(
        functools.partial(_mix_kernel, tiles_per_seq=tps, ntiles=ntiles),
        grid=grid,
        in_specs=[smem,
                  tok_t(ATT_Q_W), ctok(ATT_KV_W), tok_t(ATT_KV_W), ctok(RET_QK_W),
                  tok_t(RET_QK_W), ctok(RET_V_W), ctok(RET_V_W),
                  tok(D_MODEL, proj_tile), tok(D_MODEL, proj_tile), tok(d, proj_tile),
                  const((ATT_KV_HEADS, 1, ATT_GROUP * BLOCK)), const((1, RET_V_W)),
                  const((RET_HEADS, RET_CHUNK, RET_CHUNK)),
                  const((ATT_Q_W, d)), const((RET_V_W, d)), const((d, d))],
        out_specs=tok(d, proj_tile),
        out_shape=jax.ShapeDtypeStruct((b, s, d), F32),
        scratch_shapes=[
            pltpu.VMEM((BLOCK + rows, ATT_KV_W), BF16),
            pltpu.VMEM((1 + nblk, ATT_KV_W, BLOCK), BF16),
            pltpu.VMEM((RET_HEADS, RET_KEY_DIM, RET_VAL_DIM), F32),
            pltpu.VMEM((2, rows, ATT_Q_W), BF16),
            pltpu.VMEM((2, rows, RET_V_W), BF16),
            pltpu.VMEM((rows, d), BF16),
        ],
        compiler_params=pltpu.CompilerParams(
            dimension_semantics=("arbitrary",), vmem_limit_bytes=V7X_VMEM_LIMIT_BYTES),
        name="mix",
    )(cdec, *proj_out, x, sinks, gn_gain, tril, w_att_up, w_ret_up, w_out)


def _ffn_kernel(h_ref, gain_ref, w1_ref, w2_ref, fgain_ref, o_ref, *, final_norm):
    for s in range(h_ref.shape[0] // FFN_SUB_ROWS):
        rs = slice(s * FFN_SUB_ROWS, (s + 1) * FFN_SUB_ROWS)
        h = h_ref[rs, :]
        xb = _rms_scale(h, gain_ref[...]).astype(BF16)
        acc = jnp.zeros(h.shape, F32)
        for c in range(D_FF // FFN_COLS):
            sl = slice(c * FFN_COLS, (c + 1) * FFN_COLS)
            a = jnp.maximum(jnp.dot(xb, w1_ref[:, sl], preferred_element_type=F32), 0.0)
            acc = acc + jnp.dot((a * a).astype(BF16), w2_ref[sl, :], preferred_element_type=F32)
        y = h + acc
        if final_norm:
            y = _rms_scale(y, fgain_ref[...])
        o_ref[rs, :] = y


def _ffn_call(h, gain, w1, w2, fgain, final_norm):
    b, s, d = h.shape
    rows = FFN_ROWS
    grid = (b, s // rows)

    def const(shape):
        return pl.BlockSpec(shape, lambda i, j: (0,) * len(shape), pipeline_mode=pl.Buffered(1))

    tok = pl.BlockSpec((None, rows, d), lambda i, j: (i, j, 0))
    return pl.pallas_call(
        functools.partial(_ffn_kernel, final_norm=final_norm),
        grid=grid,
        in_specs=[tok, const((1, d)), const((d, D_FF)), const((D_FF, d)), const((1, d))],
        out_specs=tok,
        out_shape=jax.ShapeDtypeStruct((b, s, d), F32),
        compiler_params=pltpu.CompilerParams(
            dimension_semantics=("arbitrary", "arbitrary"), vmem_limit_bytes=V7X_VMEM_LIMIT_BYTES),
        name="ffn",
    )(h, gain, w1, w2, fgain)


def _decay_terms():
    h, c = RET_HEADS, RET_CHUNK
    log_gamma = jnp.log1p(-jnp.exp2(-5.0 - jnp.arange(h, dtype=F32)))
    idx = jnp.arange(c, dtype=F32)
    diff = idx[:, None] - idx[None, :]
    tril = jnp.where(diff >= 0, jnp.exp(-c * log_gamma)[:, None, None], 0.0)
    q_decay = jnp.exp((idx + 1.0)[None, :] * log_gamma[:, None])
    k_decay = jnp.exp((c - 1.0 - idx)[None, :] * log_gamma[:, None])
    chunk_decay = jnp.exp(c * log_gamma)
    return tril, q_decay, k_decay, chunk_decay


def _position_tables(seq_len, q_decay, k_decay):
    pos = jnp.arange(seq_len, dtype=F32)

    def cos_sin(dim, theta):
        inv_freq = theta ** (-jnp.arange(0, dim, 2, dtype=F32) / dim)
        ang = pos[:, None] * inv_freq[None, :]
        return jnp.cos(ang), jnp.sin(ang)

    cos, sin = cos_sin(ROPE_DIM, ROPE_THETA)
    pad = ATT_HEAD_DIM - ROPE_DIM
    zeros_half = jnp.zeros_like(sin)
    one_head = lambda parts: jnp.tile(jnp.concatenate(parts, axis=-1), (1, LANES // ATT_HEAD_DIM))
    ca = one_head([cos, cos, jnp.ones((seq_len, pad), F32)])
    s1 = one_head([-sin, zeros_half, jnp.zeros((seq_len, pad), F32)])
    s2 = one_head([zeros_half, sin, jnp.zeros((seq_len, pad), F32)])
    cos_r, sin_r = cos_sin(RET_KEY_DIM, RET_ROT_BASE)
    cr = jnp.concatenate([cos_r, cos_r], axis=-1)
    sr = jnp.concatenate([-sin_r, sin_r], axis=-1)

    def per_row(decay):
        t = jnp.repeat(decay.T, RET_KEY_DIM, axis=1)
        return jnp.tile(t, (PROJ_ROWS // RET_CHUNK, 1))

    return cos.T, sin.T, ca, s1, s2, cr, sr, per_row(q_decay), per_row(k_decay)


def kernel(x, norm_mix_gain, w_in, b_gates, attn_sinks, ret_gn_gain, w_att_up, w_ret_up, w_out,
           norm_mlp_gain, w_ff1, w_ff2, norm_final_gain):
    b, s, d = x.shape
    depth = w_in.shape[0]
    assert d == D_MODEL and s % MIX_ROWS == 0 and s % PROJ_ROWS == 0 and s % FFN_ROWS == 0
    tril, q_decay, k_decay, cdec = _decay_terms()
    tables = _position_tables(s, q_decay, k_decay)
    row = lambda v: v.reshape(1, -1).astype(F32)
    att_perm = np.arange(ATT_Q_W).reshape(ATT_KV_HEADS, ATT_GROUP, ATT_HEAD_DIM).transpose(1, 0, 2).reshape(-1)
    h = x
    for l in range(depth):
        w = w_in[l].astype(BF16)
        proj_out = _proj_call(h, row(norm_mix_gain[l]), w, row(b_gates[l]), tables)
        sinks = jnp.repeat((attn_sinks[l].astype(F32) * LOG2E).reshape(ATT_KV_HEADS, 1, ATT_GROUP), BLOCK, axis=-1)
        h = _mix_call(proj_out, h, cdec, sinks, row(ret_gn_gain[l]), tril,
                      w_att_up[l][att_perm].astype(BF16), w_ret_up[l].astype(BF16), w_out[l].astype(BF16))
        h = _ffn_call(h, row(norm_mlp_gain[l]), w_ff1[l].astype(BF16), w_ff2[l].astype(BF16),
                      row(norm_final_gain), final_norm=(l == depth - 1))
    return h
```

```python
import functools

import jax
import jax.numpy as jnp
import numpy as np
from jax import lax
from jax.experimental import pallas as pl
from jax.experimental.pallas import tpu as pltpu

D_MODEL = 1024
ATT_HEADS = 8
ATT_KV_HEADS = 2
ATT_GROUP = ATT_HEADS // ATT_KV_HEADS
ATT_HEAD_DIM = 64
WINDOW = 128
ATT_BLOCK = 128
ROPE_DIM = ATT_HEAD_DIM // 4
ROPE_HALF = ROPE_DIM // 2
ROPE_THETA = 500000.0
RET_HEADS = 4
RET_KEY_DIM = 128
RET_VAL_DIM = 256
RET_CHUNK = 128
RET_ROT_BASE = 10000.0
D_FF = 4 * D_MODEL
NORM_EPS = 1e-6
GN_EPS = 1e-6
NEG_INF = -1e30
LOG2E = 1.4426950408889634

ATT_Q_W = ATT_HEADS * ATT_HEAD_DIM
ATT_KV_W = ATT_KV_HEADS * ATT_HEAD_DIM
RET_QK_W = RET_HEADS * RET_KEY_DIM
RET_V_W = RET_HEADS * RET_VAL_DIM
IN_SPLITS = (ATT_Q_W, ATT_KV_W, ATT_KV_W, RET_QK_W, RET_QK_W, RET_V_W, RET_V_W, D_MODEL, D_MODEL)
IN_WIDTH = sum(IN_SPLITS)
(OFF_QA, OFF_KA, OFF_VA, OFF_QR, OFF_KR, OFF_VR, OFF_GR, OFF_GA, OFF_GB) = (
    int(v) for v in np.concatenate([[0], np.cumsum(IN_SPLITS)[:-1]]))

LANES = 128
SUBLANES = 8
V7X_VMEM_LIMIT_BYTES = 56 * 1024 * 1024

BLOCK = 128
PROJ_ROWS = 512
PROJ_COLS = 512
MIX_ROWS = 512
FFN_ROWS = 1024
FFN_SUB_ROWS = 512
FFN_COLS = 512

BF16 = jnp.bfloat16
F32 = jnp.float32

assert WINDOW == ATT_BLOCK == RET_CHUNK == BLOCK == LANES
assert RET_KEY_DIM == LANES and ATT_KV_W == LANES and ROPE_HALF == SUBLANES
assert PROJ_ROWS % BLOCK == 0 and MIX_ROWS % BLOCK == 0


def _sigmoid(x):
    return 0.5 * jnp.tanh(0.5 * x) + 0.5


def _silu(x):
    return x * _sigmoid(x)


def _inv_rms(x):
    return lax.rsqrt(jnp.mean(x * x, axis=-1, keepdims=True) + NORM_EPS)


def _rms_scale(x, gain):
    return (x * _inv_rms(x)) * gain


def _proj_kernel(x_ref, w_ref, bg_ref,
                 cat_ref, sat_ref, ca_ref, s1_ref, s2_ref, cr_ref, sr_ref, qdec_ref, kdec_ref,
                 qat_ref, ka_ref, vat_ref, qdr_ref, kdt_ref, vr_ref, sg_ref, ga_ref, gb_ref):
    rows = x_ref.shape[0]
    nblk = rows // BLOCK
    x = x_ref[...]
    xb = x.astype(BF16)
    inv = _inv_rms(x)

    def proj(off, width):
        return jnp.dot(xb, w_ref[:, off:off + width], preferred_element_type=F32) * inv

    def store_blocks(ref, feat, val):
        for c in range(nblk):
            ref[c, feat, :] = val[:, c * BLOCK:(c + 1) * BLOCK].astype(BF16)

    zt = proj(OFF_QA, ATT_Q_W).T
    cat, sat = cat_ref[...], sat_ref[...]
    for hd in range(ATT_HEADS):
        b0 = hd * ATT_HEAD_DIM
        x1, x2 = zt[b0:b0 + ROPE_HALF], zt[b0 + ROPE_HALF:b0 + ROPE_DIM]
        head = jnp.concatenate([x1 * cat - x2 * sat, x2 * cat + x1 * sat, zt[b0 + ROPE_DIM:b0 + ATT_HEAD_DIM]], axis=0)
        store_blocks(qat_ref, slice(b0, b0 + ATT_HEAD_DIM), head * (ATT_HEAD_DIM ** -0.5 * LOG2E))
    zkv = proj(OFF_KA, 2 * ATT_KV_W)
    store_blocks(vat_ref, slice(None), zkv[:, ATT_KV_W:].T)
    z = zkv[:, :ATT_KV_W]
    ka_ref[...] = (z * ca_ref[...] + pltpu.roll(z, LANES - ROPE_HALF, 1) * s1_ref[...]
                   + pltpu.roll(z, ROPE_HALF, 1) * s2_ref[...]).astype(BF16)

    cr, sr = cr_ref[...], sr_ref[...]

    def rope_ret(zh):
        return zh * cr + pltpu.roll(zh, RET_KEY_DIM // 2, 1) * sr

    z = proj(OFF_QR, RET_QK_W)
    for h in range(RET_HEADS):
        sl = slice(h * RET_KEY_DIM, (h + 1) * RET_KEY_DIM)
        qdr_ref[:, sl] = (rope_ret(z[:, sl]) * qdec_ref[:, sl]).astype(BF16)
    z = proj(OFF_KR, RET_QK_W)
    for h in range(RET_HEADS):
        sl = slice(h * RET_KEY_DIM, (h + 1) * RET_KEY_DIM)
        k = rope_ret(z[:, sl]) * (RET_KEY_DIM ** -0.5)
        store_blocks(kdt_ref, sl, (k * kdec_ref[:, sl]).T)
    for c in range(RET_V_W // PROJ_COLS):
        sl = slice(c * PROJ_COLS, (c + 1) * PROJ_COLS)
        sg_ref[:, sl] = _silu(proj(OFF_GR + c * PROJ_COLS, PROJ_COLS)).astype(BF16)
    for c in range(D_MODEL // PROJ_COLS):
        sl = slice(c * PROJ_COLS, (c + 1) * PROJ_COLS)
        g = proj(OFF_GA + c * PROJ_COLS, PROJ_COLS) + bg_ref[:, sl]
        ga_ref[:, sl] = _sigmoid(g).astype(BF16)
    for c in range(D_MODEL // PROJ_COLS):
        sl = slice(c * PROJ_COLS, (c + 1) * PROJ_COLS)
        g = proj(OFF_GB + c * PROJ_COLS, PROJ_COLS) + bg_ref[:, D_MODEL + c * PROJ_COLS:D_MODEL + (c + 1) * PROJ_COLS]
        gb_ref[:, sl] = _sigmoid(g).astype(BF16)
    for c in range(RET_V_W // PROJ_COLS):
        sl = slice(c * PROJ_COLS, (c + 1) * PROJ_COLS)
        vr_ref[:, sl] = proj(OFF_VR + c * PROJ_COLS, PROJ_COLS).astype(BF16)


def _proj_call(x, w_in, b_gates, tables):
    b, s, d = x.shape
    rows = PROJ_ROWS
    nblk = rows // BLOCK
    grid = (s // rows, b)

    def tok(width):
        return pl.BlockSpec((None, rows, width), lambda j, i: (i, j, 0))

    def tok_t(feat):
        return pl.BlockSpec((None, nblk, feat, BLOCK), lambda j, i: (i, j, 0, 0))

    def const(shape):
        return pl.BlockSpec(shape, lambda j, i: (0,) * len(shape), pipeline_mode=pl.Buffered(1))

    pos_t = pl.BlockSpec((ROPE_HALF, rows), lambda j, i: (0, j))
    pos = pl.BlockSpec((rows, LANES), lambda j, i: (j, 0))
    tok_shape = lambda w: jax.ShapeDtypeStruct((b, s, w), BF16)
    tok_t_shape = lambda f: jax.ShapeDtypeStruct((b, s // BLOCK, f, BLOCK), BF16)
    return pl.pallas_call(
        _proj_kernel,
        grid=grid,
        in_specs=[tok(d), const((d, IN_WIDTH)),
                  const((1, 2 * D_MODEL)), pos_t, pos_t, pos, pos, pos, pos, pos,
                  const((rows, RET_QK_W)), const((rows, RET_QK_W))],
        out_specs=[tok_t(ATT_Q_W), tok(ATT_KV_W), tok_t(ATT_KV_W), tok(RET_QK_W),
                   tok_t(RET_QK_W), tok(RET_V_W), tok(RET_V_W), tok(D_MODEL), tok(D_MODEL)],
        out_shape=[tok_t_shape(ATT_Q_W), tok_shape(ATT_KV_W), tok_t_shape(ATT_KV_W), tok_shape(RET_QK_W),
                   tok_t_shape(RET_QK_W), tok_shape(RET_V_W),
                   tok_shape(RET_V_W), tok_shape(D_MODEL), tok_shape(D_MODEL)],
        compiler_params=pltpu.CompilerParams(
            dimension_semantics=("arbitrary", "arbitrary"), vmem_limit_bytes=V7X_VMEM_LIMIT_BYTES),
        name="proj",
    )(x, w_in, b_gates, *tables)


def _mix_kernel(cdec_ref,
                qat_ref, ka_ref, vat_ref, qdr_ref, kdt_ref, vr_ref, sg_ref, ga_ref, gb_ref, x_ref,
                sink_ref, gn_ref, tril_ref, watt_ref, wret_ref, wout_ref,
                h_ref,
                kext_ref, vext_ref, state_ref, att_ref, ret_ref, merged_ref, *, tiles_per_seq, ntiles):
    t = pl.program_id(0)
    ws = t & 1
    rs = 1 - ws
    nblk = x_ref.shape[0] // BLOCK
    first = (jnp.minimum(t, ntiles - 1) % tiles_per_seq) == 0

    @pl.when(t == 0)
    def _():
        att_ref[1] = jnp.zeros(att_ref.shape[1:], BF16)
        ret_ref[1] = jnp.zeros(ret_ref.shape[1:], BF16)

    @pl.when(first)
    def _():
        kext_ref[:BLOCK, :] = jnp.zeros((BLOCK, ATT_KV_W), BF16)
        vext_ref[0] = jnp.zeros((ATT_KV_W, BLOCK), BF16)
        state_ref[...] = jnp.zeros_like(state_ref)

    kext_ref[BLOCK:, :] = ka_ref[...]
    vext_ref[1:] = vat_ref[...]

    from_prev = (lax.broadcasted_iota(jnp.int32, (BLOCK, BLOCK), 0)
                 > lax.broadcasted_iota(jnp.int32, (BLOCK, BLOCK), 1))

    def band_select(a, b):
        pick = lambda v, g: v[:, g * BLOCK:(g + 1) * BLOCK] if hasattr(v, "shape") else v
        return jnp.concatenate([jnp.where(from_prev, pick(a, g), pick(b, g)) for g in range(ATT_GROUP)], axis=1)
    zeros_q = jnp.zeros((ATT_HEAD_DIM, ATT_GROUP * BLOCK), BF16)

    ksl = [slice(h * RET_KEY_DIM, (h + 1) * RET_KEY_DIM) for h in range(RET_HEADS)]
    vsl = [slice(h * RET_VAL_DIM, (h + 1) * RET_VAL_DIM) for h in range(RET_HEADS)]
    rows_of = lambda c: pl.ds(c * BLOCK, BLOCK)

    def first_matmuls(c):
        rows = rows_of(c)
        qt = qat_ref[c]
        kc = kext_ref[pl.ds(c * BLOCK, 2 * BLOCK), :]
        scores = []
        for kv in range(ATT_KV_HEADS):
            qg = jnp.concatenate(
                [qt[(kv * ATT_GROUP + g) * ATT_HEAD_DIM:(kv * ATT_GROUP + g + 1) * ATT_HEAD_DIM] for g in range(ATT_GROUP)],
                axis=1)
            rhs = jnp.concatenate([qg, zeros_q] if kv == 0 else [zeros_q, qg], axis=0)
            scores.append(jnp.dot(kc, rhs, preferred_element_type=F32))
        zeros_k = jnp.zeros((RET_KEY_DIM, BLOCK), BF16)
        att = []
        for h in range(0, RET_HEADS, 2):
            pair = slice(h * RET_KEY_DIM, (h + 2) * RET_KEY_DIM)
            kdiag = jnp.concatenate([jnp.concatenate([kdt_ref[c, ksl[h], :], zeros_k], axis=1),
                                     jnp.concatenate([zeros_k, kdt_ref[c, ksl[h + 1], :]], axis=1)], axis=0)
            both = jnp.dot(qdr_ref[rows, pair], kdiag, preferred_element_type=F32)
            att += [both[:, :BLOCK], both[:, BLOCK:]]
        return scores, att

    def softmax_and_decay(c, stage1, state):
        scores, att = stage1
        probs, inv_den = [], []
        for kv in range(ATT_KV_HEADS):
            sc = scores[kv]
            prev = sc[:BLOCK]
            if c == 0:
                prev = prev + jnp.where(first, NEG_INF, 0.0).astype(F32)
            f = band_select(prev, sc[BLOCK:])
            sink = sink_ref[kv]
            m = jnp.maximum(jnp.max(f, axis=0, keepdims=True), sink)
            e = jnp.exp2(f - m)
            inv_den.append(1.0 / (jnp.sum(e, axis=0, keepdims=True) + jnp.exp2(sink - m)))
            probs.append(jnp.concatenate([band_select(e, 0.0), band_select(0.0, e)], axis=0).astype(BF16))
        att_b = [(att[h] * tril_ref[h]).astype(BF16) for h in range(RET_HEADS)]
        state_b = [state[h].astype(BF16) for h in range(RET_HEADS)]
        return probs, inv_den, att_b, state_b

    def second_matmuls(c, ops):
        probs, inv_den, att_b, state_b = ops
        rows = rows_of(c)
        vt_prev, vt_cur = vext_ref[c], vext_ref[c + 1]
        outs = []
        for kv in range(ATT_KV_HEADS):
            hs = slice(kv * ATT_HEAD_DIM, (kv + 1) * ATT_HEAD_DIM)
            vt = jnp.concatenate([vt_prev[hs], vt_cur[hs]], axis=1)
            outs.append(jnp.dot(vt, probs[kv], preferred_element_type=F32) * inv_den[kv])
        zeros_k = jnp.zeros((RET_KEY_DIM, RET_KEY_DIM), BF16)
        both = [jnp.dot(jnp.concatenate([jnp.concatenate([att_b[h], qdr_ref[rows, ksl[h]]], axis=1),
                                         jnp.concatenate([kdt_ref[c, ksl[h], :], zeros_k], axis=1)], axis=0),
                        jnp.concatenate([vr_ref[rows, vsl[h]], state_b[h]], axis=0), preferred_element_type=F32)
                for h in range(RET_HEADS)]
        ret = [bh[:BLOCK] for bh in both]
        upd = [bh[BLOCK:] for bh in both]
        return outs, ret, upd

    def finish(c, stage2):
        outs, ret, _ = stage2
        rows = rows_of(c)
        ot = jnp.concatenate(outs, axis=0)
        for g in range(ATT_GROUP):
            att_ref[ws, rows, g * LANES:(g + 1) * LANES] = ot[:, g * BLOCK:(g + 1) * BLOCK].T.astype(BF16)
        for h in range(RET_HEADS):
            out = ret[h]
            mu = jnp.mean(out, axis=-1, keepdims=True)
            dev = out - mu
            var = jnp.mean(dev * dev, axis=-1, keepdims=True)
            yn = dev * lax.rsqrt(var + GN_EPS) * gn_ref[:, vsl[h]]
            ret_ref[ws, rows, vsl[h]] = yn.astype(BF16) * sg_ref[rows, vsl[h]]

    def chunk_tail(c, stage1, state):
        stage2 = second_matmuls(c, softmax_and_decay(c, stage1, state))
        finish(c, stage2)
        return [state[h] * cdec_ref[h] + stage2[2][h] for h in range(RET_HEADS)]

    half = D_MODEL // 2
    cols = [slice(0, half), slice(half, D_MODEL)]

    assert nblk == 4
    state = [state_ref[h] for h in range(RET_HEADS)]
    def merged_half(n):
        ya = jnp.dot(att_ref[rs], watt_ref[:, cols[n]], preferred_element_type=F32)
        yr = jnp.dot(ret_ref[rs], wret_ref[:, cols[n]], preferred_element_type=F32)
        merged_ref[:, cols[n]] = ga_ref[:, cols[n]] * ya.astype(BF16) + gb_ref[:, cols[n]] * yr.astype(BF16)

    def out_half(n):
        h_ref[:, cols[n]] = x_ref[:, cols[n]] + jnp.dot(merged_ref[...], wout_ref[:, cols[n]],
                                                        preferred_element_type=F32)

    s0 = first_matmuls(0)
    s1 = first_matmuls(1)
    merged_half(0)
    state = chunk_tail(0, s0, state)
    s2 = first_matmuls(2)
    merged_half(1)
    state = chunk_tail(1, s1, state)
    s3 = first_matmuls(3)
    state = chunk_tail(2, s2, state)
    out_half(0)
    state = chunk_tail(3, s3, state)
    out_half(1)
    for h in range(RET_HEADS):
        state_ref[h] = state[h]

    kext_ref[:BLOCK, :] = kext_ref[nblk * BLOCK:, :]
    vext_ref[0] = vext_ref[nblk]


def _mix_call(proj_out, x, cdec, sinks, gn_gain, tril, w_att_up, w_ret_up, w_out):
    b, s, d = x.shape
    rows = MIX_ROWS
    nblk = rows // BLOCK
    tps = s // rows
    ntiles = b * tps
    grid = (ntiles + 1,)

    chunk_tile = lambda t: jnp.minimum(t, ntiles - 1)
    proj_tile = lambda t: jnp.maximum(t - 1, 0)

    def tok(width, tile):
        return pl.BlockSpec((None, rows, width), lambda t: (tile(t) // tps, tile(t) % tps, 0))

    def tok_t(feat):
        return pl.BlockSpec((None, nblk, feat, BLOCK), lambda t: (chunk_tile(t) // tps, chunk_tile(t) % tps, 0, 0))

    def const(shape):
        return pl.BlockSpec(shape, lambda t: (0,) * len(shape), pipeline_mode=pl.Buffered(1))

    smem = pl.BlockSpec(memory_space=pltpu.SMEM)
    ctok = lambda width: tok(width, chunk_tile)
    return pl.pallas_call(
        functools.partial(_mix_kernel, tiles_per_seq=tps, ntiles=ntiles),
        grid=grid,
        in_specs=[smem,
                  tok_t(ATT_Q_W), ctok(ATT_KV_W), tok_t(ATT_KV_W), ctok(RET_QK_W),
                  tok_t(RET_QK_W), ctok(RET_V_W), ctok(RET_V_W),
                  tok(D_MODEL, proj_tile), tok(D_MODEL, proj_tile), tok(d, proj_tile),
                  const((ATT_KV_HEADS, 1, ATT_GROUP * BLOCK)), const((1, RET_V_W)),
                  const((RET_HEADS, RET_CHUNK, RET_CHUNK)),
                  const((ATT_Q_W, d)), const((RET_V_W, d)), const((d, d))],
        out_specs=tok(d, proj_tile),
        out_shape=jax.ShapeDtypeStruct((b, s, d), F32),
        scratch_shapes=[
            pltpu.VMEM((BLOCK + rows, ATT_KV_W), BF16),
            pltpu.VMEM((1 + nblk, ATT_KV_W, BLOCK), BF16),
            pltpu.VMEM((RET_HEADS, RET_KEY_DIM, RET_VAL_DIM), F32),
            pltpu.VMEM((2, rows, ATT_Q_W), BF16),
            pltpu.VMEM((2, rows, RET_V_W), BF16),
            pltpu.VMEM((rows, d), BF16),
        ],
        compiler_params=pltpu.CompilerParams(
            dimension_semantics=("arbitrary",), vmem_limit_bytes=V7X_VMEM_LIMIT_BYTES),
        name="mix",
    )(cdec, *proj_out, x, sinks, gn_gain, tril, w_att_up, w_ret_up, w_out)


def _ffn_kernel(h_ref, w1_ref, w2_ref, fgain_ref, o_ref, *, final_norm):
    for s in range(h_ref.shape[0] // FFN_SUB_ROWS):
        rs = slice(s * FFN_SUB_ROWS, (s + 1) * FFN_SUB_ROWS)
        h = h_ref[rs, :]
        hb = h.astype(BF16)
        inv = _inv_rms(h)
        acc = jnp.zeros(h.shape, F32)
        for c in range(D_FF // FFN_COLS):
            sl = slice(c * FFN_COLS, (c + 1) * FFN_COLS)
            a = jnp.maximum(jnp.dot(hb, w1_ref[:, sl], preferred_element_type=F32) * inv, 0.0)
            acc = acc + jnp.dot((a * a).astype(BF16), w2_ref[sl, :], preferred_element_type=F32)
        y = h + acc
        if final_norm:
            y = _rms_scale(y, fgain_ref[...])
        o_ref[rs, :] = y


def _ffn_call(h, w1, w2, fgain, final_norm):
    b, s, d = h.shape
    rows = FFN_ROWS
    grid = (b, s // rows)

    def const(shape):
        return pl.BlockSpec(shape, lambda i, j: (0,) * len(shape), pipeline_mode=pl.Buffered(1))

    tok = pl.BlockSpec((None, rows, d), lambda i, j: (i, j, 0))
    return pl.pallas_call(
        functools.partial(_ffn_kernel, final_norm=final_norm),
        grid=grid,
        in_specs=[tok, const((d, D_FF)), const((D_FF, d)), const((1, d))],
        out_specs=tok,
        out_shape=jax.ShapeDtypeStruct((b, s, d), F32),
        compiler_params=pltpu.CompilerParams(
            dimension_semantics=("arbitrary", "arbitrary"), vmem_limit_bytes=V7X_VMEM_LIMIT_BYTES),
        name="ffn",
    )(h, w1, w2, fgain)


def _decay_terms():
    h, c = RET_HEADS, RET_CHUNK
    log_gamma = jnp.log1p(-jnp.exp2(-5.0 - jnp.arange(h, dtype=F32)))
    idx = jnp.arange(c, dtype=F32)
    diff = idx[:, None] - idx[None, :]
    tril = jnp.where(diff >= 0, jnp.exp(-c * log_gamma)[:, None, None], 0.0)
    q_decay = jnp.exp((idx + 1.0)[None, :] * log_gamma[:, None])
    k_decay = jnp.exp((c - 1.0 - idx)[None, :] * log_gamma[:, None])
    chunk_decay = jnp.exp(c * log_gamma)
    return tril, q_decay, k_decay, chunk_decay


def _position_tables(seq_len, q_decay, k_decay):
    pos = jnp.arange(seq_len, dtype=F32)

    def cos_sin(dim, theta):
        inv_freq = theta ** (-jnp.arange(0, dim, 2, dtype=F32) / dim)
        ang = pos[:, None] * inv_freq[None, :]
        return jnp.cos(ang), jnp.sin(ang)

    cos, sin = cos_sin(ROPE_DIM, ROPE_THETA)
    pad = ATT_HEAD_DIM - ROPE_DIM
    zeros_half = jnp.zeros_like(sin)
    one_head = lambda parts: jnp.tile(jnp.concatenate(parts, axis=-1), (1, LANES // ATT_HEAD_DIM))
    ca = one_head([cos, cos, jnp.ones((seq_len, pad), F32)])
    s1 = one_head([-sin, zeros_half, jnp.zeros((seq_len, pad), F32)])
    s2 = one_head([zeros_half, sin, jnp.zeros((seq_len, pad), F32)])
    cos_r, sin_r = cos_sin(RET_KEY_DIM, RET_ROT_BASE)
    cr = jnp.concatenate([cos_r, cos_r], axis=-1)
    sr = jnp.concatenate([-sin_r, sin_r], axis=-1)

    def per_row(decay):
        t = jnp.repeat(decay.T, RET_KEY_DIM, axis=1)
        return jnp.tile(t, (PROJ_ROWS // RET_CHUNK, 1))

    return cos.T, sin.T, ca, s1, s2, cr, sr, per_row(q_decay), per_row(k_decay)


def kernel(x, norm_mix_gain, w_in, b_gates, attn_sinks, ret_gn_gain, w_att_up, w_ret_up, w_out,
           norm_mlp_gain, w_ff1, w_ff2, norm_final_gain):
    b, s, d = x.shape
    depth = w_in.shape[0]
    assert d == D_MODEL and s % MIX_ROWS == 0 and s % PROJ_ROWS == 0 and s % FFN_ROWS == 0
    tril, q_decay, k_decay, cdec = _decay_terms()
    tables = _position_tables(s, q_decay, k_decay)
    row = lambda v: v.reshape(1, -1).astype(F32)
    att_perm = np.arange(ATT_Q_W).reshape(ATT_KV_HEADS, ATT_GROUP, ATT_HEAD_DIM).transpose(1, 0, 2).reshape(-1)
    h = x
    for l in range(depth):
        w = (w_in[l] * norm_mix_gain[l].astype(F32)[:, None]).astype(BF16)
        proj_out = _proj_call(h, w, row(b_gates[l]), tables)
        sinks = jnp.repeat((attn_sinks[l].astype(F32) * LOG2E).reshape(ATT_KV_HEADS, 1, ATT_GROUP), BLOCK, axis=-1)
        h = _mix_call(proj_out, h, cdec, sinks, row(ret_gn_gain[l]), tril,
                      w_att_up[l][att_perm].astype(BF16), w_ret_up[l].astype(BF16), w_out[l].astype(BF16))
        w1 = (w_ff1[l] * norm_mlp_gain[l].astype(F32)[:, None]).astype(BF16)
        h = _ffn_call(h, w1, w_ff2[l].astype(BF16),
                      row(norm_final_gain), final_norm=(l == depth - 1))
    return h
```

```python
import functools

import jax
import jax.numpy as jnp
import numpy as np
from jax import lax
from jax.experimental import pallas as pl
from jax.experimental.pallas import tpu as pltpu

D_MODEL = 1024
ATT_HEADS = 8
ATT_KV_HEADS = 2
ATT_GROUP = ATT_HEADS // ATT_KV_HEADS
ATT_HEAD_DIM = 64
WINDOW = 128
ATT_BLOCK = 128
ROPE_DIM = ATT_HEAD_DIM // 4
ROPE_HALF = ROPE_DIM // 2
ROPE_THETA = 500000.0
RET_HEADS = 4
RET_KEY_DIM = 128
RET_VAL_DIM = 256
RET_CHUNK = 128
RET_ROT_BASE = 10000.0
D_FF = 4 * D_MODEL
NORM_EPS = 1e-6
GN_EPS = 1e-6
NEG_INF = -1e30
LOG2E = 1.4426950408889634

ATT_Q_W = ATT_HEADS * ATT_HEAD_DIM
ATT_KV_W = ATT_KV_HEADS * ATT_HEAD_DIM
RET_QK_W = RET_HEADS * RET_KEY_DIM
RET_V_W = RET_HEADS * RET_VAL_DIM
IN_SPLITS = (ATT_Q_W, ATT_KV_W, ATT_KV_W, RET_QK_W, RET_QK_W, RET_V_W, RET_V_W, D_MODEL, D_MODEL)
IN_WIDTH = sum(IN_SPLITS)
(OFF_QA, OFF_KA, OFF_VA, OFF_QR, OFF_KR, OFF_VR, OFF_GR, OFF_GA, OFF_GB) = (
    int(v) for v in np.concatenate([[0], np.cumsum(IN_SPLITS)[:-1]]))

LANES = 128
SUBLANES = 8
V7X_VMEM_LIMIT_BYTES = 56 * 1024 * 1024

BLOCK = 128
PROJ_ROWS = 512
PROJ_COLS = 512
MIX_ROWS = 512
FFN_ROWS = 1024
FFN_SUB_ROWS = 512
FFN_COLS = 512

BF16 = jnp.bfloat16
F32 = jnp.float32

assert WINDOW == ATT_BLOCK == RET_CHUNK == BLOCK == LANES
assert RET_KEY_DIM == LANES and ATT_KV_W == LANES and ROPE_HALF == SUBLANES
assert PROJ_ROWS % BLOCK == 0 and MIX_ROWS % BLOCK == 0


def _sigmoid(x):
    return 0.5 * jnp.tanh(0.5 * x) + 0.5


def _silu(x):
    return x * _sigmoid(x)


def _inv_rms(x):
    return lax.rsqrt(jnp.mean(x * x, axis=-1, keepdims=True) + NORM_EPS)


def _rms_scale(x, gain):
    return (x * _inv_rms(x)) * gain


def _proj_kernel(x_ref, w_ref, bg_ref,
                 cat_ref, sat_ref, ca_ref, s1_ref, s2_ref, cr_ref, sr_ref, qdec_ref, kdec_ref,
                 qat_ref, ka_ref, vat_ref, qdr_ref, kdt_ref, vr_ref, sg_ref, ga_ref, gb_ref):
    rows = x_ref.shape[0]
    nblk = rows // BLOCK
    x = x_ref[...]
    xb = x.astype(BF16)
    inv = _inv_rms(x)

    def proj(off, width):
        return jnp.dot(xb, w_ref[:, off:off + width], preferred_element_type=F32) * inv

    def store_blocks(ref, feat, val):
        for c in range(nblk):
            ref[c, feat, :] = val[:, c * BLOCK:(c + 1) * BLOCK].astype(BF16)

    zt = proj(OFF_QA, ATT_Q_W).T
    cat, sat = cat_ref[...], sat_ref[...]
    for hd in range(ATT_HEADS):
        b0 = hd * ATT_HEAD_DIM
        x1, x2 = zt[b0:b0 + ROPE_HALF], zt[b0 + ROPE_HALF:b0 + ROPE_DIM]
        head = jnp.concatenate([x1 * cat - x2 * sat, x2 * cat + x1 * sat, zt[b0 + ROPE_DIM:b0 + ATT_HEAD_DIM]], axis=0)
        store_blocks(qat_ref, slice(b0, b0 + ATT_HEAD_DIM), head * (ATT_HEAD_DIM ** -0.5 * LOG2E))
    zkv = proj(OFF_KA, 2 * ATT_KV_W)
    store_blocks(vat_ref, slice(None), zkv[:, ATT_KV_W:].T)
    z = zkv[:, :ATT_KV_W]
    ka_ref[...] = (z * ca_ref[...] + pltpu.roll(z, LANES - ROPE_HALF, 1) * s1_ref[...]
                   + pltpu.roll(z, ROPE_HALF, 1) * s2_ref[...]).astype(BF16)

    cr, sr = cr_ref[...], sr_ref[...]

    def rope_ret(zh):
        return zh * cr + pltpu.roll(zh, RET_KEY_DIM // 2, 1) * sr

    z = proj(OFF_QR, RET_QK_W)
    for h in range(RET_HEADS):
        sl = slice(h * RET_KEY_DIM, (h + 1) * RET_KEY_DIM)
        qdr_ref[:, sl] = (rope_ret(z[:, sl]) * qdec_ref[:, sl]).astype(BF16)
    z = proj(OFF_KR, RET_QK_W)
    for h in range(RET_HEADS):
        sl = slice(h * RET_KEY_DIM, (h + 1) * RET_KEY_DIM)
        k = rope_ret(z[:, sl]) * (RET_KEY_DIM ** -0.5)
        store_blocks(kdt_ref, sl, (k * kdec_ref[:, sl]).T)
    for c in range(RET_V_W // PROJ_COLS):
        sl = slice(c * PROJ_COLS, (c + 1) * PROJ_COLS)
        sg_ref[:, sl] = _silu(proj(OFF_GR + c * PROJ_COLS, PROJ_COLS)).astype(BF16)
    for c in range(D_MODEL // PROJ_COLS):
        sl = slice(c * PROJ_COLS, (c + 1) * PROJ_COLS)
        g = proj(OFF_GA + c * PROJ_COLS, PROJ_COLS) + bg_ref[:, sl]
        ga_ref[:, sl] = _sigmoid(g).astype(BF16)
    for c in range(D_MODEL // PROJ_COLS):
        sl = slice(c * PROJ_COLS, (c + 1) * PROJ_COLS)
        g = proj(OFF_GB + c * PROJ_COLS, PROJ_COLS) + bg_ref[:, D_MODEL + c * PROJ_COLS:D_MODEL + (c + 1) * PROJ_COLS]
        gb_ref[:, sl] = _sigmoid(g).astype(BF16)
    for c in range(RET_V_W // PROJ_COLS):
        sl = slice(c * PROJ_COLS, (c + 1) * PROJ_COLS)
        vr_ref[:, sl] = proj(OFF_VR + c * PROJ_COLS, PROJ_COLS).astype(BF16)


def _proj_call(x, w_in, b_gates, tables):
    b, s, d = x.shape
    rows = PROJ_ROWS
    nblk = rows // BLOCK
    grid = (s // rows, b)

    def tok(width):
        return pl.BlockSpec((None, rows, width), lambda j, i: (i, j, 0))

    def tok_t(feat):
        return pl.BlockSpec((None, nblk, feat, BLOCK), lambda j, i: (i, j, 0, 0))

    def const(shape):
        return pl.BlockSpec(shape, lambda j, i: (0,) * len(shape), pipeline_mode=pl.Buffered(1))

    pos_t = pl.BlockSpec((ROPE_HALF, rows), lambda j, i: (0, j))
    pos = pl.BlockSpec((rows, LANES), lambda j, i: (j, 0))
    tok_shape = lambda w: jax.ShapeDtypeStruct((b, s, w), BF16)
    tok_t_shape = lambda f: jax.ShapeDtypeStruct((b, s // BLOCK, f, BLOCK), BF16)
    return pl.pallas_call(
        _proj_kernel,
        grid=grid,
        in_specs=[tok(d), const((d, IN_WIDTH)),
                  const((1, 2 * D_MODEL)), pos_t, pos_t, pos, pos, pos, pos, pos,
                  const((rows, RET_QK_W)), const((rows, RET_QK_W))],
        out_specs=[tok_t(ATT_Q_W), tok(ATT_KV_W), tok_t(ATT_KV_W), tok(RET_QK_W),
                   tok_t(RET_QK_W), tok(RET_V_W), tok(RET_V_W), tok(D_MODEL), tok(D_MODEL)],
        out_shape=[tok_t_shape(ATT_Q_W), tok_shape(ATT_KV_W), tok_t_shape(ATT_KV_W), tok_shape(RET_QK_W),
                   tok_t_shape(RET_QK_W), tok_shape(RET_V_W),
                   tok_shape(RET_V_W), tok_shape(D_MODEL), tok_shape(D_MODEL)],
        compiler_params=pltpu.CompilerParams(
            dimension_semantics=("arbitrary", "arbitrary"), vmem_limit_bytes=V7X_VMEM_LIMIT_BYTES),
        name="proj",
    )(x, w_in, b_gates, *tables)


def _mix_kernel(cdec_ref,
                qat_ref, ka_ref, vat_ref, qdr_ref, kdt_ref, vr_ref, sg_ref, ga_ref, gb_ref, x_ref,
                sink_ref, gn_ref, tril_ref, watt_ref, wret_ref, wout_ref,
                h_ref,
                kext_ref, vext_ref, state_ref, att_ref, ret_ref, merged_ref, *, tiles_per_seq, ntiles):
    t = pl.program_id(0)
    ws = t & 1
    rs = 1 - ws
    nblk = x_ref.shape[0] // BLOCK
    first = (jnp.minimum(t, ntiles - 1) % tiles_per_seq) == 0

    @pl.when(t == 0)
    def _():
        att_ref[1] = jnp.zeros(att_ref.shape[1:], BF16)
        ret_ref[1] = jnp.zeros(ret_ref.shape[1:], BF16)

    @pl.when(first)
    def _():
        kext_ref[:BLOCK, :] = jnp.zeros((BLOCK, ATT_KV_W), BF16)
        vext_ref[0] = jnp.zeros((ATT_KV_W, BLOCK), BF16)
        state_ref[...] = jnp.zeros_like(state_ref)

    kext_ref[BLOCK:, :] = ka_ref[...]
    vext_ref[1:] = vat_ref[...]

    from_prev = (lax.broadcasted_iota(jnp.int32, (BLOCK, BLOCK), 0)
                 > lax.broadcasted_iota(jnp.int32, (BLOCK, BLOCK), 1))

    def band_select(a, b):
        pick = lambda v, g: v[:, g * BLOCK:(g + 1) * BLOCK] if hasattr(v, "shape") else v
        return jnp.concatenate([jnp.where(from_prev, pick(a, g), pick(b, g)) for g in range(ATT_GROUP)], axis=1)
    zeros_q = jnp.zeros((ATT_HEAD_DIM, ATT_GROUP * BLOCK), BF16)

    ksl = [slice(h * RET_KEY_DIM, (h + 1) * RET_KEY_DIM) for h in range(RET_HEADS)]
    vsl = [slice(h * RET_VAL_DIM, (h + 1) * RET_VAL_DIM) for h in range(RET_HEADS)]
    rows_of = lambda c: pl.ds(c * BLOCK, BLOCK)

    def first_matmuls(c):
        rows = rows_of(c)
        qt = qat_ref[c]
        kc = kext_ref[pl.ds(c * BLOCK, 2 * BLOCK), :]
        scores = []
        for kv in range(ATT_KV_HEADS):
            qg = jnp.concatenate(
                [qt[(kv * ATT_GROUP + g) * ATT_HEAD_DIM:(kv * ATT_GROUP + g + 1) * ATT_HEAD_DIM] for g in range(ATT_GROUP)],
                axis=1)
            rhs = jnp.concatenate([qg, zeros_q] if kv == 0 else [zeros_q, qg], axis=0)
            scores.append(jnp.dot(kc, rhs, preferred_element_type=F32))
        zeros_k = jnp.zeros((RET_KEY_DIM, BLOCK), BF16)
        att = []
        for h in range(0, RET_HEADS, 2):
            pair = slice(h * RET_KEY_DIM, (h + 2) * RET_KEY_DIM)
            kdiag = jnp.concatenate([jnp.concatenate([kdt_ref[c, ksl[h], :], zeros_k], axis=1),
                                     jnp.concatenate([zeros_k, kdt_ref[c, ksl[h + 1], :]], axis=1)], axis=0)
            both = jnp.dot(qdr_ref[rows, pair], kdiag, preferred_element_type=F32)
            att += [both[:, :BLOCK], both[:, BLOCK:]]
        return scores, att

    def softmax_and_decay(c, stage1, state):
        scores, att = stage1
        probs, inv_den = [], []
        for kv in range(ATT_KV_HEADS):
            sc = scores[kv]
            prev = sc[:BLOCK]
            if c == 0:
                prev = prev + jnp.where(first, NEG_INF, 0.0).astype(F32)
            f = band_select(prev, sc[BLOCK:])
            sink = sink_ref[kv]
            m = jnp.maximum(jnp.max(f, axis=0, keepdims=True), sink)
            e = jnp.exp2(f - m)
            inv_den.append(1.0 / (jnp.sum(e, axis=0, keepdims=True) + jnp.exp2(sink - m)))
            probs.append(jnp.concatenate([band_select(e, 0.0), band_select(0.0, e)], axis=0).astype(BF16))
        att_b = [(att[h] * tril_ref[h]).astype(BF16) for h in range(RET_HEADS)]
        state_b = [state[h].astype(BF16) for h in range(RET_HEADS)]
        return probs, inv_den, att_b, state_b

    def second_matmuls(c, ops):
        probs, inv_den, att_b, state_b = ops
        rows = rows_of(c)
        vt_prev, vt_cur = vext_ref[c], vext_ref[c + 1]
        outs = []
        for kv in range(ATT_KV_HEADS):
            hs = slice(kv * ATT_HEAD_DIM, (kv + 1) * ATT_HEAD_DIM)
            vt = jnp.concatenate([vt_prev[hs], vt_cur[hs]], axis=1)
            outs.append(jnp.dot(vt, probs[kv], preferred_element_type=F32) * inv_den[kv])
        zeros_k = jnp.zeros((RET_KEY_DIM, RET_KEY_DIM), BF16)
        both = [jnp.dot(jnp.concatenate([jnp.concatenate([att_b[h], qdr_ref[rows, ksl[h]]], axis=1),
                                         jnp.concatenate([kdt_ref[c, ksl[h], :], zeros_k], axis=1)], axis=0),
                        jnp.concatenate([vr_ref[rows, vsl[h]], state_b[h]], axis=0), preferred_element_type=F32)
                for h in range(RET_HEADS)]
        ret = [bh[:BLOCK] for bh in both]
        upd = [bh[BLOCK:] for bh in both]
        return outs, ret, upd

    def finish(c, stage2):
        outs, ret, _ = stage2
        rows = rows_of(c)
        ot = jnp.concatenate(outs, axis=0)
        for g in range(ATT_GROUP):
            att_ref[ws, rows, g * LANES:(g + 1) * LANES] = ot[:, g * BLOCK:(g + 1) * BLOCK].T.astype(BF16)
        for h in range(RET_HEADS):
            out = ret[h]
            mu = jnp.mean(out, axis=-1, keepdims=True)
            dev = out - mu
            var = jnp.mean(dev * dev, axis=-1, keepdims=True)
            yn = dev * lax.rsqrt(var + GN_EPS) * gn_ref[:, vsl[h]]
            ret_ref[ws, rows, vsl[h]] = yn.astype(BF16) * sg_ref[rows, vsl[h]]

    def chunk_tail(c, stage1, state):
        stage2 = second_matmuls(c, softmax_and_decay(c, stage1, state))
        finish(c, stage2)
        return [state[h] * cdec_ref[h] + stage2[2][h] for h in range(RET_HEADS)]

    half = D_MODEL // 2
    cols = [slice(0, half), slice(half, D_MODEL)]

    assert nblk == 4
    state = [state_ref[h] for h in range(RET_HEADS)]
    def merged_half(n):
        ya = jnp.dot(att_ref[rs], watt_ref[:, cols[n]], preferred_element_type=F32)
        yr = jnp.dot(ret_ref[rs], wret_ref[:, cols[n]], preferred_element_type=F32)
        merged_ref[:, cols[n]] = ga_ref[:, cols[n]] * ya.astype(BF16) + gb_ref[:, cols[n]] * yr.astype(BF16)

    def out_half(n):
        h_ref[:, cols[n]] = x_ref[:, cols[n]] + jnp.dot(merged_ref[...], wout_ref[:, cols[n]],
                                                        preferred_element_type=F32)

    s0 = first_matmuls(0)
    s1 = first_matmuls(1)
    merged_half(0)
    state = chunk_tail(0, s0, state)
    s2 = first_matmuls(2)
    merged_half(1)
    state = chunk_tail(1, s1, state)
    s3 = first_matmuls(3)
    state = chunk_tail(2, s2, state)
    out_half(0)
    state = chunk_tail(3, s3, state)
    out_half(1)
    for h in range(RET_HEADS):
        state_ref[h] = state[h]

    kext_ref[:BLOCK, :] = kext_ref[nblk * BLOCK:, :]
    vext_ref[0] = vext_ref[nblk]


def _mix_call(proj_out, x, cdec, sinks, gn_gain, tril, w_att_up, w_ret_up, w_out):
    b, s, d = x.shape
    rows = MIX_ROWS
    nblk = rows // BLOCK
    tps = s // rows
    ntiles = b * tps
    grid = (ntiles + 1,)

    chunk_tile = lambda t: jnp.minimum(t, ntiles - 1)
    proj_tile = lambda t: jnp.maximum(t - 1, 0)

    def tok(width, tile):
        return pl.BlockSpec((None, rows, width), lambda t: (tile(t) // tps, tile(t) % tps, 0))

    def tok_t(feat):
        return pl.BlockSpec((None, nblk, feat, BLOCK), lambda t: (chunk_tile(t) // tps, chunk_tile(t) % tps, 0, 0))

    def const(shape):
        return pl.BlockSpec(shape, lambda t: (0,) * len(shape), pipeline_mode=pl.Buffered(1))

    smem = pl.BlockSpec(memory_space=pltpu.SMEM)
    ctok = lambda width: tok(width, chunk_tile)
    return pl.pallas_call(
        functools.partial(_mix_kernel, tiles_per_seq=tps, ntiles=ntiles),
        grid=grid,
        in_specs=[smem,
                  tok_t(ATT_Q_W), ctok(ATT_KV_W), tok_t(ATT_KV_W), ctok(RET_QK_W),
                  tok_t(RET_QK_W), ctok(RET_V_W), ctok(RET_V_W),
                  tok(D_MODEL, proj_tile), tok(D_MODEL, proj_tile), tok(d, proj_tile),
                  const((ATT_KV_HEADS, 1, ATT_GROUP * BLOCK)), const((1, RET_V_W)),
                  const((RET_HEADS, RET_CHUNK, RET_CHUNK)),
                  const((ATT_Q_W, d)), const((RET_V_W, d)), const((d, d))],
        out_specs=tok(d, proj_tile),
        out_shape=jax.ShapeDtypeStruct((b, s, d), F32),
        scratch_shapes=[
            pltpu.VMEM((BLOCK + rows, ATT_KV_W), BF16),
            pltpu.VMEM((1 + nblk, ATT_KV_W, BLOCK), BF16),
            pltpu.VMEM((RET_HEADS, RET_KEY_DIM, RET_VAL_DIM), F32),
            pltpu.VMEM((2, rows, ATT_Q_W), BF16),
            pltpu.VMEM((2, rows, RET_V_W), BF16),
            pltpu.VMEM((rows, d), BF16),
        ],
        compiler_params=pltpu.CompilerParams(
            dimension_semantics=("arbitrary",), vmem_limit_bytes=V7X_VMEM_LIMIT_BYTES),
        name="mix",
    )(cdec, *proj_out, x, sinks, gn_gain, tril, w_att_up, w_ret_up, w_out)


def _ffn_kernel(h_ref, gain_ref, w1_ref, w2_ref, fgain_ref, o_ref, *, final_norm):
    for s in range(h_ref.shape[0] // FFN_SUB_ROWS):
        rs = slice(s * FFN_SUB_ROWS, (s + 1) * FFN_SUB_ROWS)
        h = h_ref[rs, :]
        xb = _rms_scale(h, gain_ref[...]).astype(BF16)
        acc = jnp.zeros(h.shape, F32)
        for c in range(D_FF // FFN_COLS):
            sl = slice(c * FFN_COLS, (c + 1) * FFN_COLS)
            a = jnp.maximum(jnp.dot(xb, w1_ref[:, sl], preferred_element_type=F32), 0.0)
            acc = acc + jnp.dot((a * a).astype(BF16), w2_ref[sl, :], preferred_element_type=F32)
        y = h + acc
        if final_norm:
            y = _rms_scale(y, fgain_ref[...])
        o_ref[rs, :] = y


def _ffn_call(h, gain, w1, w2, fgain, final_norm):
    b, s, d = h.shape
    rows = FFN_ROWS
    grid = (b, s // rows)

    def const(shape):
        return pl.BlockSpec(shape, lambda i, j: (0,) * len(shape), pipeline_mode=pl.Buffered(1))

    tok = pl.BlockSpec((None, rows, d), lambda i, j: (i, j, 0))
    return pl.pallas_call(
        functools.partial(_ffn_kernel, final_norm=final_norm),
        grid=grid,
        in_specs=[tok, const((1, d)), const((d, D_FF)), const((D_FF, d)), const((1, d))],
        out_specs=tok,
        out_shape=jax.ShapeDtypeStruct((b, s, d), F32),
        compiler_params=pltpu.CompilerParams(
            dimension_semantics=("arbitrary", "arbitrary"), vmem_limit_bytes=V7X_VMEM_LIMIT_BYTES),
        name="ffn",
    )(h, gain, w1, w2, fgain)


def _decay_terms():
    h, c = RET_HEADS, RET_CHUNK
    log_gamma = jnp.log1p(-jnp.exp2(-5.0 - jnp.arange(h, dtype=F32)))
    idx = jnp.arange(c, dtype=F32)
    diff = idx[:, None] - idx[None, :]
    tril = jnp.where(diff >= 0, jnp.exp(-c * log_gamma)[:, None, None], 0.0)
    q_decay = jnp.exp((idx + 1.0)[None, :] * log_gamma[:, None])
    k_decay = jnp.exp((c - 1.0 - idx)[None, :] * log_gamma[:, None])
    chunk_decay = jnp.exp(c * log_gamma)
    return tril, q_decay, k_decay, chunk_decay


def _position_tables(seq_len, q_decay, k_decay):
    pos = jnp.arange(seq_len, dtype=F32)

    def cos_sin(dim, theta):
        inv_freq = theta ** (-jnp.arange(0, dim, 2, dtype=F32) / dim)
        ang = pos[:, None] * inv_freq[None, :]
        return jnp.cos(ang), jnp.sin(ang)

    cos, sin = cos_sin(ROPE_DIM, ROPE_THETA)
    pad = ATT_HEAD_DIM - ROPE_DIM
    zeros_half = jnp.zeros_like(sin)
    one_head = lambda parts: jnp.tile(jnp.concatenate(parts, axis=-1), (1, LANES // ATT_HEAD_DIM))
    ca = one_head([cos, cos, jnp.ones((seq_len, pad), F32)])
    s1 = one_head([-sin, zeros_half, jnp.zeros((seq_len, pad), F32)])
    s2 = one_head([zeros_half, sin, jnp.zeros((seq_len, pad), F32)])
    cos_r, sin_r = cos_sin(RET_KEY_DIM, RET_ROT_BASE)
    cr = jnp.concatenate([cos_r, cos_r], axis=-1)
    sr = jnp.concatenate([-sin_r, sin_r], axis=-1)

    def per_row(decay):
        t = jnp.repeat(decay.T, RET_KEY_DIM, axis=1)
        return jnp.tile(t, (PROJ_ROWS // RET_CHUNK, 1))

    return cos.T, sin.T, ca, s1, s2, cr, sr, per_row(q_decay), per_row(k_decay)


def kernel(x, norm_mix_gain, w_in, b_gates, attn_sinks, ret_gn_gain, w_att_up, w_ret_up, w_out,
           norm_mlp_gain, w_ff1, w_ff2, norm_final_gain):
    b, s, d = x.shape
    depth = w_in.shape[0]
    assert d == D_MODEL and s % MIX_ROWS == 0 and s % PROJ_ROWS == 0 and s % FFN_ROWS == 0
    tril, q_decay, k_decay, cdec = _decay_terms()
    tables = _position_tables(s, q_decay, k_decay)
    row = lambda v: v.reshape(1, -1).astype(F32)
    att_perm = np.arange(ATT_Q_W).reshape(ATT_KV_HEADS, ATT_GROUP, ATT_HEAD_DIM).transpose(1, 0, 2).reshape(-1)
    h = x
    for l in range(depth):
        w = (w_in[l] * norm_mix_gain[l].astype(F32)[:, None]).astype(BF16)
        proj_out = _proj_call(h, w, row(b_gates[l]), tables)
        sinks = jnp.repeat((attn_sinks[l].astype(F32) * LOG2E).reshape(ATT_KV_HEADS, 1, ATT_GROUP), BLOCK, axis=-1)
        h = _mix_call(proj_out, h, cdec, sinks, row(ret_gn_gain[l]), tril,
                      w_att_up[l][att_perm].astype(BF16), w_ret_up[l].astype(BF16), w_out[l].astype(BF16))
        h = _ffn_call(h, row(norm_mlp_gain[l]), w_ff1[l].astype(BF16), w_ff2[l].astype(BF16),
                      row(norm_final_gain), final_norm=(l == depth - 1))
    return h
```

```python
import functools

import jax
import jax.numpy as jnp
import numpy as np
from jax import lax
from jax.experimental import pallas as pl
from jax.experimental.pallas import tpu as pltpu

D_MODEL = 1024
ATT_HEADS = 8
ATT_KV_HEADS = 2
ATT_GROUP = ATT_HEADS // ATT_KV_HEADS
ATT_HEAD_DIM = 64
WINDOW = 128
ATT_BLOCK = 128
ROPE_DIM = ATT_HEAD_DIM // 4
ROPE_HALF = ROPE_DIM // 2
ROPE_THETA = 500000.0
RET_HEADS = 4
RET_KEY_DIM = 128
RET_VAL_DIM = 256
RET_CHUNK = 128
RET_ROT_BASE = 10000.0
D_FF = 4 * D_MODEL
NORM_EPS = 1e-6
GN_EPS = 1e-6
NEG_INF = -1e30
LOG2E = 1.4426950408889634

ATT_Q_W = ATT_HEADS * ATT_HEAD_DIM
ATT_KV_W = ATT_KV_HEADS * ATT_HEAD_DIM
RET_QK_W = RET_HEADS * RET_KEY_DIM
RET_V_W = RET_HEADS * RET_VAL_DIM
IN_SPLITS = (ATT_Q_W, ATT_KV_W, ATT_KV_W, RET_QK_W, RET_QK_W, RET_V_W, RET_V_W, D_MODEL, D_MODEL)
IN_WIDTH = sum(IN_SPLITS)
(OFF_QA, OFF_KA, OFF_VA, OFF_QR, OFF_KR, OFF_VR, OFF_GR, OFF_GA, OFF_GB) = (
    int(v) for v in np.concatenate([[0], np.cumsum(IN_SPLITS)[:-1]]))

LANES = 128
SUBLANES = 8
V7X_VMEM_LIMIT_BYTES = 56 * 1024 * 1024

BLOCK = 128
PROJ_ROWS = 512
PROJ_COLS = 512
MIX_ROWS = 512
FFN_ROWS = 1024
FFN_SUB_ROWS = 512
FFN_COLS = 512

BF16 = jnp.bfloat16
F32 = jnp.float32

assert WINDOW == ATT_BLOCK == RET_CHUNK == BLOCK == LANES
assert RET_KEY_DIM == LANES and ATT_KV_W == LANES and ROPE_HALF == SUBLANES
assert PROJ_ROWS % BLOCK == 0 and MIX_ROWS % BLOCK == 0


def _sigmoid(x):
    return 0.5 * jnp.tanh(0.5 * x) + 0.5


def _silu(x):
    return x * _sigmoid(x)


def _inv_rms(x):
    return lax.rsqrt(jnp.mean(x * x, axis=-1, keepdims=True) + NORM_EPS)


def _rms_scale(x, gain):
    return (x * _inv_rms(x)) * gain


def _proj_kernel(x_ref, w_ref, bg_ref,
                 cat_ref, sat_ref, ca_ref, s1_ref, s2_ref, cr_ref, sr_ref, qdec_ref, kdec_ref,
                 qa_ref, kat_ref, va_ref, qdr_ref, kdt_ref, vr_ref, sg_ref, ga_ref, gb_ref):
    rows = x_ref.shape[0]
    nblk = rows // BLOCK
    x = x_ref[...]
    xb = x.astype(BF16)
    inv = _inv_rms(x)

    def proj(off, width):
        return jnp.dot(xb, w_ref[:, off:off + width], preferred_element_type=F32) * inv

    def store_blocks(ref, feat, val):
        for c in range(nblk):
            ref[c, feat, :] = val[:, c * BLOCK:(c + 1) * BLOCK].astype(BF16)

    def rope_lanes(z):
        return (z * ca_ref[...] + pltpu.roll(z, LANES - ROPE_HALF, 1) * s1_ref[...]
                + pltpu.roll(z, ROPE_HALF, 1) * s2_ref[...])

    z = proj(OFF_QA, ATT_Q_W)
    for g in range(ATT_Q_W // LANES):
        sl = slice(g * LANES, (g + 1) * LANES)
        qa_ref[:, sl] = (rope_lanes(z[:, sl]) * (ATT_HEAD_DIM ** -0.5 * LOG2E)).astype(BF16)
    zkv = proj(OFF_KA, 2 * ATT_KV_W)
    va_ref[...] = zkv[:, ATT_KV_W:].astype(BF16)
    zt = zkv[:, :ATT_KV_W].T
    cat, sat = cat_ref[...], sat_ref[...]
    for hd in range(ATT_KV_HEADS):
        b0 = hd * ATT_HEAD_DIM
        x1, x2 = zt[b0:b0 + ROPE_HALF], zt[b0 + ROPE_HALF:b0 + ROPE_DIM]
        head = jnp.concatenate([x1 * cat - x2 * sat, x2 * cat + x1 * sat, zt[b0 + ROPE_DIM:b0 + ATT_HEAD_DIM]], axis=0)
        store_blocks(kat_ref, slice(b0, b0 + ATT_HEAD_DIM), head)

    cr, sr = cr_ref[...], sr_ref[...]

    def rope_ret(zh):
        return zh * cr + pltpu.roll(zh, RET_KEY_DIM // 2, 1) * sr

    z = proj(OFF_QR, RET_QK_W)
    for h in range(RET_HEADS):
        sl = slice(h * RET_KEY_DIM, (h + 1) * RET_KEY_DIM)
        qdr_ref[:, sl] = (rope_ret(z[:, sl]) * qdec_ref[:, sl]).astype(BF16)
    z = proj(OFF_KR, RET_QK_W)
    for h in range(RET_HEADS):
        sl = slice(h * RET_KEY_DIM, (h + 1) * RET_KEY_DIM)
        k = rope_ret(z[:, sl]) * (RET_KEY_DIM ** -0.5)
        store_blocks(kdt_ref, sl, (k * kdec_ref[:, sl]).T)
    for c in range(RET_V_W // PROJ_COLS):
        sl = slice(c * PROJ_COLS, (c + 1) * PROJ_COLS)
        sg_ref[:, sl] = _silu(proj(OFF_GR + c * PROJ_COLS, PROJ_COLS)).astype(BF16)
    for c in range(D_MODEL // PROJ_COLS):
        sl = slice(c * PROJ_COLS, (c + 1) * PROJ_COLS)
        g = proj(OFF_GA + c * PROJ_COLS, PROJ_COLS) + bg_ref[:, sl]
        ga_ref[:, sl] = _sigmoid(g).astype(BF16)
    for c in range(D_MODEL // PROJ_COLS):
        sl = slice(c * PROJ_COLS, (c + 1) * PROJ_COLS)
        g = proj(OFF_GB + c * PROJ_COLS, PROJ_COLS) + bg_ref[:, D_MODEL + c * PROJ_COLS:D_MODEL + (c + 1) * PROJ_COLS]
        gb_ref[:, sl] = _sigmoid(g).astype(BF16)
    for c in range(RET_V_W // PROJ_COLS):
        sl = slice(c * PROJ_COLS, (c + 1) * PROJ_COLS)
        vr_ref[:, sl] = proj(OFF_VR + c * PROJ_COLS, PROJ_COLS).astype(BF16)


def _proj_call(x, w_in, b_gates, tables):
    b, s, d = x.shape
    rows = PROJ_ROWS
    nblk = rows // BLOCK
    grid = (s // rows, b)

    def tok(width):
        return pl.BlockSpec((None, rows, width), lambda j, i: (i, j, 0))

    def tok_t(feat):
        return pl.BlockSpec((None, nblk, feat, BLOCK), lambda j, i: (i, j, 0, 0))

    def const(shape):
        return pl.BlockSpec(shape, lambda j, i: (0,) * len(shape), pipeline_mode=pl.Buffered(1))

    pos_t = pl.BlockSpec((ROPE_HALF, rows), lambda j, i: (0, j))
    pos = pl.BlockSpec((rows, LANES), lambda j, i: (j, 0))
    tok_shape = lambda w: jax.ShapeDtypeStruct((b, s, w), BF16)
    tok_t_shape = lambda f: jax.ShapeDtypeStruct((b, s // BLOCK, f, BLOCK), BF16)
    return pl.pallas_call(
        _proj_kernel,
        grid=grid,
        in_specs=[tok(d), const((d, IN_WIDTH)),
                  const((1, 2 * D_MODEL)), pos_t, pos_t, pos, pos, pos, pos, pos,
                  const((rows, RET_QK_W)), const((rows, RET_QK_W))],
        out_specs=[tok(ATT_Q_W), tok_t(ATT_KV_W), tok(ATT_KV_W), tok(RET_QK_W),
                   tok_t(RET_QK_W), tok(RET_V_W), tok(RET_V_W), tok(D_MODEL), tok(D_MODEL)],
        out_shape=[tok_shape(ATT_Q_W), tok_t_shape(ATT_KV_W), tok_shape(ATT_KV_W), tok_shape(RET_QK_W),
                   tok_t_shape(RET_QK_W), tok_shape(RET_V_W),
                   tok_shape(RET_V_W), tok_shape(D_MODEL), tok_shape(D_MODEL)],
        compiler_params=pltpu.CompilerParams(
            dimension_semantics=("arbitrary", "arbitrary"), vmem_limit_bytes=V7X_VMEM_LIMIT_BYTES),
        name="proj",
    )(x, w_in, b_gates, *tables)


def _mix_kernel(cdec_ref, sinks_ref,
                qa_ref, kat_ref, va_ref, qdr_ref, kdt_ref, vr_ref, sg_ref, ga_ref, gb_ref, x_ref,
                gn_ref, tril_ref, watt_ref, wret_ref, wout_ref,
                h_ref,
                kext_ref, vext_ref, state_ref, att_ref, ret_ref, merged_ref, *, tiles_per_seq, ntiles):
    t = pl.program_id(0)
    ws = t & 1
    rs = 1 - ws
    nblk = x_ref.shape[0] // BLOCK
    first = (jnp.minimum(t, ntiles - 1) % tiles_per_seq) == 0

    @pl.when(t == 0)
    def _():
        att_ref[1] = jnp.zeros(att_ref.shape[1:], BF16)
        ret_ref[1] = jnp.zeros(ret_ref.shape[1:], BF16)

    @pl.when(first)
    def _():
        kext_ref[0] = jnp.zeros((ATT_KV_W, BLOCK), BF16)
        vext_ref[:BLOCK, :] = jnp.zeros((BLOCK, ATT_KV_W), BF16)
        state_ref[...] = jnp.zeros_like(state_ref)

    kext_ref[1:] = kat_ref[...]
    vext_ref[BLOCK:, :] = va_ref[...]

    from_prev = (lax.broadcasted_iota(jnp.int32, (BLOCK, BLOCK), 1)
                 > lax.broadcasted_iota(jnp.int32, (BLOCK, BLOCK), 0))
    low_half = lax.broadcasted_iota(jnp.int32, (ATT_GROUP * BLOCK, LANES), 1) < ATT_HEAD_DIM
    zeros_kt = jnp.zeros((ATT_HEAD_DIM, 2 * BLOCK), BF16)

    ksl = [slice(h * RET_KEY_DIM, (h + 1) * RET_KEY_DIM) for h in range(RET_HEADS)]
    vsl = [slice(h * RET_VAL_DIM, (h + 1) * RET_VAL_DIM) for h in range(RET_HEADS)]
    rows_of = lambda c: pl.ds(c * BLOCK, BLOCK)

    def first_matmuls(c):
        rows = rows_of(c)
        q4 = jnp.concatenate([qa_ref[rows, g * LANES:(g + 1) * LANES] for g in range(ATT_GROUP)], axis=0)
        kt = jnp.concatenate([kext_ref[c], kext_ref[c + 1]], axis=1)
        scores = []
        for kv in range(ATT_KV_HEADS):
            kh = kt[kv * ATT_HEAD_DIM:(kv + 1) * ATT_HEAD_DIM]
            rhs = jnp.concatenate([kh, zeros_kt] if kv == 0 else [zeros_kt, kh], axis=0)
            scores.append(jnp.dot(q4, rhs, preferred_element_type=F32))
        zeros_k = jnp.zeros((RET_KEY_DIM, BLOCK), BF16)
        att = []
        for h in range(0, RET_HEADS, 2):
            pair = slice(h * RET_KEY_DIM, (h + 2) * RET_KEY_DIM)
            kdiag = jnp.concatenate([jnp.concatenate([kdt_ref[c, ksl[h], :], zeros_k], axis=1),
                                     jnp.concatenate([zeros_k, kdt_ref[c, ksl[h + 1], :]], axis=1)], axis=0)
            both = jnp.dot(qdr_ref[rows, pair], kdiag, preferred_element_type=F32)
            att += [both[:, :BLOCK], both[:, BLOCK:]]
        return scores, att

    def softmax_and_decay(c, stage1, state):
        scores, att = stage1
        probs, inv_den = [], []
        for kv in range(ATT_KV_HEADS):
            sc = scores[kv]
            eu, inv = [], []
            for g in range(ATT_GROUP):
                blk = sc[g * BLOCK:(g + 1) * BLOCK]
                prev = blk[:, :BLOCK]
                if c == 0:
                    prev = prev + jnp.where(first, NEG_INF, 0.0).astype(F32)
                f = jnp.where(from_prev, prev, blk[:, BLOCK:])
                sink = sinks_ref[kv * ATT_GROUP + g]
                m = jnp.maximum(jnp.max(f, axis=-1, keepdims=True), sink)
                e = jnp.exp2(f - m)
                inv.append(1.0 / (jnp.sum(e, axis=-1, keepdims=True) + jnp.exp2(sink - m)))
                eu.append(jnp.concatenate([jnp.where(from_prev, e, 0.0), jnp.where(from_prev, 0.0, e)], axis=1))
            probs.append(jnp.concatenate(eu, axis=0).astype(BF16))
            inv_den.append(jnp.concatenate(inv, axis=0))
        att_b = [(att[h] * tril_ref[h]).astype(BF16) for h in range(RET_HEADS)]
        state_b = [state[h].astype(BF16) for h in range(RET_HEADS)]
        return probs, inv_den, att_b, state_b

    def second_matmuls(c, ops):
        probs, inv_den, att_b, state_b = ops
        rows = rows_of(c)
        vc = vext_ref[pl.ds(c * BLOCK, 2 * BLOCK), :]
        per_kv = [jnp.dot(probs[kv], vc, preferred_element_type=F32) * inv_den[kv] for kv in range(ATT_KV_HEADS)]
        outs = jnp.where(low_half, per_kv[0], per_kv[1])
        zeros_k = jnp.zeros((RET_KEY_DIM, RET_KEY_DIM), BF16)
        both = [jnp.dot(jnp.concatenate([jnp.concatenate([att_b[h], qdr_ref[rows, ksl[h]]], axis=1),
                                         jnp.concatenate([kdt_ref[c, ksl[h], :], zeros_k], axis=1)], axis=0),
                        jnp.concatenate([vr_ref[rows, vsl[h]], state_b[h]], axis=0), preferred_element_type=F32)
                for h in range(RET_HEADS)]
        ret = [bh[:BLOCK] for bh in both]
        upd = [bh[BLOCK:] for bh in both]
        return outs, ret, upd

    def finish(c, stage2):
        outs, ret, _ = stage2
        rows = rows_of(c)
        for g in range(ATT_GROUP):
            att_ref[ws, rows, g * LANES:(g + 1) * LANES] = outs[g * BLOCK:(g + 1) * BLOCK].astype(BF16)
        for h in range(RET_HEADS):
            out = ret[h]
            mu = jnp.mean(out, axis=-1, keepdims=True)
            dev = out - mu
            var = jnp.mean(dev * dev, axis=-1, keepdims=True)
            yn = dev * lax.rsqrt(var + GN_EPS) * gn_ref[:, vsl[h]]
            ret_ref[ws, rows, vsl[h]] = yn.astype(BF16) * sg_ref[rows, vsl[h]]

    def chunk_tail(c, stage1, state):
        stage2 = second_matmuls(c, softmax_and_decay(c, stage1, state))
        finish(c, stage2)
        return [state[h] * cdec_ref[h] + stage2[2][h] for h in range(RET_HEADS)]

    half = D_MODEL // 2
    cols = [slice(0, half), slice(half, D_MODEL)]

    assert nblk == 4
    state = [state_ref[h] for h in range(RET_HEADS)]
    def merged_half(n):
        ya = jnp.dot(att_ref[rs], watt_ref[:, cols[n]], preferred_element_type=F32)
        yr = jnp.dot(ret_ref[rs], wret_ref[:, cols[n]], preferred_element_type=F32)
        merged_ref[:, cols[n]] = ga_ref[:, cols[n]] * ya.astype(BF16) + gb_ref[:, cols[n]] * yr.astype(BF16)

    def out_half(n):
        h_ref[:, cols[n]] = x_ref[:, cols[n]] + jnp.dot(merged_ref[...], wout_ref[:, cols[n]],
                                                        preferred_element_type=F32)

    s0 = first_matmuls(0)
    s1 = first_matmuls(1)
    merged_half(0)
    state = chunk_tail(0, s0, state)
    s2 = first_matmuls(2)
    merged_half(1)
    state = chunk_tail(1, s1, state)
    s3 = first_matmuls(3)
    state = chunk_tail(2, s2, state)
    out_half(0)
    state = chunk_tail(3, s3, state)
    out_half(1)
    for h in range(RET_HEADS):
        state_ref[h] = state[h]

    kext_ref[0] = kext_ref[nblk]
    vext_ref[:BLOCK, :] = vext_ref[nblk * BLOCK:, :]


def _mix_call(proj_out, x, cdec, sinks, gn_gain, tril, w_att_up, w_ret_up, w_out):
    b, s, d = x.shape
    rows = MIX_ROWS
    nblk = rows // BLOCK
    tps = s // rows
    ntiles = b * tps
    grid = (ntiles + 1,)

    chunk_tile = lambda t: jnp.minimum(t, ntiles - 1)
    proj_tile = lambda t: jnp.maximum(t - 1, 0)

    def tok(width, tile):
        return pl.BlockSpec((None, rows, width), lambda t: (tile(t) // tps, tile(t) % tps, 0))

    def tok_t(feat):
        return pl.BlockSpec((None, nblk, feat, BLOCK), lambda t: (chunk_tile(t) // tps, chunk_tile(t) % tps, 0, 0))

    def const(shape):
        return pl.BlockSpec(shape, lambda t: (0,) * len(shape), pipeline_mode=pl.Buffered(1))

    smem = pl.BlockSpec(memory_space=pltpu.SMEM)
    ctok = lambda width: tok(width, chunk_tile)
    return pl.pallas_call(
        functools.partial(_mix_kernel, tiles_per_seq=tps, ntiles=ntiles),
        grid=grid,
        in_specs=[smem, smem,
                  ctok(ATT_Q_W), tok_t(ATT_KV_W), ctok(ATT_KV_W), ctok(RET_QK_W),
                  tok_t(RET_QK_W), ctok(RET_V_W), ctok(RET_V_W),
                  tok(D_MODEL, proj_tile), tok(D_MODEL, proj_tile), tok(d, proj_tile),
                  const((1, RET_V_W)),
                  const((RET_HEADS, RET_CHUNK, RET_CHUNK)),
                  const((ATT_Q_W, d)), const((RET_V_W, d)), const((d, d))],
        out_specs=tok(d, proj_tile),
        out_shape=jax.ShapeDtypeStruct((b, s, d), F32),
        scratch_shapes=[
            pltpu.VMEM((1 + nblk, ATT_KV_W, BLOCK), BF16),
            pltpu.VMEM((BLOCK + rows, ATT_KV_W), BF16),
            pltpu.VMEM((RET_HEADS, RET_KEY_DIM, RET_VAL_DIM), F32),
            pltpu.VMEM((2, rows, ATT_Q_W), BF16),
            pltpu.VMEM((2, rows, RET_V_W), BF16),
            pltpu.VMEM((rows, d), BF16),
        ],
        compiler_params=pltpu.CompilerParams(
            dimension_semantics=("arbitrary",), vmem_limit_bytes=V7X_VMEM_LIMIT_BYTES),
        name="mix",
    )(cdec, sinks, *proj_out, x, gn_gain, tril, w_att_up, w_ret_up, w_out)


def _ffn_kernel(h_ref, gain_ref, w1_ref, w2_ref, fgain_ref, o_ref, *, final_norm):
    for s in range(h_ref.shape[0] // FFN_SUB_ROWS):
        rs = slice(s * FFN_SUB_ROWS, (s + 1) * FFN_SUB_ROWS)
        h = h_ref[rs, :]
        xb = _rms_scale(h, gain_ref[...]).astype(BF16)
        acc = jnp.zeros(h.shape, F32)
        for c in range(D_FF // FFN_COLS):
            sl = slice(c * FFN_COLS, (c + 1) * FFN_COLS)
            a = jnp.maximum(jnp.dot(xb, w1_ref[:, sl], preferred_element_type=F32), 0.0)
            acc = acc + jnp.dot((a * a).astype(BF16), w2_ref[sl, :], preferred_element_type=F32)
        y = h + acc
        if final_norm:
            y = _rms_scale(y, fgain_ref[...])
        o_ref[rs, :] = y


def _ffn_call(h, gain, w1, w2, fgain, final_norm):
    b, s, d = h.shape
    rows = FFN_ROWS
    grid = (b, s // rows)

    def const(shape):
        return pl.BlockSpec(shape, lambda i, j: (0,) * len(shape), pipeline_mode=pl.Buffered(1))

    tok = pl.BlockSpec((None, rows, d), lambda i, j: (i, j, 0))
    return pl.pallas_call(
        functools.partial(_ffn_kernel, final_norm=final_norm),
        grid=grid,
        in_specs=[tok, const((1, d)), const((d, D_FF)), const((D_FF, d)), const((1, d))],
        out_specs=tok,
        out_shape=jax.ShapeDtypeStruct((b, s, d), F32),
        compiler_params=pltpu.CompilerParams(
            dimension_semantics=("arbitrary", "arbitrary"), vmem_limit_bytes=V7X_VMEM_LIMIT_BYTES),
        name="ffn",
    )(h, gain, w1, w2, fgain)


def _decay_terms():
    h, c = RET_HEADS, RET_CHUNK
    log_gamma = jnp.log1p(-jnp.exp2(-5.0 - jnp.arange(h, dtype=F32)))
    idx = jnp.arange(c, dtype=F32)
    diff = idx[:, None] - idx[None, :]
    tril = jnp.where(diff >= 0, jnp.exp(-c * log_gamma)[:, None, None], 0.0)
    q_decay = jnp.exp((idx + 1.0)[None, :] * log_gamma[:, None])
    k_decay = jnp.exp((c - 1.0 - idx)[None, :] * log_gamma[:, None])
    chunk_decay = jnp.exp(c * log_gamma)
    return tril, q_decay, k_decay, chunk_decay


def _position_tables(seq_len, q_decay, k_decay):
    pos = jnp.arange(seq_len, dtype=F32)

    def cos_sin(dim, theta):
        inv_freq = theta ** (-jnp.arange(0, dim, 2, dtype=F32) / dim)
        ang = pos[:, None] * inv_freq[None, :]
        return jnp.cos(ang), jnp.sin(ang)

    cos, sin = cos_sin(ROPE_DIM, ROPE_THETA)
    pad = ATT_HEAD_DIM - ROPE_DIM
    zeros_half = jnp.zeros_like(sin)
    one_head = lambda parts: jnp.tile(jnp.concatenate(parts, axis=-1), (1, LANES // ATT_HEAD_DIM))
    ca = one_head([cos, cos, jnp.ones((seq_len, pad), F32)])
    s1 = one_head([-sin, zeros_half, jnp.zeros((seq_len, pad), F32)])
    s2 = one_head([zeros_half, sin, jnp.zeros((seq_len, pad), F32)])
    cos_r, sin_r = cos_sin(RET_KEY_DIM, RET_ROT_BASE)
    cr = jnp.concatenate([cos_r, cos_r], axis=-1)
    sr = jnp.concatenate([-sin_r, sin_r], axis=-1)

    def per_row(decay):
        t = jnp.repeat(decay.T, RET_KEY_DIM, axis=1)
        return jnp.tile(t, (PROJ_ROWS // RET_CHUNK, 1))

    return cos.T, sin.T, ca, s1, s2, cr, sr, per_row(q_decay), per_row(k_decay)


def kernel(x, norm_mix_gain, w_in, b_gates, attn_sinks, ret_gn_gain, w_att_up, w_ret_up, w_out,
           norm_mlp_gain, w_ff1, w_ff2, norm_final_gain):
    b, s, d = x.shape
    depth = w_in.shape[0]
    assert d == D_MODEL and s % MIX_ROWS == 0 and s % PROJ_ROWS == 0 and s % FFN_ROWS == 0
    tril, q_decay, k_decay, cdec = _decay_terms()
    tables = _position_tables(s, q_decay, k_decay)
    row = lambda v: v.reshape(1, -1).astype(F32)
    att_perm = np.arange(ATT_Q_W).reshape(ATT_KV_HEADS, ATT_GROUP, ATT_HEAD_DIM).transpose(1, 0, 2).reshape(-1)
    h = x
    for l in range(depth):
        w = (w_in[l] * norm_mix_gain[l].astype(F32)[:, None]).astype(BF16)
        w = jnp.concatenate([w[:, OFF_QA:OFF_QA + ATT_Q_W][:, att_perm], w[:, OFF_QA + ATT_Q_W:]], axis=1)
        proj_out = _proj_call(h, w, row(b_gates[l]), tables)
        sinks = attn_sinks[l].astype(F32) * LOG2E
        h = _mix_call(proj_out, h, cdec, sinks, row(ret_gn_gain[l]), tril,
                      w_att_up[l][att_perm].astype(BF16), w_ret_up[l].astype(BF16), w_out[l].astype(BF16))
        h = _ffn_call(h, row(norm_mlp_gain[l]), w_ff1[l].astype(BF16), w_ff2[l].astype(BF16),
                      row(norm_final_gain), final_norm=(l == depth - 1))
    return h
```

```python
import functools

import jax
import jax.numpy as jnp
import numpy as np
from jax import lax
from jax.experimental import pallas as pl
from jax.experimental.pallas import tpu as pltpu

D_MODEL = 1024
ATT_HEADS = 8
ATT_KV_HEADS = 2
ATT_GROUP = ATT_HEADS // ATT_KV_HEADS
ATT_HEAD_DIM = 64
WINDOW = 128
ATT_BLOCK = 128
ROPE_DIM = ATT_HEAD_DIM // 4
ROPE_HALF = ROPE_DIM // 2
ROPE_THETA = 500000.0
RET_HEADS = 4
RET_KEY_DIM = 128
RET_VAL_DIM = 256
RET_CHUNK = 128
RET_ROT_BASE = 10000.0
D_FF = 4 * D_MODEL
NORM_EPS = 1e-6
GN_EPS = 1e-6
NEG_INF = -1e30
LOG2E = 1.4426950408889634

ATT_Q_W = ATT_HEADS * ATT_HEAD_DIM
ATT_KV_W = ATT_KV_HEADS * ATT_HEAD_DIM
RET_QK_W = RET_HEADS * RET_KEY_DIM
RET_V_W = RET_HEADS * RET_VAL_DIM
IN_SPLITS = (ATT_Q_W, ATT_KV_W, ATT_KV_W, RET_QK_W, RET_QK_W, RET_V_W, RET_V_W, D_MODEL, D_MODEL)
IN_WIDTH = sum(IN_SPLITS)
(OFF_QA, OFF_KA, OFF_VA, OFF_QR, OFF_KR, OFF_VR, OFF_GR, OFF_GA, OFF_GB) = (
    int(v) for v in np.concatenate([[0], np.cumsum(IN_SPLITS)[:-1]]))

LANES = 128
SUBLANES = 8
BF16_SUBLANES = 16
V7X_VMEM_LIMIT_BYTES = 56 * 1024 * 1024

BLOCK = 128
PROJ_ROWS = 512
PROJ_COLS = 512
MIX_ROWS = 512
FFN_ROWS = 1024
FFN_SUB_ROWS = 512
FFN_COLS = 512

BF16 = jnp.bfloat16
F32 = jnp.float32

assert WINDOW == ATT_BLOCK == RET_CHUNK == BLOCK == LANES
assert RET_KEY_DIM == LANES and ATT_KV_W == LANES and ROPE_HALF == SUBLANES
assert PROJ_ROWS % BLOCK == 0 and MIX_ROWS % BLOCK == 0


def _sigmoid(x):
    return 0.5 * jnp.tanh(0.5 * x) + 0.5


def _silu(x):
    return x * _sigmoid(x)


def _inv_rms(x):
    return lax.rsqrt(jnp.mean(x * x, axis=-1, keepdims=True) + NORM_EPS)


def _rms_scale(x, gain):
    return (x * _inv_rms(x)) * gain


def _proj_kernel(x_ref, w_ref, bg_ref,
                 cat_ref, sat_ref, ca_ref, s1_ref, s2_ref, cr_ref, sr_ref, qdec_ref, kdec_ref,
                 watt_ref, wret_ref, wout_ref,
                 qat_ref, ka_ref, vat_ref, qdr_ref, kdt_ref, vr_ref, sg_ref, ga_ref, gb_ref,
                 wattb_ref, wretb_ref, woutb_ref):
    rows = x_ref.shape[0]
    nblk = rows // BLOCK
    wattb_ref[...] = watt_ref[...].astype(BF16)
    wretb_ref[...] = wret_ref[...].astype(BF16)
    woutb_ref[...] = wout_ref[...].astype(BF16)
    x = x_ref[...]
    xb = x.astype(BF16)
    inv = _inv_rms(x)

    def proj(off, width):
        return jnp.dot(xb, w_ref[:, off:off + width], preferred_element_type=F32) * inv

    def store_blocks(ref, feat, val):
        for c in range(nblk):
            ref[c, feat, :] = val[:, c * BLOCK:(c + 1) * BLOCK].astype(BF16)

    zt = proj(OFF_QA, ATT_Q_W).T
    cat, sat = cat_ref[...], sat_ref[...]
    for hd in range(ATT_HEADS):
        b0 = hd * ATT_HEAD_DIM
        x1, x2 = zt[b0:b0 + ROPE_HALF], zt[b0 + ROPE_HALF:b0 + ROPE_DIM]
        head = jnp.concatenate([x1 * cat - x2 * sat, x2 * cat + x1 * sat, zt[b0 + ROPE_DIM:b0 + ATT_HEAD_DIM]], axis=0)
        store_blocks(qat_ref, slice(b0, b0 + ATT_HEAD_DIM), head * (ATT_HEAD_DIM ** -0.5 * LOG2E))
    zkv = proj(OFF_KA, 2 * ATT_KV_W)
    store_blocks(vat_ref, slice(None), zkv[:, ATT_KV_W:].T)
    z = zkv[:, :ATT_KV_W]
    ka_ref[...] = (z * ca_ref[...] + pltpu.roll(z, LANES - ROPE_HALF, 1) * s1_ref[...]
                   + pltpu.roll(z, ROPE_HALF, 1) * s2_ref[...]).astype(BF16)

    cr, sr = cr_ref[...], sr_ref[...]

    def rope_ret(zh):
        return zh * cr + pltpu.roll(zh, RET_KEY_DIM // 2, 1) * sr

    z = proj(OFF_QR, RET_QK_W)
    for h in range(RET_HEADS):
        sl = slice(h * RET_KEY_DIM, (h + 1) * RET_KEY_DIM)
        qdr_ref[:, sl] = (rope_ret(z[:, sl]) * qdec_ref[:, sl]).astype(BF16)
    z = proj(OFF_KR, RET_QK_W)
    for h in range(RET_HEADS):
        sl = slice(h * RET_KEY_DIM, (h + 1) * RET_KEY_DIM)
        k = rope_ret(z[:, sl]) * (RET_KEY_DIM ** -0.5)
        store_blocks(kdt_ref, sl, (k * kdec_ref[:, sl]).T)
    for c in range(RET_V_W // PROJ_COLS):
        sl = slice(c * PROJ_COLS, (c + 1) * PROJ_COLS)
        sg_ref[:, sl] = _silu(proj(OFF_GR + c * PROJ_COLS, PROJ_COLS)).astype(BF16)
    for c in range(D_MODEL // PROJ_COLS):
        sl = slice(c * PROJ_COLS, (c + 1) * PROJ_COLS)
        g = proj(OFF_GA + c * PROJ_COLS, PROJ_COLS) + bg_ref[:, sl]
        ga_ref[:, sl] = _sigmoid(g).astype(BF16)
    for c in range(D_MODEL // PROJ_COLS):
        sl = slice(c * PROJ_COLS, (c + 1) * PROJ_COLS)
        g = proj(OFF_GB + c * PROJ_COLS, PROJ_COLS) + bg_ref[:, D_MODEL + c * PROJ_COLS:D_MODEL + (c + 1) * PROJ_COLS]
        gb_ref[:, sl] = _sigmoid(g).astype(BF16)
    for c in range(RET_V_W // PROJ_COLS):
        sl = slice(c * PROJ_COLS, (c + 1) * PROJ_COLS)
        vr_ref[:, sl] = proj(OFF_VR + c * PROJ_COLS, PROJ_COLS).astype(BF16)


def _slab_rows(nrows, nsteps):
    slab = max(BF16_SUBLANES, nrows // nsteps)
    assert nrows % slab == 0 and slab % BF16_SUBLANES == 0 and nrows // slab <= nsteps
    return slab


def _proj_call(x, w_in, b_gates, tables, w_att_up, w_ret_up, w_out):
    b, s, d = x.shape
    rows = PROJ_ROWS
    nblk = rows // BLOCK
    grid = (s // rows, b)

    def weight_slabs(w, out_block=lambda blk: blk):
        slab = _slab_rows(w.shape[0], grid[0] * grid[1])
        blk = lambda j, i: jnp.minimum(j * b + i, w.shape[0] // slab - 1)
        return (pl.BlockSpec((slab, w.shape[1]), lambda j, i: (blk(j, i), 0)),
                pl.BlockSpec((slab, w.shape[1]), lambda j, i: (out_block(blk(j, i)), 0)),
                jax.ShapeDtypeStruct(w.shape, BF16))

    att_slab = _slab_rows(w_att_up.shape[0], grid[0] * grid[1])
    assert ATT_HEAD_DIM % att_slab == 0
    head_slabs = ATT_HEAD_DIM // att_slab

    def att_out_block(blk):
        head, part = blk // head_slabs, blk % head_slabs
        return ((head % ATT_GROUP) * ATT_KV_HEADS + head // ATT_GROUP) * head_slabs + part

    weights = [weight_slabs(w_att_up, att_out_block), weight_slabs(w_ret_up), weight_slabs(w_out)]

    def tok(width):
        return pl.BlockSpec((None, rows, width), lambda j, i: (i, j, 0))

    def tok_t(feat):
        return pl.BlockSpec((None, nblk, feat, BLOCK), lambda j, i: (i, j, 0, 0))

    def const(shape):
        return pl.BlockSpec(shape, lambda j, i: (0,) * len(shape), pipeline_mode=pl.Buffered(1))

    pos_t = pl.BlockSpec((ROPE_HALF, rows), lambda j, i: (0, j))
    pos = pl.BlockSpec((rows, LANES), lambda j, i: (j, 0))
    tok_shape = lambda w: jax.ShapeDtypeStruct((b, s, w), BF16)
    tok_t_shape = lambda f: jax.ShapeDtypeStruct((b, s // BLOCK, f, BLOCK), BF16)
    return pl.pallas_call(
        _proj_kernel,
        grid=grid,
        in_specs=[tok(d), const((d, IN_WIDTH)),
                  const((1, 2 * D_MODEL)), pos_t, pos_t, pos, pos, pos, pos, pos,
                  const((rows, RET_QK_W)), const((rows, RET_QK_W))] + [w[0] for w in weights],
        out_specs=[tok_t(ATT_Q_W), tok(ATT_KV_W), tok_t(ATT_KV_W), tok(RET_QK_W),
                   tok_t(RET_QK_W), tok(RET_V_W), tok(RET_V_W), tok(D_MODEL), tok(D_MODEL)] + [w[1] for w in weights],
        out_shape=[tok_t_shape(ATT_Q_W), tok_shape(ATT_KV_W), tok_t_shape(ATT_KV_W), tok_shape(RET_QK_W),
                   tok_t_shape(RET_QK_W), tok_shape(RET_V_W),
                   tok_shape(RET_V_W), tok_shape(D_MODEL), tok_shape(D_MODEL)] + [w[2] for w in weights],
        compiler_params=pltpu.CompilerParams(
            dimension_semantics=("arbitrary", "arbitrary"), vmem_limit_bytes=V7X_VMEM_LIMIT_BYTES),
        name="proj",
    )(x, w_in, b_gates, *tables, w_att_up, w_ret_up, w_out)


def _mix_kernel(cdec_ref,
                qat_ref, ka_ref, vat_ref, qdr_ref, kdt_ref, vr_ref, sg_ref, ga_ref, gb_ref, x_ref,
                sink_ref, gn_ref, tril_ref, watt_ref, wret_ref, wout_ref, w1_ref, w2_ref,
                h_ref, w1b_ref, w2b_ref,
                kext_ref, vext_ref, state_ref, att_ref, ret_ref, merged_ref, *, tiles_per_seq, ntiles):
    t = pl.program_id(0)
    w1b_ref[...] = w1_ref[...].astype(BF16)
    w2b_ref[...] = w2_ref[...].astype(BF16)
    ws = t & 1
    rs = 1 - ws
    nblk = x_ref.shape[0] // BLOCK
    first = (jnp.minimum(t, ntiles - 1) % tiles_per_seq) == 0

    @pl.when(t == 0)
    def _():
        att_ref[1] = jnp.zeros(att_ref.shape[1:], BF16)
        ret_ref[1] = jnp.zeros(ret_ref.shape[1:], BF16)

    @pl.when(first)
    def _():
        kext_ref[:BLOCK, :] = jnp.zeros((BLOCK, ATT_KV_W), BF16)
        vext_ref[0] = jnp.zeros((ATT_KV_W, BLOCK), BF16)
        state_ref[...] = jnp.zeros_like(state_ref)

    kext_ref[BLOCK:, :] = ka_ref[...]
    vext_ref[1:] = vat_ref[...]

    from_prev = (lax.broadcasted_iota(jnp.int32, (BLOCK, BLOCK), 0)
                 > lax.broadcasted_iota(jnp.int32, (BLOCK, BLOCK), 1))

    def band_select(a, b):
        pick = lambda v, g: v[:, g * BLOCK:(g + 1) * BLOCK] if hasattr(v, "shape") else v
        return jnp.concatenate([jnp.where(from_prev, pick(a, g), pick(b, g)) for g in range(ATT_GROUP)], axis=1)
    zeros_q = jnp.zeros((ATT_HEAD_DIM, ATT_GROUP * BLOCK), BF16)

    ksl = [slice(h * RET_KEY_DIM, (h + 1) * RET_KEY_DIM) for h in range(RET_HEADS)]
    vsl = [slice(h * RET_VAL_DIM, (h + 1) * RET_VAL_DIM) for h in range(RET_HEADS)]
    rows_of = lambda c: pl.ds(c * BLOCK, BLOCK)

    def first_matmuls(c):
        rows = rows_of(c)
        qt = qat_ref[c]
        kc = kext_ref[pl.ds(c * BLOCK, 2 * BLOCK), :]
        scores = []
        for kv in range(ATT_KV_HEADS):
            qg = jnp.concatenate(
                [qt[(kv * ATT_GROUP + g) * ATT_HEAD_DIM:(kv * ATT_GROUP + g + 1) * ATT_HEAD_DIM] for g in range(ATT_GROUP)],
                axis=1)
            rhs = jnp.concatenate([qg, zeros_q] if kv == 0 else [zeros_q, qg], axis=0)
            scores.append(jnp.dot(kc, rhs, preferred_element_type=F32))
        zeros_k = jnp.zeros((RET_KEY_DIM, BLOCK), BF16)
        att = []
        for h in range(0, RET_HEADS, 2):
            pair = slice(h * RET_KEY_DIM, (h + 2) * RET_KEY_DIM)
            kdiag = jnp.concatenate([jnp.concatenate([kdt_ref[c, ksl[h], :], zeros_k], axis=1),
                                     jnp.concatenate([zeros_k, kdt_ref[c, ksl[h + 1], :]], axis=1)], axis=0)
            both = jnp.dot(qdr_ref[rows, pair], kdiag, preferred_element_type=F32)
            att += [both[:, :BLOCK], both[:, BLOCK:]]
        return scores, att

    def softmax_and_decay(c, stage1, state):
        scores, att = stage1
        probs, inv_den = [], []
        for kv in range(ATT_KV_HEADS):
            sc = scores[kv]
            prev = sc[:BLOCK]
            if c == 0:
                prev = prev + jnp.where(first, NEG_INF, 0.0).astype(F32)
            f = band_select(prev, sc[BLOCK:])
            sink = sink_ref[kv]
            m = jnp.maximum(jnp.max(f, axis=0, keepdims=True), sink)
            e = jnp.exp2(f - m)
            inv_den.append(1.0 / (jnp.sum(e, axis=0, keepdims=True) + jnp.exp2(sink - m)))
            probs.append(jnp.concatenate([band_select(e, 0.0), band_select(0.0, e)], axis=0).astype(BF16))
        att_b = [(att[h] * tril_ref[h]).astype(BF16) for h in range(RET_HEADS)]
        state_b = [state[h].astype(BF16) for h in range(RET_HEADS)]
        return probs, inv_den, att_b, state_b

    def second_matmuls(c, ops):
        probs, inv_den, att_b, state_b = ops
        rows = rows_of(c)
        vt_prev, vt_cur = vext_ref[c], vext_ref[c + 1]
        outs = []
        for kv in range(ATT_KV_HEADS):
            hs = slice(kv * ATT_HEAD_DIM, (kv + 1) * ATT_HEAD_DIM)
            vt = jnp.concatenate([vt_prev[hs], vt_cur[hs]], axis=1)
            outs.append(jnp.dot(vt, probs[kv], preferred_element_type=F32) * inv_den[kv])
        zeros_k = jnp.zeros((RET_KEY_DIM, RET_KEY_DIM), BF16)
        both = [jnp.dot(jnp.concatenate([jnp.concatenate([att_b[h], qdr_ref[rows, ksl[h]]], axis=1),
                                         jnp.concatenate([kdt_ref[c, ksl[h], :], zeros_k], axis=1)], axis=0),
                        jnp.concatenate([vr_ref[rows, vsl[h]], state_b[h]], axis=0), preferred_element_type=F32)
                for h in range(RET_HEADS)]
        ret = [bh[:BLOCK] for bh in both]
        upd = [bh[BLOCK:] for bh in both]
        return outs, ret, upd

    def finish(c, stage2):
        outs, ret, _ = stage2
        rows = rows_of(c)
        ot = jnp.concatenate(outs, axis=0)
        for g in range(ATT_GROUP):
            att_ref[ws, rows, g * LANES:(g + 1) * LANES] = ot[:, g * BLOCK:(g + 1) * BLOCK].T.astype(BF16)
        for h in range(RET_HEADS):
            out = ret[h]
            mu = jnp.mean(out, axis=-1, keepdims=True)
            dev = out - mu
            var = jnp.mean(dev * dev, axis=-1, keepdims=True)
            yn = dev * lax.rsqrt(var + GN_EPS) * gn_ref[:, vsl[h]]
            ret_ref[ws, rows, vsl[h]] = yn.astype(BF16) * sg_ref[rows, vsl[h]]

    def chunk_tail(c, stage1, state):
        stage2 = second_matmuls(c, softmax_and_decay(c, stage1, state))
        finish(c, stage2)
        return [state[h] * cdec_ref[h] + stage2[2][h] for h in range(RET_HEADS)]

    half = D_MODEL // 2
    cols = [slice(0, half), slice(half, D_MODEL)]

    assert nblk == 4
    state = [state_ref[h] for h in range(RET_HEADS)]
    def merged_half(n):
        ya = jnp.dot(att_ref[rs], watt_ref[:, cols[n]], preferred_element_type=F32)
        yr = jnp.dot(ret_ref[rs], wret_ref[:, cols[n]], preferred_element_type=F32)
        merged_ref[:, cols[n]] = ga_ref[:, cols[n]] * ya.astype(BF16) + gb_ref[:, cols[n]] * yr.astype(BF16)

    def out_half(n):
        h_ref[:, cols[n]] = x_ref[:, cols[n]] + jnp.dot(merged_ref[...], wout_ref[:, cols[n]],
                                                        preferred_element_type=F32)

    s0 = first_matmuls(0)
    s1 = first_matmuls(1)
    merged_half(0)
    state = chunk_tail(0, s0, state)
    s2 = first_matmuls(2)
    merged_half(1)
    state = chunk_tail(1, s1, state)
    s3 = first_matmuls(3)
    state = chunk_tail(2, s2, state)
    out_half(0)
    state = chunk_tail(3, s3, state)
    out_half(1)
    for h in range(RET_HEADS):
        state_ref[h] = state[h]

    kext_ref[:BLOCK, :] = kext_ref[nblk * BLOCK:, :]
    vext_ref[0] = vext_ref[nblk]


def _mix_call(proj_out, x, cdec, sinks, gn_gain, tril, w_att_up, w_ret_up, w_out, w_ff1, w_ff2):
    b, s, d = x.shape
    rows = MIX_ROWS
    nblk = rows // BLOCK
    tps = s // rows
    ntiles = b * tps
    grid = (ntiles + 1,)

    def weight_slab(w):
        slab = _slab_rows(w.shape[0], ntiles)
        return pl.BlockSpec((slab, w.shape[1]), lambda t: (jnp.minimum(t, w.shape[0] // slab - 1), 0))

    chunk_tile = lambda t: jnp.minimum(t, ntiles - 1)
    proj_tile = lambda t: jnp.maximum(t - 1, 0)

    def tok(width, tile):
        return pl.BlockSpec((None, rows, width), lambda t: (tile(t) // tps, tile(t) % tps, 0))

    def tok_t(feat):
        return pl.BlockSpec((None, nblk, feat, BLOCK), lambda t: (chunk_tile(t) // tps, chunk_tile(t) % tps, 0, 0))

    def const(shape):
        return pl.BlockSpec(shape, lambda t: (0,) * len(shape), pipeline_mode=pl.Buffered(1))

    smem = pl.BlockSpec(memory_space=pltpu.SMEM)
    ctok = lambda width: tok(width, chunk_tile)
    return pl.pallas_call(
        functools.partial(_mix_kernel, tiles_per_seq=tps, ntiles=ntiles),
        grid=grid,
        in_specs=[smem,
                  tok_t(ATT_Q_W), ctok(ATT_KV_W), tok_t(ATT_KV_W), ctok(RET_QK_W),
                  tok_t(RET_QK_W), ctok(RET_V_W), ctok(RET_V_W),
                  tok(D_MODEL, proj_tile), tok(D_MODEL, proj_tile), tok(d, proj_tile),
                  const((ATT_KV_HEADS, 1, ATT_GROUP * BLOCK)), const((1, RET_V_W)),
                  const((RET_HEADS, RET_CHUNK, RET_CHUNK)),
                  const((ATT_Q_W, d)), const((RET_V_W, d)), const((d, d)),
                  weight_slab(w_ff1), weight_slab(w_ff2)],
        out_specs=[tok(d, proj_tile), weight_slab(w_ff1), weight_slab(w_ff2)],
        out_shape=[jax.ShapeDtypeStruct((b, s, d), F32),
                   jax.ShapeDtypeStruct(w_ff1.shape, BF16), jax.ShapeDtypeStruct(w_ff2.shape, BF16)],
        scratch_shapes=[
            pltpu.VMEM((BLOCK + rows, ATT_KV_W), BF16),
            pltpu.VMEM((1 + nblk, ATT_KV_W, BLOCK), BF16),
            pltpu.VMEM((RET_HEADS, RET_KEY_DIM, RET_VAL_DIM), F32),
            pltpu.VMEM((2, rows, ATT_Q_W), BF16),
            pltpu.VMEM((2, rows, RET_V_W), BF16),
            pltpu.VMEM((rows, d), BF16),
        ],
        compiler_params=pltpu.CompilerParams(
            dimension_semantics=("arbitrary",), vmem_limit_bytes=V7X_VMEM_LIMIT_BYTES),
        name="mix",
    )(cdec, *proj_out, x, sinks, gn_gain, tril, w_att_up, w_ret_up, w_out, w_ff1, w_ff2)


def _ffn_kernel(h_ref, gain_ref, w1_ref, w2_ref, fgain_ref, o_ref, *, final_norm):
    for s in range(h_ref.shape[0] // FFN_SUB_ROWS):
        rs = slice(s * FFN_SUB_ROWS, (s + 1) * FFN_SUB_ROWS)
        h = h_ref[rs, :]
        xb = _rms_scale(h, gain_ref[...]).astype(BF16)
        acc = jnp.zeros(h.shape, F32)
        for c in range(D_FF // FFN_COLS):
            sl = slice(c * FFN_COLS, (c + 1) * FFN_COLS)
            a = jnp.maximum(jnp.dot(xb, w1_ref[:, sl], preferred_element_type=F32), 0.0)
            acc = acc + jnp.dot((a * a).astype(BF16), w2_ref[sl, :], preferred_element_type=F32)
        y = h + acc
        if final_norm:
            y = _rms_scale(y, fgain_ref[...])
        o_ref[rs, :] = y


def _ffn_call(h, gain, w1, w2, fgain, final_norm):
    b, s, d = h.shape
    rows = FFN_ROWS
    grid = (b, s // rows)

    def const(shape):
        return pl.BlockSpec(shape, lambda i, j: (0,) * len(shape), pipeline_mode=pl.Buffered(1))

    tok = pl.BlockSpec((None, rows, d), lambda i, j: (i, j, 0))
    return pl.pallas_call(
        functools.partial(_ffn_kernel, final_norm=final_norm),
        grid=grid,
        in_specs=[tok, const((1, d)), const((d, D_FF)), const((D_FF, d)), const((1, d))],
        out_specs=tok,
        out_shape=jax.ShapeDtypeStruct((b, s, d), F32),
        compiler_params=pltpu.CompilerParams(
            dimension_semantics=("arbitrary", "arbitrary"), vmem_limit_bytes=V7X_VMEM_LIMIT_BYTES),
        name="ffn",
    )(h, gain, w1, w2, fgain)


def _decay_terms():
    h, c = RET_HEADS, RET_CHUNK
    f32 = np.float32
    log_gamma = np.log1p(-np.exp2(f32(-5.0) - np.arange(h, dtype=f32)))
    idx = np.arange(c, dtype=f32)
    diff = idx[:, None] - idx[None, :]
    tril = np.where(diff >= 0, np.exp(f32(-c) * log_gamma)[:, None, None], f32(0.0)).astype(f32)
    q_decay = np.exp((idx + f32(1.0))[None, :] * log_gamma[:, None])
    k_decay = np.exp((f32(c - 1.0) - idx)[None, :] * log_gamma[:, None])
    chunk_decay = np.exp(f32(c) * log_gamma)
    return tril, q_decay, k_decay, chunk_decay


def _position_tables(seq_len, q_decay, k_decay):
    f32 = np.float32
    pos = np.arange(seq_len, dtype=f32)

    def cos_sin(dim, theta):
        inv_freq = f32(theta) ** (-np.arange(0, dim, 2, dtype=f32) / f32(dim))
        ang = pos[:, None] * inv_freq[None, :]
        return np.cos(ang), np.sin(ang)

    cos, sin = cos_sin(ROPE_DIM, ROPE_THETA)
    pad = ATT_HEAD_DIM - ROPE_DIM
    zeros_half = np.zeros_like(sin)
    one_head = lambda parts: np.tile(np.concatenate(parts, axis=-1), (1, LANES // ATT_HEAD_DIM))
    ca = one_head([cos, cos, np.ones((seq_len, pad), f32)])
    s1 = one_head([-sin, zeros_half, np.zeros((seq_len, pad), f32)])
    s2 = one_head([zeros_half, sin, np.zeros((seq_len, pad), f32)])
    cos_r, sin_r = cos_sin(RET_KEY_DIM, RET_ROT_BASE)
    cr = np.concatenate([cos_r, cos_r], axis=-1)
    sr = np.concatenate([-sin_r, sin_r], axis=-1)

    def per_row(decay):
        t = np.repeat(decay.T, RET_KEY_DIM, axis=1)
        return np.tile(t, (PROJ_ROWS // RET_CHUNK, 1))

    tables = (cos.T, sin.T, ca, s1, s2, cr, sr, per_row(q_decay), per_row(k_decay))
    return tuple(np.ascontiguousarray(t, dtype=f32) for t in tables)


def kernel(x, norm_mix_gain, w_in, b_gates, attn_sinks, ret_gn_gain, w_att_up, w_ret_up, w_out,
           norm_mlp_gain, w_ff1, w_ff2, norm_final_gain):
    b, s, d = x.shape
    depth = w_in.shape[0]
    assert d == D_MODEL and s % MIX_ROWS == 0 and s % PROJ_ROWS == 0 and s % FFN_ROWS == 0
    tril, q_decay, k_decay, cdec = _decay_terms()
    tables = _position_tables(s, q_decay, k_decay)
    row = lambda v: v.reshape(1, -1).astype(F32)
    h = x
    for l in range(depth):
        w = (w_in[l] * norm_mix_gain[l].astype(F32)[:, None]).astype(BF16)
        *proj_out, w_att, w_ret, w_o = _proj_call(h, w, row(b_gates[l]), tables, w_att_up[l], w_ret_up[l], w_out[l])
        sinks = jnp.repeat((attn_sinks[l].astype(F32) * LOG2E).reshape(ATT_KV_HEADS, 1, ATT_GROUP), BLOCK, axis=-1)
        h, w1, w2 = _mix_call(proj_out, h, cdec, sinks, row(ret_gn_gain[l]), tril, w_att, w_ret, w_o,
                              w_ff1[l], w_ff2[l])
        h = _ffn_call(h, row(norm_mlp_gain[l]), w1, w2, row(norm_final_gain), final_norm=(l == depth - 1))
    return h
```

```python
import functools

import jax
import jax.numpy as jnp
import numpy as np
from jax import lax
from jax.experimental import pallas as pl
from jax.experimental.pallas import tpu as pltpu

D_MODEL = 1024
ATT_HEADS = 8
ATT_KV_HEADS = 2
ATT_GROUP = ATT_HEADS // ATT_KV_HEADS
ATT_HEAD_DIM = 64
WINDOW = 128
ATT_BLOCK = 128
ROPE_DIM = ATT_HEAD_DIM // 4
ROPE_HALF = ROPE_DIM // 2
ROPE_THETA = 500000.0
RET_HEADS = 4
RET_KEY_DIM = 128
RET_VAL_DIM = 256
RET_CHUNK = 128
RET_ROT_BASE = 10000.0
D_FF = 4 * D_MODEL
NORM_EPS = 1e-6
GN_EPS = 1e-6
NEG_INF = -1e30
LOG2E = 1.4426950408889634

ATT_Q_W = ATT_HEADS * ATT_HEAD_DIM
ATT_KV_W = ATT_KV_HEADS * ATT_HEAD_DIM
RET_QK_W = RET_HEADS * RET_KEY_DIM
RET_V_W = RET_HEADS * RET_VAL_DIM
IN_SPLITS = (ATT_Q_W, ATT_KV_W, ATT_KV_W, RET_QK_W, RET_QK_W, RET_V_W, RET_V_W, D_MODEL, D_MODEL)
IN_WIDTH = sum(IN_SPLITS)
(OFF_QA, OFF_KA, OFF_VA, OFF_QR, OFF_KR, OFF_VR, OFF_GR, OFF_GA, OFF_GB) = (
    int(v) for v in np.concatenate([[0], np.cumsum(IN_SPLITS)[:-1]]))

LANES = 128
SUBLANES = 8
BF16_SUBLANES = 16
V7X_VMEM_LIMIT_BYTES = 56 * 1024 * 1024

BLOCK = 128
PROJ_ROWS = 1024
PROJ_COLS = 512
MIX_ROWS = 512
FFN_ROWS = 1024
FFN_SUB_ROWS = 512
FFN_COLS = 512

BF16 = jnp.bfloat16
F32 = jnp.float32

assert WINDOW == ATT_BLOCK == RET_CHUNK == BLOCK == LANES
assert RET_KEY_DIM == LANES and ATT_KV_W == LANES and ROPE_HALF == SUBLANES
assert PROJ_ROWS % BLOCK == 0 and MIX_ROWS % BLOCK == 0


def _sigmoid(x):
    return 0.5 * jnp.tanh(0.5 * x) + 0.5


def _silu(x):
    return x * _sigmoid(x)


def _inv_rms(x):
    return lax.rsqrt(jnp.mean(x * x, axis=-1, keepdims=True) + NORM_EPS)


def _rms_scale(x, gain):
    return (x * _inv_rms(x)) * gain


def _proj_kernel(x_ref, w_ref, bg_ref,
                 cat_ref, sat_ref, ca_ref, s1_ref, s2_ref, cr_ref, sr_ref, qdec_ref, kdec_ref,
                 watt_ref, wret_ref, wout_ref,
                 qat_ref, ka_ref, vat_ref, qdr_ref, kdt_ref, vr_ref, sg_ref, ga_ref, gb_ref,
                 wattb_ref, wretb_ref, woutb_ref):
    rows = x_ref.shape[0]
    nblk = rows // BLOCK
    wattb_ref[...] = watt_ref[...].astype(BF16)
    wretb_ref[...] = wret_ref[...].astype(BF16)
    woutb_ref[...] = wout_ref[...].astype(BF16)
    x = x_ref[...]
    xb = x.astype(BF16)
    inv = _inv_rms(x)

    def proj(off, width):
        return jnp.dot(xb, w_ref[:, off:off + width], preferred_element_type=F32) * inv

    def store_blocks(ref, feat, val):
        for c in range(nblk):
            ref[c, feat, :] = val[:, c * BLOCK:(c + 1) * BLOCK].astype(BF16)

    zt = proj(OFF_QA, ATT_Q_W).T
    cat, sat = cat_ref[...], sat_ref[...]
    for hd in range(ATT_HEADS):
        b0 = hd * ATT_HEAD_DIM
        x1, x2 = zt[b0:b0 + ROPE_HALF], zt[b0 + ROPE_HALF:b0 + ROPE_DIM]
        head = jnp.concatenate([x1 * cat - x2 * sat, x2 * cat + x1 * sat, zt[b0 + ROPE_DIM:b0 + ATT_HEAD_DIM]], axis=0)
        store_blocks(qat_ref, slice(b0, b0 + ATT_HEAD_DIM), head * (ATT_HEAD_DIM ** -0.5 * LOG2E))
    zkv = proj(OFF_KA, 2 * ATT_KV_W)
    store_blocks(vat_ref, slice(None), zkv[:, ATT_KV_W:].T)
    z = zkv[:, :ATT_KV_W]
    ka_ref[...] = (z * ca_ref[...] + pltpu.roll(z, LANES - ROPE_HALF, 1) * s1_ref[...]
                   + pltpu.roll(z, ROPE_HALF, 1) * s2_ref[...]).astype(BF16)

    cr, sr = cr_ref[...], sr_ref[...]
    every_block = lambda tab: jnp.concatenate([tab] * nblk, axis=0)

    def rope_ret(zh):
        return zh * cr + pltpu.roll(zh, RET_KEY_DIM // 2, 1) * sr

    z = proj(OFF_QR, RET_QK_W)
    for h in range(RET_HEADS):
        sl = slice(h * RET_KEY_DIM, (h + 1) * RET_KEY_DIM)
        qdr_ref[:, sl] = (rope_ret(z[:, sl]) * every_block(qdec_ref[:, sl])).astype(BF16)
    z = proj(OFF_KR, RET_QK_W)
    for h in range(RET_HEADS):
        sl = slice(h * RET_KEY_DIM, (h + 1) * RET_KEY_DIM)
        k = rope_ret(z[:, sl]) * (RET_KEY_DIM ** -0.5)
        store_blocks(kdt_ref, sl, (k * every_block(kdec_ref[:, sl])).T)
    for c in range(RET_V_W // PROJ_COLS):
        sl = slice(c * PROJ_COLS, (c + 1) * PROJ_COLS)
        sg_ref[:, sl] = _silu(proj(OFF_GR + c * PROJ_COLS, PROJ_COLS)).astype(BF16)
    for c in range(D_MODEL // PROJ_COLS):
        sl = slice(c * PROJ_COLS, (c + 1) * PROJ_COLS)
        g = proj(OFF_GA + c * PROJ_COLS, PROJ_COLS) + bg_ref[:, sl]
        ga_ref[:, sl] = _sigmoid(g).astype(BF16)
    for c in range(D_MODEL // PROJ_COLS):
        sl = slice(c * PROJ_COLS, (c + 1) * PROJ_COLS)
        g = proj(OFF_GB + c * PROJ_COLS, PROJ_COLS) + bg_ref[:, D_MODEL + c * PROJ_COLS:D_MODEL + (c + 1) * PROJ_COLS]
        gb_ref[:, sl] = _sigmoid(g).astype(BF16)
    for c in range(RET_V_W // PROJ_COLS):
        sl = slice(c * PROJ_COLS, (c + 1) * PROJ_COLS)
        vr_ref[:, sl] = proj(OFF_VR + c * PROJ_COLS, PROJ_COLS).astype(BF16)


def _slab_rows(nrows, nsteps):
    slab = max(BF16_SUBLANES, nrows // nsteps)
    assert nrows % slab == 0 and slab % BF16_SUBLANES == 0 and nrows // slab <= nsteps
    return slab


def _proj_call(x, w_in, b_gates, tables, w_att_up, w_ret_up, w_out):
    b, s, d = x.shape
    rows = PROJ_ROWS
    nblk = rows // BLOCK
    grid = (s // rows, b)

    def weight_slabs(w, out_block=lambda blk: blk):
        slab = _slab_rows(w.shape[0], grid[0] * grid[1])
        blk = lambda j, i: jnp.minimum(j * b + i, w.shape[0] // slab - 1)
        return (pl.BlockSpec((slab, w.shape[1]), lambda j, i: (blk(j, i), 0)),
                pl.BlockSpec((slab, w.shape[1]), lambda j, i: (out_block(blk(j, i)), 0)),
                jax.ShapeDtypeStruct(w.shape, BF16))

    att_slab = _slab_rows(w_att_up.shape[0], grid[0] * grid[1])
    assert ATT_HEAD_DIM % att_slab == 0
    head_slabs = ATT_HEAD_DIM // att_slab

    def att_out_block(blk):
        head, part = blk // head_slabs, blk % head_slabs
        return ((head % ATT_GROUP) * ATT_KV_HEADS + head // ATT_GROUP) * head_slabs + part

    weights = [weight_slabs(w_att_up, att_out_block), weight_slabs(w_ret_up), weight_slabs(w_out)]

    def tok(width):
        return pl.BlockSpec((None, rows, width), lambda j, i: (i, j, 0))

    def tok_t(feat):
        return pl.BlockSpec((None, nblk, feat, BLOCK), lambda j, i: (i, j, 0, 0))

    def const(shape):
        return pl.BlockSpec(shape, lambda j, i: (0,) * len(shape), pipeline_mode=pl.Buffered(1))

    pos_t = pl.BlockSpec((ROPE_HALF, rows), lambda j, i: (0, j))
    pos = pl.BlockSpec((rows, LANES), lambda j, i: (j, 0))
    tok_shape = lambda w: jax.ShapeDtypeStruct((b, s, w), BF16)
    tok_t_shape = lambda f: jax.ShapeDtypeStruct((b, s // BLOCK, f, BLOCK), BF16)
    return pl.pallas_call(
        _proj_kernel,
        grid=grid,
        in_specs=[tok(d), const((d, IN_WIDTH)),
                  const((1, 2 * D_MODEL)), pos_t, pos_t, pos, pos, pos, pos, pos,
                  const((BLOCK, RET_QK_W)), const((BLOCK, RET_QK_W))] + [w[0] for w in weights],
        out_specs=[tok_t(ATT_Q_W), tok(ATT_KV_W), tok_t(ATT_KV_W), tok(RET_QK_W),
                   tok_t(RET_QK_W), tok(RET_V_W), tok(RET_V_W), tok(D_MODEL), tok(D_MODEL)] + [w[1] for w in weights],
        out_shape=[tok_t_shape(ATT_Q_W), tok_shape(ATT_KV_W), tok_t_shape(ATT_KV_W), tok_shape(RET_QK_W),
                   tok_t_shape(RET_QK_W), tok_shape(RET_V_W),
                   tok_shape(RET_V_W), tok_shape(D_MODEL), tok_shape(D_MODEL)] + [w[2] for w in weights],
        compiler_params=pltpu.CompilerParams(
            dimension_semantics=("arbitrary", "arbitrary"), vmem_limit_bytes=V7X_VMEM_LIMIT_BYTES),
        name="proj",
    )(x, w_in, b_gates, *tables, w_att_up, w_ret_up, w_out)


def _mix_kernel(cdec_ref,
                qat_ref, ka_ref, vat_ref, qdr_ref, kdt_ref, vr_ref, sg_ref, ga_ref, gb_ref, x_ref,
                sink_ref, gn_ref, tril_ref, watt_ref, wret_ref, wout_ref, w1_ref, w2_ref,
                h_ref, w1b_ref, w2b_ref,
                kext_ref, vext_ref, state_ref, att_ref, ret_ref, merged_ref, *, tiles_per_seq, ntiles):
    t = pl.program_id(0)
    w1b_ref[...] = w1_ref[...].astype(BF16)
    w2b_ref[...] = w2_ref[...].astype(BF16)
    ws = t & 1
    rs = 1 - ws
    nblk = x_ref.shape[0] // BLOCK
    first = (jnp.minimum(t, ntiles - 1) % tiles_per_seq) == 0

    @pl.when(t == 0)
    def _():
        att_ref[1] = jnp.zeros(att_ref.shape[1:], BF16)
        ret_ref[1] = jnp.zeros(ret_ref.shape[1:], BF16)

    @pl.when(first)
    def _():
        kext_ref[:BLOCK, :] = jnp.zeros((BLOCK, ATT_KV_W), BF16)
        vext_ref[0] = jnp.zeros((ATT_KV_W, BLOCK), BF16)
        state_ref[...] = jnp.zeros_like(state_ref)

    kext_ref[BLOCK:, :] = ka_ref[...]
    vext_ref[1:] = vat_ref[...]

    from_prev = (lax.broadcasted_iota(jnp.int32, (BLOCK, BLOCK), 0)
                 > lax.broadcasted_iota(jnp.int32, (BLOCK, BLOCK), 1))

    def band_select(a, b):
        pick = lambda v, g: v[:, g * BLOCK:(g + 1) * BLOCK] if hasattr(v, "shape") else v
        return jnp.concatenate([jnp.where(from_prev, pick(a, g), pick(b, g)) for g in range(ATT_GROUP)], axis=1)
    zeros_q = jnp.zeros((ATT_HEAD_DIM, ATT_GROUP * BLOCK), BF16)

    ksl = [slice(h * RET_KEY_DIM, (h + 1) * RET_KEY_DIM) for h in range(RET_HEADS)]
    vsl = [slice(h * RET_VAL_DIM, (h + 1) * RET_VAL_DIM) for h in range(RET_HEADS)]
    rows_of = lambda c: pl.ds(c * BLOCK, BLOCK)

    def first_matmuls(c):
        rows = rows_of(c)
        qt = qat_ref[c]
        kc = kext_ref[pl.ds(c * BLOCK, 2 * BLOCK), :]
        scores = []
        for kv in range(ATT_KV_HEADS):
            qg = jnp.concatenate(
                [qt[(kv * ATT_GROUP + g) * ATT_HEAD_DIM:(kv * ATT_GROUP + g + 1) * ATT_HEAD_DIM] for g in range(ATT_GROUP)],
                axis=1)
            rhs = jnp.concatenate([qg, zeros_q] if kv == 0 else [zeros_q, qg], axis=0)
            scores.append(jnp.dot(kc, rhs, preferred_element_type=F32))
        zeros_k = jnp.zeros((RET_KEY_DIM, BLOCK), BF16)
        att = []
        for h in range(0, RET_HEADS, 2):
            pair = slice(h * RET_KEY_DIM, (h + 2) * RET_KEY_DIM)
            kdiag = jnp.concatenate([jnp.concatenate([kdt_ref[c, ksl[h], :], zeros_k], axis=1),
                                     jnp.concatenate([zeros_k, kdt_ref[c, ksl[h + 1], :]], axis=1)], axis=0)
            both = jnp.dot(qdr_ref[rows, pair], kdiag, preferred_element_type=F32)
            att += [both[:, :BLOCK], both[:, BLOCK:]]
        return scores, att

    def softmax_and_decay(c, stage1, state):
        scores, att = stage1
        probs, inv_den = [], []
        for kv in range(ATT_KV_HEADS):
            sc = scores[kv]
            prev = sc[:BLOCK]
            if c == 0:
                prev = prev + jnp.where(first, NEG_INF, 0.0).astype(F32)
            f = band_select(prev, sc[BLOCK:])
            sink = sink_ref[kv]
            m = jnp.maximum(jnp.max(f, axis=0, keepdims=True), sink)
            e = jnp.exp2(f - m)
            inv_den.append(1.0 / (jnp.sum(e, axis=0, keepdims=True) + jnp.exp2(sink - m)))
            probs.append(jnp.concatenate([band_select(e, 0.0), band_select(0.0, e)], axis=0).astype(BF16))
        att_b = [(att[h] * tril_ref[h]).astype(BF16) for h in range(RET_HEADS)]
        state_b = [state[h].astype(BF16) for h in range(RET_HEADS)]
        return probs, inv_den, att_b, state_b

    def second_matmuls(c, ops):
        probs, inv_den, att_b, state_b = ops
        rows = rows_of(c)
        vt_prev, vt_cur = vext_ref[c], vext_ref[c + 1]
        outs = []
        for kv in range(ATT_KV_HEADS):
            hs = slice(kv * ATT_HEAD_DIM, (kv + 1) * ATT_HEAD_DIM)
            vt = jnp.concatenate([vt_prev[hs], vt_cur[hs]], axis=1)
            outs.append(jnp.dot(vt, probs[kv], preferred_element_type=F32) * inv_den[kv])
        zeros_k = jnp.zeros((RET_KEY_DIM, RET_KEY_DIM), BF16)
        both = [jnp.dot(jnp.concatenate([jnp.concatenate([att_b[h], qdr_ref[rows, ksl[h]]], axis=1),
                                         jnp.concatenate([kdt_ref[c, ksl[h], :], zeros_k], axis=1)], axis=0),
                        jnp.concatenate([vr_ref[rows, vsl[h]], state_b[h]], axis=0), preferred_element_type=F32)
                for h in range(RET_HEADS)]
        ret = [bh[:BLOCK] for bh in both]
        upd = [bh[BLOCK:] for bh in both]
        return outs, ret, upd

    def finish(c, stage2):
        outs, ret, _ = stage2
        rows = rows_of(c)
        ot = jnp.concatenate(outs, axis=0)
        for g in range(ATT_GROUP):
            att_ref[ws, rows, g * LANES:(g + 1) * LANES] = ot[:, g * BLOCK:(g + 1) * BLOCK].T.astype(BF16)
        for h in range(RET_HEADS):
            out = ret[h]
            mu = jnp.mean(out, axis=-1, keepdims=True)
            dev = out - mu
            var = jnp.mean(dev * dev, axis=-1, keepdims=True)
            yn = dev * lax.rsqrt(var + GN_EPS) * gn_ref[:, vsl[h]]
            ret_ref[ws, rows, vsl[h]] = yn.astype(BF16) * sg_ref[rows, vsl[h]]

    def chunk_tail(c, stage1, state):
        stage2 = second_matmuls(c, softmax_and_decay(c, stage1, state))
        finish(c, stage2)
        return [state[h] * cdec_ref[h] + stage2[2][h] for h in range(RET_HEADS)]

    half = D_MODEL // 2
    cols = [slice(0, half), slice(half, D_MODEL)]

    assert nblk == 4
    state = [state_ref[h] for h in range(RET_HEADS)]
    def merged_half(n):
        ya = jnp.dot(att_ref[rs], watt_ref[:, cols[n]], preferred_element_type=F32)
        yr = jnp.dot(ret_ref[rs], wret_ref[:, cols[n]], preferred_element_type=F32)
        merged_ref[:, cols[n]] = ga_ref[:, cols[n]] * ya.astype(BF16) + gb_ref[:, cols[n]] * yr.astype(BF16)

    def out_half(n):
        h_ref[:, cols[n]] = x_ref[:, cols[n]] + jnp.dot(merged_ref[...], wout_ref[:, cols[n]],
                                                        preferred_element_type=F32)

    s0 = first_matmuls(0)
    s1 = first_matmuls(1)
    merged_half(0)
    state = chunk_tail(0, s0, state)
    s2 = first_matmuls(2)
    merged_half(1)
    state = chunk_tail(1, s1, state)
    s3 = first_matmuls(3)
    state = chunk_tail(2, s2, state)
    out_half(0)
    state = chunk_tail(3, s3, state)
    out_half(1)
    for h in range(RET_HEADS):
        state_ref[h] = state[h]

    kext_ref[:BLOCK, :] = kext_ref[nblk * BLOCK:, :]
    vext_ref[0] = vext_ref[nblk]


def _mix_call(proj_out, x, cdec, sinks, gn_gain, tril, w_att_up, w_ret_up, w_out, w_ff1, w_ff2):
    b, s, d = x.shape
    rows = MIX_ROWS
    nblk = rows // BLOCK
    tps = s // rows
    ntiles = b * tps
    grid = (ntiles + 1,)

    def weight_slab(w):
        slab = _slab_rows(w.shape[0], ntiles)
        return pl.BlockSpec((slab, w.shape[1]), lambda t: (jnp.minimum(t, w.shape[0] // slab - 1), 0))

    chunk_tile = lambda t: jnp.minimum(t, ntiles - 1)
    proj_tile = lambda t: jnp.maximum(t - 1, 0)

    def tok(width, tile):
        return pl.BlockSpec((None, rows, width), lambda t: (tile(t) // tps, tile(t) % tps, 0))

    def tok_t(feat):
        return pl.BlockSpec((None, nblk, feat, BLOCK), lambda t: (chunk_tile(t) // tps, chunk_tile(t) % tps, 0, 0))

    def const(shape):
        return pl.BlockSpec(shape, lambda t: (0,) * len(shape), pipeline_mode=pl.Buffered(1))

    smem = pl.BlockSpec(memory_space=pltpu.SMEM)
    ctok = lambda width: tok(width, chunk_tile)
    return pl.pallas_call(
        functools.partial(_mix_kernel, tiles_per_seq=tps, ntiles=ntiles),
        grid=grid,
        in_specs=[smem,
                  tok_t(ATT_Q_W), ctok(ATT_KV_W), tok_t(ATT_KV_W), ctok(RET_QK_W),
                  tok_t(RET_QK_W), ctok(RET_V_W), ctok(RET_V_W),
                  tok(D_MODEL, proj_tile), tok(D_MODEL, proj_tile), tok(d, proj_tile),
                  const((ATT_KV_HEADS, 1, ATT_GROUP * BLOCK)), const((1, RET_V_W)),
                  const((RET_HEADS, RET_CHUNK, RET_CHUNK)),
                  const((ATT_Q_W, d)), const((RET_V_W, d)), const((d, d)),
                  weight_slab(w_ff1), weight_slab(w_ff2)],
        out_specs=[tok(d, proj_tile), weight_slab(w_ff1), weight_slab(w_ff2)],
        out_shape=[jax.ShapeDtypeStruct((b, s, d), F32),
                   jax.ShapeDtypeStruct(w_ff1.shape, BF16), jax.ShapeDtypeStruct(w_ff2.shape, BF16)],
        scratch_shapes=[
            pltpu.VMEM((BLOCK + rows, ATT_KV_W), BF16),
            pltpu.VMEM((1 + nblk, ATT_KV_W, BLOCK), BF16),
            pltpu.VMEM((RET_HEADS, RET_KEY_DIM, RET_VAL_DIM), F32),
            pltpu.VMEM((2, rows, ATT_Q_W), BF16),
            pltpu.VMEM((2, rows, RET_V_W), BF16),
            pltpu.VMEM((rows, d), BF16),
        ],
        compiler_params=pltpu.CompilerParams(
            dimension_semantics=("arbitrary",), vmem_limit_bytes=V7X_VMEM_LIMIT_BYTES),
        name="mix",
    )(cdec, *proj_out, x, sinks, gn_gain, tril, w_att_up, w_ret_up, w_out, w_ff1, w_ff2)


def _ffn_kernel(h_ref, gain_ref, w1_ref, w2_ref, fgain_ref, o_ref, *, final_norm):
    for s in range(h_ref.shape[0] // FFN_SUB_ROWS):
        rs = slice(s * FFN_SUB_ROWS, (s + 1) * FFN_SUB_ROWS)
        h = h_ref[rs, :]
        xb = _rms_scale(h, gain_ref[...]).astype(BF16)
        acc = jnp.zeros(h.shape, F32)
        for c in range(D_FF // FFN_COLS):
            sl = slice(c * FFN_COLS, (c + 1) * FFN_COLS)
            a = jnp.maximum(jnp.dot(xb, w1_ref[:, sl], preferred_element_type=F32), 0.0)
            acc = acc + jnp.dot((a * a).astype(BF16), w2_ref[sl, :], preferred_element_type=F32)
        y = h + acc
        if final_norm:
            y = _rms_scale(y, fgain_ref[...])
        o_ref[rs, :] = y


def _ffn_call(h, gain, w1, w2, fgain, final_norm):
    b, s, d = h.shape
    rows = FFN_ROWS
    grid = (b, s // rows)

    def const(shape):
        return pl.BlockSpec(shape, lambda i, j: (0,) * len(shape), pipeline_mode=pl.Buffered(1))

    tok = pl.BlockSpec((None, rows, d), lambda i, j: (i, j, 0))
    return pl.pallas_call(
        functools.partial(_ffn_kernel, final_norm=final_norm),
        grid=grid,
        in_specs=[tok, const((1, d)), const((d, D_FF)), const((D_FF, d)), const((1, d))],
        out_specs=tok,
        out_shape=jax.ShapeDtypeStruct((b, s, d), F32),
        compiler_params=pltpu.CompilerParams(
            dimension_semantics=("arbitrary", "arbitrary"), vmem_limit_bytes=V7X_VMEM_LIMIT_BYTES),
        name="ffn",
    )(h, gain, w1, w2, fgain)


def _decay_terms():
    h, c = RET_HEADS, RET_CHUNK
    f32 = np.float32
    log_gamma = np.log1p(-np.exp2(f32(-5.0) - np.arange(h, dtype=f32)))
    idx = np.arange(c, dtype=f32)
    diff = idx[:, None] - idx[None, :]
    tril = np.where(diff >= 0, np.exp(f32(-c) * log_gamma)[:, None, None], f32(0.0)).astype(f32)
    q_decay = np.exp((idx + f32(1.0))[None, :] * log_gamma[:, None])
    k_decay = np.exp((f32(c - 1.0) - idx)[None, :] * log_gamma[:, None])
    chunk_decay = np.exp(f32(c) * log_gamma)
    return tril, q_decay, k_decay, chunk_decay


def _position_tables(seq_len, q_decay, k_decay):
    f32 = np.float32
    pos = np.arange(seq_len, dtype=f32)

    def cos_sin(dim, theta):
        inv_freq = f32(theta) ** (-np.arange(0, dim, 2, dtype=f32) / f32(dim))
        ang = pos[:, None] * inv_freq[None, :]
        return np.cos(ang), np.sin(ang)

    cos, sin = cos_sin(ROPE_DIM, ROPE_THETA)
    pad = ATT_HEAD_DIM - ROPE_DIM
    zeros_half = np.zeros_like(sin)
    one_head = lambda parts: np.tile(np.concatenate(parts, axis=-1), (1, LANES // ATT_HEAD_DIM))
    ca = one_head([cos, cos, np.ones((seq_len, pad), f32)])
    s1 = one_head([-sin, zeros_half, np.zeros((seq_len, pad), f32)])
    s2 = one_head([zeros_half, sin, np.zeros((seq_len, pad), f32)])
    cos_r, sin_r = cos_sin(RET_KEY_DIM, RET_ROT_BASE)
    cr = np.concatenate([cos_r, cos_r], axis=-1)
    sr = np.concatenate([-sin_r, sin_r], axis=-1)

    def per_row(decay):
        return np.repeat(decay.T, RET_KEY_DIM, axis=1)

    tables = (cos.T, sin.T, ca, s1, s2, cr, sr, per_row(q_decay), per_row(k_decay))
    return tuple(np.ascontiguousarray(t, dtype=f32) for t in tables)


def kernel(x, norm_mix_gain, w_in, b_gates, attn_sinks, ret_gn_gain, w_att_up, w_ret_up, w_out,
           norm_mlp_gain, w_ff1, w_ff2, norm_final_gain):
    b, s, d = x.shape
    depth = w_in.shape[0]
    assert d == D_MODEL and s % MIX_ROWS == 0 and s % PROJ_ROWS == 0 and s % FFN_ROWS == 0
    tril, q_decay, k_decay, cdec = _decay_terms()
    tables = _position_tables(s, q_decay, k_decay)
    row = lambda v: v.reshape(1, -1).astype(F32)
    h = x
    for l in range(depth):
        w = (w_in[l] * norm_mix_gain[l].astype(F32)[:, None]).astype(BF16)
        *proj_out, w_att, w_ret, w_o = _proj_call(h, w, row(b_gates[l]), tables, w_att_up[l], w_ret_up[l], w_out[l])
        sinks = jnp.repeat((attn_sinks[l].astype(F32) * LOG2E).reshape(ATT_KV_HEADS, 1, ATT_GROUP), BLOCK, axis=-1)
        h, w1, w2 = _mix_call(proj_out, h, cdec, sinks, row(ret_gn_gain[l]), tril, w_att, w_ret, w_o,
                              w_ff1[l], w_ff2[l])
        h = _ffn_call(h, row(norm_mlp_gain[l]), w1, w2, row(norm_final_gain), final_norm=(l == depth - 1))
    return h
```

```python
import functools

import jax
import jax.numpy as jnp
import numpy as np
from jax import lax
from jax.experimental import pallas as pl
from jax.experimental.pallas import tpu as pltpu

D_MODEL = 1024
ATT_HEADS = 8
ATT_KV_HEADS = 2
ATT_GROUP = ATT_HEADS // ATT_KV_HEADS
ATT_HEAD_DIM = 64
WINDOW = 128
ATT_BLOCK = 128
ROPE_DIM = ATT_HEAD_DIM // 4
ROPE_HALF = ROPE_DIM // 2
ROPE_THETA = 500000.0
RET_HEADS = 4
RET_KEY_DIM = 128
RET_VAL_DIM = 256
RET_CHUNK = 128
RET_ROT_BASE = 10000.0
D_FF = 4 * D_MODEL
NORM_EPS = 1e-6
GN_EPS = 1e-6
NEG_INF = -1e30
LOG2E = 1.4426950408889634

ATT_Q_W = ATT_HEADS * ATT_HEAD_DIM
ATT_KV_W = ATT_KV_HEADS * ATT_HEAD_DIM
RET_QK_W = RET_HEADS * RET_KEY_DIM
RET_V_W = RET_HEADS * RET_VAL_DIM
IN_SPLITS = (ATT_Q_W, ATT_KV_W, ATT_KV_W, RET_QK_W, RET_QK_W, RET_V_W, RET_V_W, D_MODEL, D_MODEL)
IN_WIDTH = sum(IN_SPLITS)
(OFF_QA, OFF_KA, OFF_VA, OFF_QR, OFF_KR, OFF_VR, OFF_GR, OFF_GA, OFF_GB) = (
    int(v) for v in np.concatenate([[0], np.cumsum(IN_SPLITS)[:-1]]))

LANES = 128
SUBLANES = 8
BF16_SUBLANES = 16
V7X_VMEM_LIMIT_BYTES = 56 * 1024 * 1024

BLOCK = 128
PROJ_ROWS = 1024
PROJ_COLS = 512
MIX_ROWS = 512
FFN_ROWS = 1024
FFN_SUB_ROWS = 512
FFN_COLS = 512
FFN_SKEW = 2

BF16 = jnp.bfloat16
F32 = jnp.float32

assert WINDOW == ATT_BLOCK == RET_CHUNK == BLOCK == LANES
assert RET_KEY_DIM == LANES and ATT_KV_W == LANES and ROPE_HALF == SUBLANES
assert PROJ_ROWS % BLOCK == 0 and MIX_ROWS % BLOCK == 0


def _sigmoid(x):
    return 0.5 * jnp.tanh(0.5 * x) + 0.5


def _silu(x):
    return x * _sigmoid(x)


def _inv_rms(x):
    return lax.rsqrt(jnp.mean(x * x, axis=-1, keepdims=True) + NORM_EPS)


def _rms_scale(x, gain):
    return (x * _inv_rms(x)) * gain


def _proj_kernel(x_ref, w_ref, bg_ref,
                 cat_ref, sat_ref, ca_ref, s1_ref, s2_ref, cr_ref, sr_ref, qdec_ref, kdec_ref,
                 watt_ref, wret_ref, wout_ref,
                 qat_ref, ka_ref, vat_ref, qdr_ref, kdt_ref, vr_ref, sg_ref, ga_ref, gb_ref,
                 wattb_ref, wretb_ref, woutb_ref):
    rows = x_ref.shape[0]
    nblk = rows // BLOCK
    wattb_ref[...] = watt_ref[...].astype(BF16)
    wretb_ref[...] = wret_ref[...].astype(BF16)
    woutb_ref[...] = wout_ref[...].astype(BF16)
    x = x_ref[...]
    xb = x.astype(BF16)
    inv = _inv_rms(x)

    def proj(off, width):
        return jnp.dot(xb, w_ref[:, off:off + width], preferred_element_type=F32) * inv

    def store_blocks(ref, feat, val):
        for c in range(nblk):
            ref[c, feat, :] = val[:, c * BLOCK:(c + 1) * BLOCK].astype(BF16)

    zt = proj(OFF_QA, ATT_Q_W).T
    cat, sat = cat_ref[...], sat_ref[...]
    for hd in range(ATT_HEADS):
        b0 = hd * ATT_HEAD_DIM
        x1, x2 = zt[b0:b0 + ROPE_HALF], zt[b0 + ROPE_HALF:b0 + ROPE_DIM]
        head = jnp.concatenate([x1 * cat - x2 * sat, x2 * cat + x1 * sat, zt[b0 + ROPE_DIM:b0 + ATT_HEAD_DIM]], axis=0)
        store_blocks(qat_ref, slice(b0, b0 + ATT_HEAD_DIM), head * (ATT_HEAD_DIM ** -0.5 * LOG2E))
    zkv = proj(OFF_KA, 2 * ATT_KV_W)
    store_blocks(vat_ref, slice(None), zkv[:, ATT_KV_W:].T)
    z = zkv[:, :ATT_KV_W]
    ka_ref[...] = (z * ca_ref[...] + pltpu.roll(z, LANES - ROPE_HALF, 1) * s1_ref[...]
                   + pltpu.roll(z, ROPE_HALF, 1) * s2_ref[...]).astype(BF16)

    cr, sr = cr_ref[...], sr_ref[...]
    every_block = lambda tab: jnp.concatenate([tab] * nblk, axis=0)

    def rope_ret(zh):
        return zh * cr + pltpu.roll(zh, RET_KEY_DIM // 2, 1) * sr

    z = proj(OFF_QR, RET_QK_W)
    for h in range(RET_HEADS):
        sl = slice(h * RET_KEY_DIM, (h + 1) * RET_KEY_DIM)
        qdr_ref[:, sl] = (rope_ret(z[:, sl]) * every_block(qdec_ref[:, sl])).astype(BF16)
    z = proj(OFF_KR, RET_QK_W)
    for h in range(RET_HEADS):
        sl = slice(h * RET_KEY_DIM, (h + 1) * RET_KEY_DIM)
        k = rope_ret(z[:, sl]) * (RET_KEY_DIM ** -0.5)
        store_blocks(kdt_ref, sl, (k * every_block(kdec_ref[:, sl])).T)
    for c in range(RET_V_W // PROJ_COLS):
        sl = slice(c * PROJ_COLS, (c + 1) * PROJ_COLS)
        sg_ref[:, sl] = _silu(proj(OFF_GR + c * PROJ_COLS, PROJ_COLS)).astype(BF16)
    for c in range(D_MODEL // PROJ_COLS):
        sl = slice(c * PROJ_COLS, (c + 1) * PROJ_COLS)
        g = proj(OFF_GA + c * PROJ_COLS, PROJ_COLS) + bg_ref[:, sl]
        ga_ref[:, sl] = _sigmoid(g).astype(BF16)
    for c in range(D_MODEL // PROJ_COLS):
        sl = slice(c * PROJ_COLS, (c + 1) * PROJ_COLS)
        g = proj(OFF_GB + c * PROJ_COLS, PROJ_COLS) + bg_ref[:, D_MODEL + c * PROJ_COLS:D_MODEL + (c + 1) * PROJ_COLS]
        gb_ref[:, sl] = _sigmoid(g).astype(BF16)
    for c in range(RET_V_W // PROJ_COLS):
        sl = slice(c * PROJ_COLS, (c + 1) * PROJ_COLS)
        vr_ref[:, sl] = proj(OFF_VR + c * PROJ_COLS, PROJ_COLS).astype(BF16)


def _slab_rows(nrows, nsteps):
    slab = max(BF16_SUBLANES, nrows // nsteps)
    assert nrows % slab == 0 and slab % BF16_SUBLANES == 0 and nrows // slab <= nsteps
    return slab


def _proj_call(x, w_in, b_gates, tables, w_att_up, w_ret_up, w_out):
    b, s, d = x.shape
    rows = PROJ_ROWS
    nblk = rows // BLOCK
    grid = (s // rows, b)

    def weight_slabs(w, out_block=lambda blk: blk):
        slab = _slab_rows(w.shape[0], grid[0] * grid[1])
        blk = lambda j, i: jnp.minimum(j * b + i, w.shape[0] // slab - 1)
        return (pl.BlockSpec((slab, w.shape[1]), lambda j, i: (blk(j, i), 0)),
                pl.BlockSpec((slab, w.shape[1]), lambda j, i: (out_block(blk(j, i)), 0)),
                jax.ShapeDtypeStruct(w.shape, BF16))

    att_slab = _slab_rows(w_att_up.shape[0], grid[0] * grid[1])
    assert ATT_HEAD_DIM % att_slab == 0
    head_slabs = ATT_HEAD_DIM // att_slab

    def att_out_block(blk):
        head, part = blk // head_slabs, blk % head_slabs
        return ((head % ATT_GROUP) * ATT_KV_HEADS + head // ATT_GROUP) * head_slabs + part

    weights = [weight_slabs(w_att_up, att_out_block), weight_slabs(w_ret_up), weight_slabs(w_out)]

    def tok(width):
        return pl.BlockSpec((None, rows, width), lambda j, i: (i, j, 0))

    def tok_t(feat):
        return pl.BlockSpec((None, nblk, feat, BLOCK), lambda j, i: (i, j, 0, 0))

    def const(shape):
        return pl.BlockSpec(shape, lambda j, i: (0,) * len(shape), pipeline_mode=pl.Buffered(1))

    pos_t = pl.BlockSpec((ROPE_HALF, rows), lambda j, i: (0, j))
    pos = pl.BlockSpec((rows, LANES), lambda j, i: (j, 0))
    tok_shape = lambda w: jax.ShapeDtypeStruct((b, s, w), BF16)
    tok_t_shape = lambda f: jax.ShapeDtypeStruct((b, s // BLOCK, f, BLOCK), BF16)
    return pl.pallas_call(
        _proj_kernel,
        grid=grid,
        in_specs=[tok(d), const((d, IN_WIDTH)),
                  const((1, 2 * D_MODEL)), pos_t, pos_t, pos, pos, pos, pos, pos,
                  const((BLOCK, RET_QK_W)), const((BLOCK, RET_QK_W))] + [w[0] for w in weights],
        out_specs=[tok_t(ATT_Q_W), tok(ATT_KV_W), tok_t(ATT_KV_W), tok(RET_QK_W),
                   tok_t(RET_QK_W), tok(RET_V_W), tok(RET_V_W), tok(D_MODEL), tok(D_MODEL)] + [w[1] for w in weights],
        out_shape=[tok_t_shape(ATT_Q_W), tok_shape(ATT_KV_W), tok_t_shape(ATT_KV_W), tok_shape(RET_QK_W),
                   tok_t_shape(RET_QK_W), tok_shape(RET_V_W),
                   tok_shape(RET_V_W), tok_shape(D_MODEL), tok_shape(D_MODEL)] + [w[2] for w in weights],
        compiler_params=pltpu.CompilerParams(
            dimension_semantics=("arbitrary", "arbitrary"), vmem_limit_bytes=V7X_VMEM_LIMIT_BYTES),
        name="proj",
    )(x, w_in, b_gates, *tables, w_att_up, w_ret_up, w_out)


def _mix_kernel(cdec_ref,
                qat_ref, ka_ref, vat_ref, qdr_ref, kdt_ref, vr_ref, sg_ref, ga_ref, gb_ref, x_ref,
                sink_ref, gn_ref, tril_ref, watt_ref, wret_ref, wout_ref, w1_ref, w2_ref,
                h_ref, w1b_ref, w2b_ref,
                kext_ref, vext_ref, state_ref, att_ref, ret_ref, merged_ref, *, tiles_per_seq, ntiles):
    t = pl.program_id(0)
    w1b_ref[...] = w1_ref[...].astype(BF16)
    w2b_ref[...] = w2_ref[...].astype(BF16)
    ws = t & 1
    rs = 1 - ws
    nblk = x_ref.shape[0] // BLOCK
    first = (jnp.minimum(t, ntiles - 1) % tiles_per_seq) == 0

    @pl.when(t == 0)
    def _():
        att_ref[1] = jnp.zeros(att_ref.shape[1:], BF16)
        ret_ref[1] = jnp.zeros(ret_ref.shape[1:], BF16)

    @pl.when(first)
    def _():
        kext_ref[:BLOCK, :] = jnp.zeros((BLOCK, ATT_KV_W), BF16)
        vext_ref[0] = jnp.zeros((ATT_KV_W, BLOCK), BF16)
        state_ref[...] = jnp.zeros_like(state_ref)

    kext_ref[BLOCK:, :] = ka_ref[...]
    vext_ref[1:] = vat_ref[...]

    from_prev = (lax.broadcasted_iota(jnp.int32, (BLOCK, BLOCK), 0)
                 > lax.broadcasted_iota(jnp.int32, (BLOCK, BLOCK), 1))

    def band_select(a, b):
        pick = lambda v, g: v[:, g * BLOCK:(g + 1) * BLOCK] if hasattr(v, "shape") else v
        return jnp.concatenate([jnp.where(from_prev, pick(a, g), pick(b, g)) for g in range(ATT_GROUP)], axis=1)
    zeros_q = jnp.zeros((ATT_HEAD_DIM, ATT_GROUP * BLOCK), BF16)

    ksl = [slice(h * RET_KEY_DIM, (h + 1) * RET_KEY_DIM) for h in range(RET_HEADS)]
    vsl = [slice(h * RET_VAL_DIM, (h + 1) * RET_VAL_DIM) for h in range(RET_HEADS)]
    rows_of = lambda c: pl.ds(c * BLOCK, BLOCK)

    def first_matmuls(c):
        rows = rows_of(c)
        qt = qat_ref[c]
        kc = kext_ref[pl.ds(c * BLOCK, 2 * BLOCK), :]
        scores = []
        for kv in range(ATT_KV_HEADS):
            qg = jnp.concatenate(
                [qt[(kv * ATT_GROUP + g) * ATT_HEAD_DIM:(kv * ATT_GROUP + g + 1) * ATT_HEAD_DIM] for g in range(ATT_GROUP)],
                axis=1)
            rhs = jnp.concatenate([qg, zeros_q] if kv == 0 else [zeros_q, qg], axis=0)
            scores.append(jnp.dot(kc, rhs, preferred_element_type=F32))
        zeros_k = jnp.zeros((RET_KEY_DIM, BLOCK), BF16)
        att = []
        for h in range(0, RET_HEADS, 2):
            pair = slice(h * RET_KEY_DIM, (h + 2) * RET_KEY_DIM)
            kdiag = jnp.concatenate([jnp.concatenate([kdt_ref[c, ksl[h], :], zeros_k], axis=1),
                                     jnp.concatenate([zeros_k, kdt_ref[c, ksl[h + 1], :]], axis=1)], axis=0)
            both = jnp.dot(qdr_ref[rows, pair], kdiag, preferred_element_type=F32)
            att += [both[:, :BLOCK], both[:, BLOCK:]]
        return scores, att

    def softmax_and_decay(c, stage1, state):
        scores, att = stage1
        probs, inv_den = [], []
        for kv in range(ATT_KV_HEADS):
            sc = scores[kv]
            prev = sc[:BLOCK]
            if c == 0:
                prev = prev + jnp.where(first, NEG_INF, 0.0).astype(F32)
            f = band_select(prev, sc[BLOCK:])
            sink = sink_ref[kv]
            m = jnp.maximum(jnp.max(f, axis=0, keepdims=True), sink)
            e = jnp.exp2(f - m)
            inv_den.append(1.0 / (jnp.sum(e, axis=0, keepdims=True) + jnp.exp2(sink - m)))
            probs.append(jnp.concatenate([band_select(e, 0.0), band_select(0.0, e)], axis=0).astype(BF16))
        att_b = [(att[h] * tril_ref[h]).astype(BF16) for h in range(RET_HEADS)]
        state_b = [state[h].astype(BF16) for h in range(RET_HEADS)]
        return probs, inv_den, att_b, state_b

    def second_matmuls(c, ops):
        probs, inv_den, att_b, state_b = ops
        rows = rows_of(c)
        vt_prev, vt_cur = vext_ref[c], vext_ref[c + 1]
        outs = []
        for kv in range(ATT_KV_HEADS):
            hs = slice(kv * ATT_HEAD_DIM, (kv + 1) * ATT_HEAD_DIM)
            vt = jnp.concatenate([vt_prev[hs], vt_cur[hs]], axis=1)
            outs.append(jnp.dot(vt, probs[kv], preferred_element_type=F32) * inv_den[kv])
        zeros_k = jnp.zeros((RET_KEY_DIM, RET_KEY_DIM), BF16)
        both = [jnp.dot(jnp.concatenate([jnp.concatenate([att_b[h], qdr_ref[rows, ksl[h]]], axis=1),
                                         jnp.concatenate([kdt_ref[c, ksl[h], :], zeros_k], axis=1)], axis=0),
                        jnp.concatenate([vr_ref[rows, vsl[h]], state_b[h]], axis=0), preferred_element_type=F32)
                for h in range(RET_HEADS)]
        ret = [bh[:BLOCK] for bh in both]
        upd = [bh[BLOCK:] for bh in both]
        return outs, ret, upd

    def finish(c, stage2):
        outs, ret, _ = stage2
        rows = rows_of(c)
        ot = jnp.concatenate(outs, axis=0)
        for g in range(ATT_GROUP):
            att_ref[ws, rows, g * LANES:(g + 1) * LANES] = ot[:, g * BLOCK:(g + 1) * BLOCK].T.astype(BF16)
        for h in range(RET_HEADS):
            out = ret[h]
            mu = jnp.mean(out, axis=-1, keepdims=True)
            dev = out - mu
            var = jnp.mean(dev * dev, axis=-1, keepdims=True)
            yn = dev * lax.rsqrt(var + GN_EPS) * gn_ref[:, vsl[h]]
            ret_ref[ws, rows, vsl[h]] = yn.astype(BF16) * sg_ref[rows, vsl[h]]

    def chunk_tail(c, stage1, state):
        stage2 = second_matmuls(c, softmax_and_decay(c, stage1, state))
        finish(c, stage2)
        return [state[h] * cdec_ref[h] + stage2[2][h] for h in range(RET_HEADS)]

    half = D_MODEL // 2
    cols = [slice(0, half), slice(half, D_MODEL)]

    assert nblk == 4
    state = [state_ref[h] for h in range(RET_HEADS)]
    def merged_half(n):
        ya = jnp.dot(att_ref[rs], watt_ref[:, cols[n]], preferred_element_type=F32)
        yr = jnp.dot(ret_ref[rs], wret_ref[:, cols[n]], preferred_element_type=F32)
        merged_ref[:, cols[n]] = ga_ref[:, cols[n]] * ya.astype(BF16) + gb_ref[:, cols[n]] * yr.astype(BF16)

    def out_half(n):
        h_ref[:, cols[n]] = x_ref[:, cols[n]] + jnp.dot(merged_ref[...], wout_ref[:, cols[n]],
                                                        preferred_element_type=F32)

    s0 = first_matmuls(0)
    s1 = first_matmuls(1)
    merged_half(0)
    state = chunk_tail(0, s0, state)
    s2 = first_matmuls(2)
    merged_half(1)
    state = chunk_tail(1, s1, state)
    s3 = first_matmuls(3)
    state = chunk_tail(2, s2, state)
    out_half(0)
    state = chunk_tail(3, s3, state)
    out_half(1)
    for h in range(RET_HEADS):
        state_ref[h] = state[h]

    kext_ref[:BLOCK, :] = kext_ref[nblk * BLOCK:, :]
    vext_ref[0] = vext_ref[nblk]


def _mix_call(proj_out, x, cdec, sinks, gn_gain, tril, w_att_up, w_ret_up, w_out, w_ff1, w_ff2):
    b, s, d = x.shape
    rows = MIX_ROWS
    nblk = rows // BLOCK
    tps = s // rows
    ntiles = b * tps
    grid = (ntiles + 1,)

    def weight_slab(w):
        slab = _slab_rows(w.shape[0], ntiles)
        return pl.BlockSpec((slab, w.shape[1]), lambda t: (jnp.minimum(t, w.shape[0] // slab - 1), 0))

    chunk_tile = lambda t: jnp.minimum(t, ntiles - 1)
    proj_tile = lambda t: jnp.maximum(t - 1, 0)

    def tok(width, tile):
        return pl.BlockSpec((None, rows, width), lambda t: (tile(t) // tps, tile(t) % tps, 0))

    def tok_t(feat):
        return pl.BlockSpec((None, nblk, feat, BLOCK), lambda t: (chunk_tile(t) // tps, chunk_tile(t) % tps, 0, 0))

    def const(shape):
        return pl.BlockSpec(shape, lambda t: (0,) * len(shape), pipeline_mode=pl.Buffered(1))

    smem = pl.BlockSpec(memory_space=pltpu.SMEM)
    ctok = lambda width: tok(width, chunk_tile)
    return pl.pallas_call(
        functools.partial(_mix_kernel, tiles_per_seq=tps, ntiles=ntiles),
        grid=grid,
        in_specs=[smem,
                  tok_t(ATT_Q_W), ctok(ATT_KV_W), tok_t(ATT_KV_W), ctok(RET_QK_W),
                  tok_t(RET_QK_W), ctok(RET_V_W), ctok(RET_V_W),
                  tok(D_MODEL, proj_tile), tok(D_MODEL, proj_tile), tok(d, proj_tile),
                  const((ATT_KV_HEADS, 1, ATT_GROUP * BLOCK)), const((1, RET_V_W)),
                  const((RET_HEADS, RET_CHUNK, RET_CHUNK)),
                  const((ATT_Q_W, d)), const((RET_V_W, d)), const((d, d)),
                  weight_slab(w_ff1), weight_slab(w_ff2)],
        out_specs=[tok(d, proj_tile), weight_slab(w_ff1), weight_slab(w_ff2)],
        out_shape=[jax.ShapeDtypeStruct((b, s, d), F32),
                   jax.ShapeDtypeStruct(w_ff1.shape, BF16), jax.ShapeDtypeStruct(w_ff2.shape, BF16)],
        scratch_shapes=[
            pltpu.VMEM((BLOCK + rows, ATT_KV_W), BF16),
            pltpu.VMEM((1 + nblk, ATT_KV_W, BLOCK), BF16),
            pltpu.VMEM((RET_HEADS, RET_KEY_DIM, RET_VAL_DIM), F32),
            pltpu.VMEM((2, rows, ATT_Q_W), BF16),
            pltpu.VMEM((2, rows, RET_V_W), BF16),
            pltpu.VMEM((rows, d), BF16),
        ],
        compiler_params=pltpu.CompilerParams(
            dimension_semantics=("arbitrary",), vmem_limit_bytes=V7X_VMEM_LIMIT_BYTES),
        name="mix",
    )(cdec, *proj_out, x, sinks, gn_gain, tril, w_att_up, w_ret_up, w_out, w_ff1, w_ff2)


def _ffn_kernel(h_ref, gain_ref, w1_ref, w2_ref, fgain_ref, o_ref, *, final_norm):
    nchunks = D_FF // FFN_COLS
    assert h_ref.shape[0] == 2 * FFN_SUB_ROWS and nchunks > FFN_SKEW
    rows_of = lambda s: slice(s * FFN_SUB_ROWS, (s + 1) * FFN_SUB_ROWS)

    def start(s):
        h = h_ref[rows_of(s), :]
        return h, _rms_scale(h, gain_ref[...]).astype(BF16), jnp.zeros(h.shape, F32)

    def chunk(xb, acc, c):
        sl = slice(c * FFN_COLS, (c + 1) * FFN_COLS)
        a = jnp.maximum(jnp.dot(xb, w1_ref[:, sl], preferred_element_type=F32), 0.0)
        return acc + jnp.dot((a * a).astype(BF16), w2_ref[sl, :], preferred_element_type=F32)

    def finish(s, h, acc):
        y = h + acc
        if final_norm:
            y = _rms_scale(y, fgain_ref[...])
        o_ref[rows_of(s), :] = y

    h0, xb0, acc0 = start(0)
    for c in range(nchunks - FFN_SKEW):
        acc0 = chunk(xb0, acc0, c)
    h1, xb1, acc1 = start(1)
    for k in range(FFN_SKEW):
        acc1 = chunk(xb1, acc1, k)
        acc0 = chunk(xb0, acc0, nchunks - FFN_SKEW + k)
    finish(0, h0, acc0)
    for c in range(FFN_SKEW, nchunks):
        acc1 = chunk(xb1, acc1, c)
    finish(1, h1, acc1)


def _ffn_call(h, gain, w1, w2, fgain, final_norm):
    b, s, d = h.shape
    rows = FFN_ROWS
    grid = (b, s // rows)

    def const(shape):
        return pl.BlockSpec(shape, lambda i, j: (0,) * len(shape), pipeline_mode=pl.Buffered(1))

    tok = pl.BlockSpec((None, rows, d), lambda i, j: (i, j, 0))
    return pl.pallas_call(
        functools.partial(_ffn_kernel, final_norm=final_norm),
        grid=grid,
        in_specs=[tok, const((1, d)), const((d, D_FF)), const((D_FF, d)), const((1, d))],
        out_specs=tok,
        out_shape=jax.ShapeDtypeStruct((b, s, d), F32),
        compiler_params=pltpu.CompilerParams(
            dimension_semantics=("arbitrary", "arbitrary"), vmem_limit_bytes=V7X_VMEM_LIMIT_BYTES),
        name="ffn",
    )(h, gain, w1, w2, fgain)


def _decay_terms():
    h, c = RET_HEADS, RET_CHUNK
    f32 = np.float32
    log_gamma = np.log1p(-np.exp2(f32(-5.0) - np.arange(h, dtype=f32)))
    idx = np.arange(c, dtype=f32)
    diff = idx[:, None] - idx[None, :]
    tril = np.where(diff >= 0, np.exp(f32(-c) * log_gamma)[:, None, None], f32(0.0)).astype(f32)
    q_decay = np.exp((idx + f32(1.0))[None, :] * log_gamma[:, None])
    k_decay = np.exp((f32(c - 1.0) - idx)[None, :] * log_gamma[:, None])
    chunk_decay = np.exp(f32(c) * log_gamma)
    return tril, q_decay, k_decay, chunk_decay


def _position_tables(seq_len, q_decay, k_decay):
    f32 = np.float32
    pos = np.arange(seq_len, dtype=f32)

    def cos_sin(dim, theta):
        inv_freq = f32(theta) ** (-np.arange(0, dim, 2, dtype=f32) / f32(dim))
        ang = pos[:, None] * inv_freq[None, :]
        return np.cos(ang), np.sin(ang)

    cos, sin = cos_sin(ROPE_DIM, ROPE_THETA)
    pad = ATT_HEAD_DIM - ROPE_DIM
    zeros_half = np.zeros_like(sin)
    one_head = lambda parts: np.tile(np.concatenate(parts, axis=-1), (1, LANES // ATT_HEAD_DIM))
    ca = one_head([cos, cos, np.ones((seq_len, pad), f32)])
    s1 = one_head([-sin, zeros_half, np.zeros((seq_len, pad), f32)])
    s2 = one_head([zeros_half, sin, np.zeros((seq_len, pad), f32)])
    cos_r, sin_r = cos_sin(RET_KEY_DIM, RET_ROT_BASE)
    cr = np.concatenate([cos_r, cos_r], axis=-1)
    sr = np.concatenate([-sin_r, sin_r], axis=-1)

    def per_row(decay):
        return np.repeat(decay.T, RET_KEY_DIM, axis=1)

    tables = (cos.T, sin.T, ca, s1, s2, cr, sr, per_row(q_decay), per_row(k_decay))
    return tuple(np.ascontiguousarray(t, dtype=f32) for t in tables)


def kernel(x, norm_mix_gain, w_in, b_gates, attn_sinks, ret_gn_gain, w_att_up, w_ret_up, w_out,
           norm_mlp_gain, w_ff1, w_ff2, norm_final_gain):
    b, s, d = x.shape
    depth = w_in.shape[0]
    assert d == D_MODEL and s % MIX_ROWS == 0 and s % PROJ_ROWS == 0 and s % FFN_ROWS == 0
    tril, q_decay, k_decay, cdec = _decay_terms()
    tables = _position_tables(s, q_decay, k_decay)
    row = lambda v: v.reshape(1, -1).astype(F32)
    h = x
    for l in range(depth):
        w = (w_in[l] * norm_mix_gain[l].astype(F32)[:, None]).astype(BF16)
        *proj_out, w_att, w_ret, w_o = _proj_call(h, w, row(b_gates[l]), tables, w_att_up[l], w_ret_up[l], w_out[l])
        sinks = jnp.repeat((attn_sinks[l].astype(F32) * LOG2E).reshape(ATT_KV_HEADS, 1, ATT_GROUP), BLOCK, axis=-1)
        h, w1, w2 = _mix_call(proj_out, h, cdec, sinks, row(ret_gn_gain[l]), tril, w_att, w_ret, w_o,
                              w_ff1[l], w_ff2[l])
        h = _ffn_call(h, row(norm_mlp_gain[l]), w1, w2, row(norm_final_gain), final_norm=(l == depth - 1))
    return h
```

```python
import functools

import jax
import jax.numpy as jnp
import numpy as np
from jax import lax
from jax.experimental import pallas as pl
from jax.experimental.pallas import tpu as pltpu

D_MODEL = 1024
ATT_HEADS = 8
ATT_KV_HEADS = 2
ATT_GROUP = ATT_HEADS // ATT_KV_HEADS
ATT_HEAD_DIM = 64
WINDOW = 128
ATT_BLOCK = 128
ROPE_DIM = ATT_HEAD_DIM // 4
ROPE_HALF = ROPE_DIM // 2
ROPE_THETA = 500000.0
RET_HEADS = 4
RET_KEY_DIM = 128
RET_VAL_DIM = 256
RET_CHUNK = 128
RET_ROT_BASE = 10000.0
D_FF = 4 * D_MODEL
NORM_EPS = 1e-6
GN_EPS = 1e-6
NEG_INF = -1e30
LOG2E = 1.4426950408889634

ATT_Q_W = ATT_HEADS * ATT_HEAD_DIM
ATT_KV_W = ATT_KV_HEADS * ATT_HEAD_DIM
RET_QK_W = RET_HEADS * RET_KEY_DIM
RET_V_W = RET_HEADS * RET_VAL_DIM
IN_SPLITS = (ATT_Q_W, ATT_KV_W, ATT_KV_W, RET_QK_W, RET_QK_W, RET_V_W, RET_V_W, D_MODEL, D_MODEL)
IN_WIDTH = sum(IN_SPLITS)
(OFF_QA, OFF_KA, OFF_VA, OFF_QR, OFF_KR, OFF_VR, OFF_GR, OFF_GA, OFF_GB) = (
    int(v) for v in np.concatenate([[0], np.cumsum(IN_SPLITS)[:-1]]))

LANES = 128
SUBLANES = 8
BF16_SUBLANES = 16
V7X_VMEM_LIMIT_BYTES = 56 * 1024 * 1024

BLOCK = 128
PROJ_ROWS = 1024
PROJ_COLS = 512
MIX_ROWS = 512
FFN_ROWS = 1024
FFN_SUB_ROWS = 512
FFN_COLS = 512
FFN_SKEW = 2

BF16 = jnp.bfloat16
F32 = jnp.float32

assert WINDOW == ATT_BLOCK == RET_CHUNK == BLOCK == LANES
assert RET_KEY_DIM == LANES and ATT_KV_W == LANES and ROPE_HALF == SUBLANES
assert PROJ_ROWS % BLOCK == 0 and MIX_ROWS % BLOCK == 0


def _sigmoid(x):
    return 0.5 * jnp.tanh(0.5 * x) + 0.5


def _silu(x):
    return x * _sigmoid(x)


def _inv_rms(x):
    return lax.rsqrt(jnp.mean(x * x, axis=-1, keepdims=True) + NORM_EPS)


def _rms_scale(x, gain):
    return (x * _inv_rms(x)) * gain


def _proj_kernel(x_ref, w_ref, bg_ref,
                 cat_ref, sat_ref, ca_ref, s1_ref, s2_ref, cr_ref, sr_ref, qdec_ref, kdec_ref,
                 watt_ref, wret_ref, wout_ref,
                 qat_ref, ka_ref, vat_ref, qdr_ref, kdt_ref, vr_ref, sg_ref, ga_ref, gb_ref,
                 wattb_ref, wretb_ref, woutb_ref):
    rows = x_ref.shape[0]
    nblk = rows // BLOCK
    wattb_ref[...] = watt_ref[...].astype(BF16)
    wretb_ref[...] = wret_ref[...].astype(BF16)
    woutb_ref[...] = wout_ref[...].astype(BF16)
    x = x_ref[...]
    xb = x.astype(BF16)
    inv = _inv_rms(x)

    def proj(off, width):
        return jnp.dot(xb, w_ref[:, off:off + width], preferred_element_type=F32) * inv

    def store_blocks(ref, feat, val):
        for c in range(nblk):
            ref[c, feat, :] = val[:, c * BLOCK:(c + 1) * BLOCK].astype(BF16)

    zt = proj(OFF_QA, ATT_Q_W).T
    cat, sat = cat_ref[...], sat_ref[...]
    for hd in range(ATT_HEADS):
        b0 = hd * ATT_HEAD_DIM
        x1, x2 = zt[b0:b0 + ROPE_HALF], zt[b0 + ROPE_HALF:b0 + ROPE_DIM]
        head = jnp.concatenate([x1 * cat - x2 * sat, x2 * cat + x1 * sat, zt[b0 + ROPE_DIM:b0 + ATT_HEAD_DIM]], axis=0)
        store_blocks(qat_ref, slice(b0, b0 + ATT_HEAD_DIM), head * (ATT_HEAD_DIM ** -0.5 * LOG2E))
    zkv = proj(OFF_KA, 2 * ATT_KV_W)
    store_blocks(vat_ref, slice(None), zkv[:, ATT_KV_W:].T)
    z = zkv[:, :ATT_KV_W]
    ka_ref[...] = (z * ca_ref[...] + pltpu.roll(z, LANES - ROPE_HALF, 1) * s1_ref[...]
                   + pltpu.roll(z, ROPE_HALF, 1) * s2_ref[...]).astype(BF16)

    cr, sr = cr_ref[...], sr_ref[...]
    every_block = lambda tab: jnp.concatenate([tab] * nblk, axis=0)

    def rope_ret(zh):
        return zh * cr + pltpu.roll(zh, RET_KEY_DIM // 2, 1) * sr

    z = proj(OFF_QR, RET_QK_W)
    for h in range(RET_HEADS):
        sl = slice(h * RET_KEY_DIM, (h + 1) * RET_KEY_DIM)
        qdr_ref[:, sl] = (rope_ret(z[:, sl]) * every_block(qdec_ref[:, sl])).astype(BF16)
    z = proj(OFF_KR, RET_QK_W)
    for h in range(RET_HEADS):
        sl = slice(h * RET_KEY_DIM, (h + 1) * RET_KEY_DIM)
        k = rope_ret(z[:, sl]) * (RET_KEY_DIM ** -0.5)
        store_blocks(kdt_ref, sl, (k * every_block(kdec_ref[:, sl])).T)
    for c in range(RET_V_W // PROJ_COLS):
        sl = slice(c * PROJ_COLS, (c + 1) * PROJ_COLS)
        sg_ref[:, sl] = _silu(proj(OFF_GR + c * PROJ_COLS, PROJ_COLS)).astype(BF16)
    for c in range(D_MODEL // PROJ_COLS):
        sl = slice(c * PROJ_COLS, (c + 1) * PROJ_COLS)
        g = proj(OFF_GA + c * PROJ_COLS, PROJ_COLS) + bg_ref[:, sl]
        ga_ref[:, sl] = _sigmoid(g).astype(BF16)
    for c in range(D_MODEL // PROJ_COLS):
        sl = slice(c * PROJ_COLS, (c + 1) * PROJ_COLS)
        g = proj(OFF_GB + c * PROJ_COLS, PROJ_COLS) + bg_ref[:, D_MODEL + c * PROJ_COLS:D_MODEL + (c + 1) * PROJ_COLS]
        gb_ref[:, sl] = _sigmoid(g).astype(BF16)
    for c in range(RET_V_W // PROJ_COLS):
        sl = slice(c * PROJ_COLS, (c + 1) * PROJ_COLS)
        vr_ref[:, sl] = proj(OFF_VR + c * PROJ_COLS, PROJ_COLS).astype(BF16)


def _slab_rows(nrows, nsteps):
    slab = max(BF16_SUBLANES, nrows // nsteps)
    assert nrows % slab == 0 and slab % BF16_SUBLANES == 0 and nrows // slab <= nsteps
    return slab


def _proj_call(x, w_in, b_gates, tables, w_att_up, w_ret_up, w_out):
    b, s, d = x.shape
    rows = PROJ_ROWS
    nblk = rows // BLOCK
    grid = (s // rows, b)

    def weight_slabs(w, out_block=lambda blk: blk):
        slab = _slab_rows(w.shape[0], grid[0] * grid[1])
        blk = lambda j, i: jnp.minimum(j * b + i, w.shape[0] // slab - 1)
        return (pl.BlockSpec((slab, w.shape[1]), lambda j, i: (blk(j, i), 0)),
                pl.BlockSpec((slab, w.shape[1]), lambda j, i: (out_block(blk(j, i)), 0)),
                jax.ShapeDtypeStruct(w.shape, BF16))

    att_slab = _slab_rows(w_att_up.shape[0], grid[0] * grid[1])
    assert ATT_HEAD_DIM % att_slab == 0
    head_slabs = ATT_HEAD_DIM // att_slab

    def att_out_block(blk):
        head, part = blk // head_slabs, blk % head_slabs
        return ((head % ATT_GROUP) * ATT_KV_HEADS + head // ATT_GROUP) * head_slabs + part

    weights = [weight_slabs(w_att_up, att_out_block), weight_slabs(w_ret_up), weight_slabs(w_out)]

    def tok(width):
        return pl.BlockSpec((None, rows, width), lambda j, i: (i, j, 0))

    def tok_t(feat):
        return pl.BlockSpec((None, nblk, feat, BLOCK), lambda j, i: (i, j, 0, 0))

    def const(shape):
        return pl.BlockSpec(shape, lambda j, i: (0,) * len(shape), pipeline_mode=pl.Buffered(1))

    pos_t = pl.BlockSpec((ROPE_HALF, rows), lambda j, i: (0, j))
    pos = pl.BlockSpec((rows, LANES), lambda j, i: (j, 0))
    tok_shape = lambda w: jax.ShapeDtypeStruct((b, s, w), BF16)
    tok_t_shape = lambda f: jax.ShapeDtypeStruct((b, s // BLOCK, f, BLOCK), BF16)
    return pl.pallas_call(
        _proj_kernel,
        grid=grid,
        in_specs=[tok(d), const((d, IN_WIDTH)),
                  const((1, 2 * D_MODEL)), pos_t, pos_t, pos, pos, pos, pos, pos,
                  const((BLOCK, RET_QK_W)), const((BLOCK, RET_QK_W))] + [w[0] for w in weights],
        out_specs=[tok_t(ATT_Q_W), tok(ATT_KV_W), tok_t(ATT_KV_W), tok(RET_QK_W),
                   tok_t(RET_QK_W), tok(RET_V_W), tok(RET_V_W), tok(D_MODEL), tok(D_MODEL)] + [w[1] for w in weights],
        out_shape=[tok_t_shape(ATT_Q_W), tok_shape(ATT_KV_W), tok_t_shape(ATT_KV_W), tok_shape(RET_QK_W),
                   tok_t_shape(RET_QK_W), tok_shape(RET_V_W),
                   tok_shape(RET_V_W), tok_shape(D_MODEL), tok_shape(D_MODEL)] + [w[2] for w in weights],
        compiler_params=pltpu.CompilerParams(
            dimension_semantics=("arbitrary", "arbitrary"), vmem_limit_bytes=V7X_VMEM_LIMIT_BYTES),
        name="proj",
    )(x, w_in, b_gates, *tables, w_att_up, w_ret_up, w_out)


def _mix_kernel(cdec_ref,
                qat_ref, ka_ref, vat_ref, qdr_ref, kdt_ref, vr_ref, sg_ref, ga_ref, gb_ref, x_ref,
                sink_ref, gn_ref, tril_ref, watt_ref, wret_ref, wout_ref, w1_ref, w2_ref,
                h_ref, w1b_ref, w2b_ref,
                kext_ref, vext_ref, state_ref, att_ref, ret_ref, merged_ref, *, tiles_per_seq, ntiles):
    t = pl.program_id(0)
    w1b_ref[...] = w1_ref[...].astype(BF16)
    w2b_ref[...] = w2_ref[...].astype(BF16)
    ws = t & 1
    rs = 1 - ws
    nblk = x_ref.shape[0] // BLOCK
    first = (jnp.minimum(t, ntiles - 1) % tiles_per_seq) == 0

    @pl.when(first)
    def _():
        kext_ref[:BLOCK, :] = jnp.zeros((BLOCK, ATT_KV_W), BF16)
        vext_ref[0] = jnp.zeros((ATT_KV_W, BLOCK), BF16)
        state_ref[...] = jnp.zeros_like(state_ref)

    kext_ref[BLOCK:, :] = ka_ref[...]
    vext_ref[1:] = vat_ref[...]

    from_prev = (lax.broadcasted_iota(jnp.int32, (BLOCK, BLOCK), 0)
                 > lax.broadcasted_iota(jnp.int32, (BLOCK, BLOCK), 1))

    def band_select(a, b):
        pick = lambda v, g: v[:, g * BLOCK:(g + 1) * BLOCK] if hasattr(v, "shape") else v
        return jnp.concatenate([jnp.where(from_prev, pick(a, g), pick(b, g)) for g in range(ATT_GROUP)], axis=1)
    zeros_q = jnp.zeros((ATT_HEAD_DIM, ATT_GROUP * BLOCK), BF16)

    ksl = [slice(h * RET_KEY_DIM, (h + 1) * RET_KEY_DIM) for h in range(RET_HEADS)]
    vsl = [slice(h * RET_VAL_DIM, (h + 1) * RET_VAL_DIM) for h in range(RET_HEADS)]
    rows_of = lambda c: pl.ds(c * BLOCK, BLOCK)

    def first_matmuls(c):
        rows = rows_of(c)
        qt = qat_ref[c]
        kc = kext_ref[pl.ds(c * BLOCK, 2 * BLOCK), :]
        scores = []
        for kv in range(ATT_KV_HEADS):
            qg = jnp.concatenate(
                [qt[(kv * ATT_GROUP + g) * ATT_HEAD_DIM:(kv * ATT_GROUP + g + 1) * ATT_HEAD_DIM] for g in range(ATT_GROUP)],
                axis=1)
            rhs = jnp.concatenate([qg, zeros_q] if kv == 0 else [zeros_q, qg], axis=0)
            scores.append(jnp.dot(kc, rhs, preferred_element_type=F32))
        zeros_k = jnp.zeros((RET_KEY_DIM, BLOCK), BF16)
        att = []
        for h in range(0, RET_HEADS, 2):
            pair = slice(h * RET_KEY_DIM, (h + 2) * RET_KEY_DIM)
            kdiag = jnp.concatenate([jnp.concatenate([kdt_ref[c, ksl[h], :], zeros_k], axis=1),
                                     jnp.concatenate([zeros_k, kdt_ref[c, ksl[h + 1], :]], axis=1)], axis=0)
            both = jnp.dot(qdr_ref[rows, pair], kdiag, preferred_element_type=F32)
            att += [both[:, :BLOCK], both[:, BLOCK:]]
        return scores, att

    def softmax_and_decay(c, stage1, state):
        scores, att = stage1
        probs, inv_den = [], []
        for kv in range(ATT_KV_HEADS):
            sc = scores[kv]
            prev = sc[:BLOCK]
            if c == 0:
                prev = prev + jnp.where(first, NEG_INF, 0.0).astype(F32)
            f = band_select(prev, sc[BLOCK:])
            sink = sink_ref[kv]
            m = jnp.maximum(jnp.max(f, axis=0, keepdims=True), sink)
            e = jnp.exp2(f - m)
            inv_den.append(1.0 / (jnp.sum(e, axis=0, keepdims=True) + jnp.exp2(sink - m)))
            probs.append(jnp.concatenate([band_select(e, 0.0), band_select(0.0, e)], axis=0).astype(BF16))
        att_b = [(att[h] * tril_ref[h]).astype(BF16) for h in range(RET_HEADS)]
        state_b = [state[h].astype(BF16) for h in range(RET_HEADS)]
        return probs, inv_den, att_b, state_b

    def second_matmuls(c, ops):
        probs, inv_den, att_b, state_b = ops
        rows = rows_of(c)
        vt_prev, vt_cur = vext_ref[c], vext_ref[c + 1]
        outs = []
        for kv in range(ATT_KV_HEADS):
            hs = slice(kv * ATT_HEAD_DIM, (kv + 1) * ATT_HEAD_DIM)
            vt = jnp.concatenate([vt_prev[hs], vt_cur[hs]], axis=1)
            outs.append(jnp.dot(vt, probs[kv], preferred_element_type=F32) * inv_den[kv])
        zeros_k = jnp.zeros((RET_KEY_DIM, RET_KEY_DIM), BF16)
        both = [jnp.dot(jnp.concatenate([jnp.concatenate([att_b[h], qdr_ref[rows, ksl[h]]], axis=1),
                                         jnp.concatenate([kdt_ref[c, ksl[h], :], zeros_k], axis=1)], axis=0),
                        jnp.concatenate([vr_ref[rows, vsl[h]], state_b[h]], axis=0), preferred_element_type=F32)
                for h in range(RET_HEADS)]
        ret = [bh[:BLOCK] for bh in both]
        upd = [bh[BLOCK:] for bh in both]
        return outs, ret, upd

    def finish(c, stage2):
        outs, ret, _ = stage2
        rows = rows_of(c)
        ot = jnp.concatenate(outs, axis=0)
        for g in range(ATT_GROUP):
            att_ref[ws, rows, g * LANES:(g + 1) * LANES] = ot[:, g * BLOCK:(g + 1) * BLOCK].T.astype(BF16)
        for h in range(RET_HEADS):
            out = ret[h]
            mu = jnp.mean(out, axis=-1, keepdims=True)
            dev = out - mu
            var = jnp.mean(dev * dev, axis=-1, keepdims=True)
            yn = dev * lax.rsqrt(var + GN_EPS) * gn_ref[:, vsl[h]]
            ret_ref[ws, rows, vsl[h]] = yn.astype(BF16) * sg_ref[rows, vsl[h]]

    def chunk_tail(c, stage1, state):
        stage2 = second_matmuls(c, softmax_and_decay(c, stage1, state))
        finish(c, stage2)
        return [state[h] * cdec_ref[h] + stage2[2][h] for h in range(RET_HEADS)]

    half = D_MODEL // 2
    cols = [slice(0, half), slice(half, D_MODEL)]

    assert nblk == 4

    def merged_half(n):
        ya = jnp.dot(att_ref[rs], watt_ref[:, cols[n]], preferred_element_type=F32)
        yr = jnp.dot(ret_ref[rs], wret_ref[:, cols[n]], preferred_element_type=F32)
        merged_ref[:, cols[n]] = ga_ref[:, cols[n]] * ya.astype(BF16) + gb_ref[:, cols[n]] * yr.astype(BF16)

    def out_half(n):
        h_ref[:, cols[n]] = x_ref[:, cols[n]] + jnp.dot(merged_ref[...], wout_ref[:, cols[n]],
                                                        preferred_element_type=F32)

    def run(chunks, projections):
        state = [state_ref[h] for h in range(RET_HEADS)]
        first_of = lambda c: first_matmuls(c) if chunks else None
        tail = lambda c, stage1, state: chunk_tail(c, stage1, state) if chunks else state
        s0 = first_of(0)
        s1 = first_of(1)
        if projections:
            merged_half(0)
        state = tail(0, s0, state)
        s2 = first_of(2)
        if projections:
            merged_half(1)
        state = tail(1, s1, state)
        s3 = first_of(3)
        state = tail(2, s2, state)
        if projections:
            out_half(0)
        state = tail(3, s3, state)
        if projections:
            out_half(1)
        if chunks:
            for h in range(RET_HEADS):
                state_ref[h] = state[h]
            kext_ref[:BLOCK, :] = kext_ref[nblk * BLOCK:, :]
            vext_ref[0] = vext_ref[nblk]

    pl.when(t == 0)(lambda: run(True, False))
    pl.when((t > 0) & (t < ntiles))(lambda: run(True, True))
    pl.when(t == ntiles)(lambda: run(False, True))


def _mix_call(proj_out, x, cdec, sinks, gn_gain, tril, w_att_up, w_ret_up, w_out, w_ff1, w_ff2):
    b, s, d = x.shape
    rows = MIX_ROWS
    nblk = rows // BLOCK
    tps = s // rows
    ntiles = b * tps
    grid = (ntiles + 1,)

    def weight_slab(w):
        slab = _slab_rows(w.shape[0], ntiles)
        return pl.BlockSpec((slab, w.shape[1]), lambda t: (jnp.minimum(t, w.shape[0] // slab - 1), 0))

    chunk_tile = lambda t: jnp.minimum(t, ntiles - 1)
    proj_tile = lambda t: jnp.maximum(t - 1, 0)

    def tok(width, tile):
        return pl.BlockSpec((None, rows, width), lambda t: (tile(t) // tps, tile(t) % tps, 0))

    def tok_t(feat):
        return pl.BlockSpec((None, nblk, feat, BLOCK), lambda t: (chunk_tile(t) // tps, chunk_tile(t) % tps, 0, 0))

    def const(shape):
        return pl.BlockSpec(shape, lambda t: (0,) * len(shape), pipeline_mode=pl.Buffered(1))

    smem = pl.BlockSpec(memory_space=pltpu.SMEM)
    ctok = lambda width: tok(width, chunk_tile)
    return pl.pallas_call(
        functools.partial(_mix_kernel, tiles_per_seq=tps, ntiles=ntiles),
        grid=grid,
        in_specs=[smem,
                  tok_t(ATT_Q_W), ctok(ATT_KV_W), tok_t(ATT_KV_W), ctok(RET_QK_W),
                  tok_t(RET_QK_W), ctok(RET_V_W), ctok(RET_V_W),
                  tok(D_MODEL, proj_tile), tok(D_MODEL, proj_tile), tok(d, proj_tile),
                  const((ATT_KV_HEADS, 1, ATT_GROUP * BLOCK)), const((1, RET_V_W)),
                  const((RET_HEADS, RET_CHUNK, RET_CHUNK)),
                  const((ATT_Q_W, d)), const((RET_V_W, d)), const((d, d)),
                  weight_slab(w_ff1), weight_slab(w_ff2)],
        out_specs=[tok(d, proj_tile), weight_slab(w_ff1), weight_slab(w_ff2)],
        out_shape=[jax.ShapeDtypeStruct((b, s, d), F32),
                   jax.ShapeDtypeStruct(w_ff1.shape, BF16), jax.ShapeDtypeStruct(w_ff2.shape, BF16)],
        scratch_shapes=[
            pltpu.VMEM((BLOCK + rows, ATT_KV_W), BF16),
            pltpu.VMEM((1 + nblk, ATT_KV_W, BLOCK), BF16),
            pltpu.VMEM((RET_HEADS, RET_KEY_DIM, RET_VAL_DIM), F32),
            pltpu.VMEM((2, rows, ATT_Q_W), BF16),
            pltpu.VMEM((2, rows, RET_V_W), BF16),
            pltpu.VMEM((rows, d), BF16),
        ],
        compiler_params=pltpu.CompilerParams(
            dimension_semantics=("arbitrary",), vmem_limit_bytes=V7X_VMEM_LIMIT_BYTES),
        name="mix",
    )(cdec, *proj_out, x, sinks, gn_gain, tril, w_att_up, w_ret_up, w_out, w_ff1, w_ff2)


def _ffn_kernel(h_ref, gain_ref, w1_ref, w2_ref, fgain_ref, o_ref, *, final_norm):
    nchunks = D_FF // FFN_COLS
    assert h_ref.shape[0] == 2 * FFN_SUB_ROWS and nchunks > FFN_SKEW
    rows_of = lambda s: slice(s * FFN_SUB_ROWS, (s + 1) * FFN_SUB_ROWS)

    def start(s):
        h = h_ref[rows_of(s), :]
        return h, _rms_scale(h, gain_ref[...]).astype(BF16), jnp.zeros(h.shape, F32)

    def chunk(xb, acc, c):
        sl = slice(c * FFN_COLS, (c + 1) * FFN_COLS)
        a = jnp.maximum(jnp.dot(xb, w1_ref[:, sl], preferred_element_type=F32), 0.0)
        return acc + jnp.dot((a * a).astype(BF16), w2_ref[sl, :], preferred_element_type=F32)

    def finish(s, h, acc):
        y = h + acc
        if final_norm:
            y = _rms_scale(y, fgain_ref[...])
        o_ref[rows_of(s), :] = y

    h0, xb0, acc0 = start(0)
    for c in range(nchunks - FFN_SKEW):
        acc0 = chunk(xb0, acc0, c)
    h1, xb1, acc1 = start(1)
    for k in range(FFN_SKEW):
        acc1 = chunk(xb1, acc1, k)
        acc0 = chunk(xb0, acc0, nchunks - FFN_SKEW + k)
    for c in range(FFN_SKEW, nchunks):
        acc1 = chunk(xb1, acc1, c)
        if c == FFN_SKEW + 1:
            finish(0, h0, acc0)
    finish(1, h1, acc1)


def _ffn_call(h, gain, w1, w2, fgain, final_norm):
    b, s, d = h.shape
    rows = FFN_ROWS
    grid = (b, s // rows)

    def const(shape):
        return pl.BlockSpec(shape, lambda i, j: (0,) * len(shape), pipeline_mode=pl.Buffered(1))

    tok = pl.BlockSpec((None, rows, d), lambda i, j: (i, j, 0))
    return pl.pallas_call(
        functools.partial(_ffn_kernel, final_norm=final_norm),
        grid=grid,
        in_specs=[tok, const((1, d)), const((d, D_FF)), const((D_FF, d)), const((1, d))],
        out_specs=tok,
        out_shape=jax.ShapeDtypeStruct((b, s, d), F32),
        compiler_params=pltpu.CompilerParams(
            dimension_semantics=("arbitrary", "arbitrary"), vmem_limit_bytes=V7X_VMEM_LIMIT_BYTES),
        name="ffn",
    )(h, gain, w1, w2, fgain)


def _decay_terms():
    h, c = RET_HEADS, RET_CHUNK
    f32 = np.float32
    log_gamma = np.log1p(-np.exp2(f32(-5.0) - np.arange(h, dtype=f32)))
    idx = np.arange(c, dtype=f32)
    diff = idx[:, None] - idx[None, :]
    tril = np.where(diff >= 0, np.exp(f32(-c) * log_gamma)[:, None, None], f32(0.0)).astype(f32)
    q_decay = np.exp((idx + f32(1.0))[None, :] * log_gamma[:, None])
    k_decay = np.exp((f32(c - 1.0) - idx)[None, :] * log_gamma[:, None])
    chunk_decay = np.exp(f32(c) * log_gamma)
    return tril, q_decay, k_decay, chunk_decay


def _position_tables(seq_len, q_decay, k_decay):
    f32 = np.float32
    pos = np.arange(seq_len, dtype=f32)

    def cos_sin(dim, theta):
        inv_freq = f32(theta) ** (-np.arange(0, dim, 2, dtype=f32) / f32(dim))
        ang = pos[:, None] * inv_freq[None, :]
        return np.cos(ang), np.sin(ang)

    cos, sin = cos_sin(ROPE_DIM, ROPE_THETA)
    pad = ATT_HEAD_DIM - ROPE_DIM
    zeros_half = np.zeros_like(sin)
    one_head = lambda parts: np.tile(np.concatenate(parts, axis=-1), (1, LANES // ATT_HEAD_DIM))
    ca = one_head([cos, cos, np.ones((seq_len, pad), f32)])
    s1 = one_head([-sin, zeros_half, np.zeros((seq_len, pad), f32)])
    s2 = one_head([zeros_half, sin, np.zeros((seq_len, pad), f32)])
    cos_r, sin_r = cos_sin(RET_KEY_DIM, RET_ROT_BASE)
    cr = np.concatenate([cos_r, cos_r], axis=-1)
    sr = np.concatenate([-sin_r, sin_r], axis=-1)

    def per_row(decay):
        return np.repeat(decay.T, RET_KEY_DIM, axis=1)

    tables = (cos.T, sin.T, ca, s1, s2, cr, sr, per_row(q_decay), per_row(k_decay))
    return tuple(np.ascontiguousarray(t, dtype=f32) for t in tables)


def kernel(x, norm_mix_gain, w_in, b_gates, attn_sinks, ret_gn_gain, w_att_up, w_ret_up, w_out,
           norm_mlp_gain, w_ff1, w_ff2, norm_final_gain):
    b, s, d = x.shape
    depth = w_in.shape[0]
    assert d == D_MODEL and s % MIX_ROWS == 0 and s % PROJ_ROWS == 0 and s % FFN_ROWS == 0
    tril, q_decay, k_decay, cdec = _decay_terms()
    tables = _position_tables(s, q_decay, k_decay)
    row = lambda v: v.reshape(1, -1).astype(F32)
    h = x
    for l in range(depth):
        w = (w_in[l] * norm_mix_gain[l].astype(F32)[:, None]).astype(BF16)
        *proj_out, w_att, w_ret, w_o = _proj_call(h, w, row(b_gates[l]), tables, w_att_up[l], w_ret_up[l], w_out[l])
        sinks = jnp.repeat((attn_sinks[l].astype(F32) * LOG2E).reshape(ATT_KV_HEADS, 1, ATT_GROUP), BLOCK, axis=-1)
        h, w1, w2 = _mix_call(proj_out, h, cdec, sinks, row(ret_gn_gain[l]), tril, w_att, w_ret, w_o,
                              w_ff1[l], w_ff2[l])
        h = _ffn_call(h, row(norm_mlp_gain[l]), w1, w2, row(norm_final_gain), final_norm=(l == depth - 1))
    return h
```

```python
import functools

import jax
import jax.numpy as jnp
import numpy as np
from jax import lax
from jax.experimental import pallas as pl
from jax.experimental.pallas import tpu as pltpu

D_MODEL = 1024
ATT_HEADS = 8
ATT_KV_HEADS = 2
ATT_GROUP = ATT_HEADS // ATT_KV_HEADS
ATT_HEAD_DIM = 64
WINDOW = 128
ATT_BLOCK = 128
ROPE_DIM = ATT_HEAD_DIM // 4
ROPE_HALF = ROPE_DIM // 2
ROPE_THETA = 500000.0
RET_HEADS = 4
RET_KEY_DIM = 128
RET_VAL_DIM = 256
RET_CHUNK = 128
RET_ROT_BASE = 10000.0
D_FF = 4 * D_MODEL
NORM_EPS = 1e-6
GN_EPS = 1e-6
NEG_INF = -1e30
LOG2E = 1.4426950408889634

ATT_Q_W = ATT_HEADS * ATT_HEAD_DIM
ATT_KV_W = ATT_KV_HEADS * ATT_HEAD_DIM
RET_QK_W = RET_HEADS * RET_KEY_DIM
RET_V_W = RET_HEADS * RET_VAL_DIM
IN_SPLITS = (ATT_Q_W, ATT_KV_W, ATT_KV_W, RET_QK_W, RET_QK_W, RET_V_W, RET_V_W, D_MODEL, D_MODEL)
IN_WIDTH = sum(IN_SPLITS)
(OFF_QA, OFF_KA, OFF_VA, OFF_QR, OFF_KR, OFF_VR, OFF_GR, OFF_GA, OFF_GB) = (
    int(v) for v in np.concatenate([[0], np.cumsum(IN_SPLITS)[:-1]]))

LANES = 128
SUBLANES = 8
BF16_SUBLANES = 16
V7X_VMEM_LIMIT_BYTES = 62 * 1024 * 1024

BLOCK = 128
PROJ_ROWS = 1024
PROJ_COLS = 512
MIX_ROWS = 512
FFN_ROWS = 2048
FFN_SUB_ROWS = 512
FFN_COLS = 512
FFN_SKEW = 2

BF16 = jnp.bfloat16
F32 = jnp.float32

assert WINDOW == ATT_BLOCK == RET_CHUNK == BLOCK == LANES
assert RET_KEY_DIM == LANES and ATT_KV_W == LANES and ROPE_HALF == SUBLANES
assert PROJ_ROWS % BLOCK == 0 and MIX_ROWS % BLOCK == 0


def _sigmoid(x):
    return 0.5 * jnp.tanh(0.5 * x) + 0.5


def _silu(x):
    return x * _sigmoid(x)


def _inv_rms(x):
    return lax.rsqrt(jnp.mean(x * x, axis=-1, keepdims=True) + NORM_EPS)


def _rms_scale(x, gain):
    return (x * _inv_rms(x)) * gain


def _proj_kernel(x_ref, w_ref, bg_ref,
                 cat_ref, sat_ref, ca_ref, s1_ref, s2_ref, cr_ref, sr_ref, qdec_ref, kdec_ref,
                 watt_ref, wret_ref, wout_ref,
                 qat_ref, ka_ref, vat_ref, qdr_ref, kdt_ref, vr_ref, sg_ref, ga_ref, gb_ref,
                 wattb_ref, wretb_ref, woutb_ref):
    rows = x_ref.shape[0]
    nblk = rows // BLOCK
    wattb_ref[...] = watt_ref[...].astype(BF16)
    wretb_ref[...] = wret_ref[...].astype(BF16)
    woutb_ref[...] = wout_ref[...].astype(BF16)
    x = x_ref[...]
    xb = x.astype(BF16)
    inv = _inv_rms(x)

    def proj(off, width):
        return jnp.dot(xb, w_ref[:, off:off + width], preferred_element_type=F32) * inv

    def store_blocks(ref, feat, val):
        for c in range(nblk):
            ref[c, feat, :] = val[:, c * BLOCK:(c + 1) * BLOCK].astype(BF16)

    zt = proj(OFF_QA, ATT_Q_W).T
    cat, sat = cat_ref[...], sat_ref[...]
    for hd in range(ATT_HEADS):
        b0 = hd * ATT_HEAD_DIM
        x1, x2 = zt[b0:b0 + ROPE_HALF], zt[b0 + ROPE_HALF:b0 + ROPE_DIM]
        head = jnp.concatenate([x1 * cat - x2 * sat, x2 * cat + x1 * sat, zt[b0 + ROPE_DIM:b0 + ATT_HEAD_DIM]], axis=0)
        store_blocks(qat_ref, slice(b0, b0 + ATT_HEAD_DIM), head * (ATT_HEAD_DIM ** -0.5 * LOG2E))
    zkv = proj(OFF_KA, 2 * ATT_KV_W)
    store_blocks(vat_ref, slice(None), zkv[:, ATT_KV_W:].T)
    z = zkv[:, :ATT_KV_W]
    ka_ref[...] = (z * ca_ref[...] + pltpu.roll(z, LANES - ROPE_HALF, 1) * s1_ref[...]
                   + pltpu.roll(z, ROPE_HALF, 1) * s2_ref[...]).astype(BF16)

    cr, sr = cr_ref[...], sr_ref[...]
    every_block = lambda tab: jnp.concatenate([tab] * nblk, axis=0)

    def rope_ret(zh):
        return zh * cr + pltpu.roll(zh, RET_KEY_DIM // 2, 1) * sr

    z = proj(OFF_QR, RET_QK_W)
    for h in range(RET_HEADS):
        sl = slice(h * RET_KEY_DIM, (h + 1) * RET_KEY_DIM)
        qdr_ref[:, sl] = (rope_ret(z[:, sl]) * every_block(qdec_ref[:, sl])).astype(BF16)
    z = proj(OFF_KR, RET_QK_W)
    for h in range(RET_HEADS):
        sl = slice(h * RET_KEY_DIM, (h + 1) * RET_KEY_DIM)
        k = rope_ret(z[:, sl]) * (RET_KEY_DIM ** -0.5)
        store_blocks(kdt_ref, sl, (k * every_block(kdec_ref[:, sl])).T)
    for c in range(RET_V_W // PROJ_COLS):
        sl = slice(c * PROJ_COLS, (c + 1) * PROJ_COLS)
        sg_ref[:, sl] = _silu(proj(OFF_GR + c * PROJ_COLS, PROJ_COLS)).astype(BF16)
    for c in range(D_MODEL // PROJ_COLS):
        sl = slice(c * PROJ_COLS, (c + 1) * PROJ_COLS)
        g = proj(OFF_GA + c * PROJ_COLS, PROJ_COLS) + bg_ref[:, sl]
        ga_ref[:, sl] = _sigmoid(g).astype(BF16)
    for c in range(D_MODEL // PROJ_COLS):
        sl = slice(c * PROJ_COLS, (c + 1) * PROJ_COLS)
        g = proj(OFF_GB + c * PROJ_COLS, PROJ_COLS) + bg_ref[:, D_MODEL + c * PROJ_COLS:D_MODEL + (c + 1) * PROJ_COLS]
        gb_ref[:, sl] = _sigmoid(g).astype(BF16)
    for c in range(RET_V_W // PROJ_COLS):
        sl = slice(c * PROJ_COLS, (c + 1) * PROJ_COLS)
        vr_ref[:, sl] = proj(OFF_VR + c * PROJ_COLS, PROJ_COLS).astype(BF16)


def _slab_rows(nrows, nsteps):
    slab = max(BF16_SUBLANES, nrows // nsteps)
    assert nrows % slab == 0 and slab % BF16_SUBLANES == 0 and nrows // slab <= nsteps
    return slab


def _proj_call(x, w_in, b_gates, tables, w_att_up, w_ret_up, w_out):
    b, s, d = x.shape
    rows = PROJ_ROWS
    nblk = rows // BLOCK
    grid = (s // rows, b)

    def weight_slabs(w, out_block=lambda blk: blk):
        slab = _slab_rows(w.shape[0], grid[0] * grid[1])
        blk = lambda j, i: jnp.minimum(j * b + i, w.shape[0] // slab - 1)
        return (pl.BlockSpec((slab, w.shape[1]), lambda j, i: (blk(j, i), 0)),
                pl.BlockSpec((slab, w.shape[1]), lambda j, i: (out_block(blk(j, i)), 0)),
                jax.ShapeDtypeStruct(w.shape, BF16))

    att_slab = _slab_rows(w_att_up.shape[0], grid[0] * grid[1])
    assert ATT_HEAD_DIM % att_slab == 0
    head_slabs = ATT_HEAD_DIM // att_slab

    def att_out_block(blk):
        head, part = blk // head_slabs, blk % head_slabs
        return ((head % ATT_GROUP) * ATT_KV_HEADS + head // ATT_GROUP) * head_slabs + part

    weights = [weight_slabs(w_att_up, att_out_block), weight_slabs(w_ret_up), weight_slabs(w_out)]

    def tok(width):
        return pl.BlockSpec((None, rows, width), lambda j, i: (i, j, 0))

    def tok_t(feat):
        return pl.BlockSpec((None, nblk, feat, BLOCK), lambda j, i: (i, j, 0, 0))

    def const(shape):
        return pl.BlockSpec(shape, lambda j, i: (0,) * len(shape), pipeline_mode=pl.Buffered(1))

    pos_t = pl.BlockSpec((ROPE_HALF, rows), lambda j, i: (0, j))
    pos = pl.BlockSpec((rows, LANES), lambda j, i: (j, 0))
    tok_shape = lambda w: jax.ShapeDtypeStruct((b, s, w), BF16)
    tok_t_shape = lambda f: jax.ShapeDtypeStruct((b, s // BLOCK, f, BLOCK), BF16)
    return pl.pallas_call(
        _proj_kernel,
        grid=grid,
        in_specs=[tok(d), const((d, IN_WIDTH)),
                  const((1, 2 * D_MODEL)), pos_t, pos_t, pos, pos, pos, pos, pos,
                  const((BLOCK, RET_QK_W)), const((BLOCK, RET_QK_W))] + [w[0] for w in weights],
        out_specs=[tok_t(ATT_Q_W), tok(ATT_KV_W), tok_t(ATT_KV_W), tok(RET_QK_W),
                   tok_t(RET_QK_W), tok(RET_V_W), tok(RET_V_W), tok(D_MODEL), tok(D_MODEL)] + [w[1] for w in weights],
        out_shape=[tok_t_shape(ATT_Q_W), tok_shape(ATT_KV_W), tok_t_shape(ATT_KV_W), tok_shape(RET_QK_W),
                   tok_t_shape(RET_QK_W), tok_shape(RET_V_W),
                   tok_shape(RET_V_W), tok_shape(D_MODEL), tok_shape(D_MODEL)] + [w[2] for w in weights],
        compiler_params=pltpu.CompilerParams(
            dimension_semantics=("arbitrary", "arbitrary"), vmem_limit_bytes=V7X_VMEM_LIMIT_BYTES),
        name="proj",
    )(x, w_in, b_gates, *tables, w_att_up, w_ret_up, w_out)


def _mix_kernel(cdec_ref,
                qat_ref, ka_ref, vat_ref, qdr_ref, kdt_ref, vr_ref, sg_ref, ga_ref, gb_ref, x_ref,
                sink_ref, gn_ref, tril_ref, watt_ref, wret_ref, wout_ref, w1_ref, w2_ref,
                h_ref, w1b_ref, w2b_ref,
                kext_ref, vext_ref, state_ref, att_ref, ret_ref, merged_ref, *, tiles_per_seq, ntiles):
    t = pl.program_id(0)
    w1b_ref[...] = w1_ref[...].astype(BF16)
    w2b_ref[...] = w2_ref[...].astype(BF16)
    ws = t & 1
    rs = 1 - ws
    nblk = x_ref.shape[0] // BLOCK
    first = (jnp.minimum(t, ntiles - 1) % tiles_per_seq) == 0

    @pl.when(first)
    def _():
        kext_ref[:BLOCK, :] = jnp.zeros((BLOCK, ATT_KV_W), BF16)
        vext_ref[0] = jnp.zeros((ATT_KV_W, BLOCK), BF16)
        state_ref[...] = jnp.zeros_like(state_ref)

    kext_ref[BLOCK:, :] = ka_ref[...]
    vext_ref[1:] = vat_ref[...]

    from_prev = (lax.broadcasted_iota(jnp.int32, (BLOCK, BLOCK), 0)
                 > lax.broadcasted_iota(jnp.int32, (BLOCK, BLOCK), 1))

    def band_select(a, b):
        pick = lambda v, g: v[:, g * BLOCK:(g + 1) * BLOCK] if hasattr(v, "shape") else v
        return jnp.concatenate([jnp.where(from_prev, pick(a, g), pick(b, g)) for g in range(ATT_GROUP)], axis=1)
    zeros_q = jnp.zeros((ATT_HEAD_DIM, ATT_GROUP * BLOCK), BF16)

    ksl = [slice(h * RET_KEY_DIM, (h + 1) * RET_KEY_DIM) for h in range(RET_HEADS)]
    vsl = [slice(h * RET_VAL_DIM, (h + 1) * RET_VAL_DIM) for h in range(RET_HEADS)]
    rows_of = lambda c: pl.ds(c * BLOCK, BLOCK)

    def first_matmuls(c):
        rows = rows_of(c)
        qt = qat_ref[c]
        kc = kext_ref[pl.ds(c * BLOCK, 2 * BLOCK), :]
        scores = []
        for kv in range(ATT_KV_HEADS):
            qg = jnp.concatenate(
                [qt[(kv * ATT_GROUP + g) * ATT_HEAD_DIM:(kv * ATT_GROUP + g + 1) * ATT_HEAD_DIM] for g in range(ATT_GROUP)],
                axis=1)
            rhs = jnp.concatenate([qg, zeros_q] if kv == 0 else [zeros_q, qg], axis=0)
            scores.append(jnp.dot(kc, rhs, preferred_element_type=F32))
        zeros_k = jnp.zeros((RET_KEY_DIM, BLOCK), BF16)
        att = []
        for h in range(0, RET_HEADS, 2):
            pair = slice(h * RET_KEY_DIM, (h + 2) * RET_KEY_DIM)
            kdiag = jnp.concatenate([jnp.concatenate([kdt_ref[c, ksl[h], :], zeros_k], axis=1),
                                     jnp.concatenate([zeros_k, kdt_ref[c, ksl[h + 1], :]], axis=1)], axis=0)
            both = jnp.dot(qdr_ref[rows, pair], kdiag, preferred_element_type=F32)
            att += [both[:, :BLOCK], both[:, BLOCK:]]
        return scores, att

    def softmax_and_decay(c, stage1, state):
        scores, att = stage1
        probs, inv_den = [], []
        for kv in range(ATT_KV_HEADS):
            sc = scores[kv]
            prev = sc[:BLOCK]
            if c == 0:
                prev = prev + jnp.where(first, NEG_INF, 0.0).astype(F32)
            f = band_select(prev, sc[BLOCK:])
            sink = sink_ref[kv]
            m = jnp.maximum(jnp.max(f, axis=0, keepdims=True), sink)
            e = jnp.exp2(f - m)
            inv_den.append(1.0 / (jnp.sum(e, axis=0, keepdims=True) + jnp.exp2(sink - m)))
            probs.append(jnp.concatenate([band_select(e, 0.0), band_select(0.0, e)], axis=0).astype(BF16))
        att_b = [(att[h] * tril_ref[h]).astype(BF16) for h in range(RET_HEADS)]
        state_b = [state[h].astype(BF16) for h in range(RET_HEADS)]
        return probs, inv_den, att_b, state_b

    def second_matmuls(c, ops):
        probs, inv_den, att_b, state_b = ops
        rows = rows_of(c)
        vt_prev, vt_cur = vext_ref[c], vext_ref[c + 1]
        outs = []
        for kv in range(ATT_KV_HEADS):
            hs = slice(kv * ATT_HEAD_DIM, (kv + 1) * ATT_HEAD_DIM)
            vt = jnp.concatenate([vt_prev[hs], vt_cur[hs]], axis=1)
            outs.append(jnp.dot(vt, probs[kv], preferred_element_type=F32) * inv_den[kv])
        zeros_k = jnp.zeros((RET_KEY_DIM, RET_KEY_DIM), BF16)
        both = [jnp.dot(jnp.concatenate([jnp.concatenate([att_b[h], qdr_ref[rows, ksl[h]]], axis=1),
                                         jnp.concatenate([kdt_ref[c, ksl[h], :], zeros_k], axis=1)], axis=0),
                        jnp.concatenate([vr_ref[rows, vsl[h]], state_b[h]], axis=0), preferred_element_type=F32)
                for h in range(RET_HEADS)]
        ret = [bh[:BLOCK] for bh in both]
        upd = [bh[BLOCK:] for bh in both]
        return outs, ret, upd

    def finish(c, stage2):
        outs, ret, _ = stage2
        rows = rows_of(c)
        ot = jnp.concatenate(outs, axis=0)
        for g in range(ATT_GROUP):
            att_ref[ws, rows, g * LANES:(g + 1) * LANES] = ot[:, g * BLOCK:(g + 1) * BLOCK].T.astype(BF16)
        for h in range(RET_HEADS):
            out = ret[h]
            mu = jnp.mean(out, axis=-1, keepdims=True)
            dev = out - mu
            var = jnp.mean(dev * dev, axis=-1, keepdims=True)
            yn = dev * lax.rsqrt(var + GN_EPS) * gn_ref[:, vsl[h]]
            ret_ref[ws, rows, vsl[h]] = yn.astype(BF16) * sg_ref[rows, vsl[h]]

    def chunk_tail(c, stage1, state):
        stage2 = second_matmuls(c, softmax_and_decay(c, stage1, state))
        finish(c, stage2)
        return [state[h] * cdec_ref[h] + stage2[2][h] for h in range(RET_HEADS)]

    half = D_MODEL // 2
    cols = [slice(0, half), slice(half, D_MODEL)]

    assert nblk == 4

    def merged_half(n):
        ya = jnp.dot(att_ref[rs], watt_ref[:, cols[n]], preferred_element_type=F32)
        yr = jnp.dot(ret_ref[rs], wret_ref[:, cols[n]], preferred_element_type=F32)
        merged_ref[:, cols[n]] = ga_ref[:, cols[n]] * ya.astype(BF16) + gb_ref[:, cols[n]] * yr.astype(BF16)

    def out_half(n):
        h_ref[:, cols[n]] = x_ref[:, cols[n]] + jnp.dot(merged_ref[...], wout_ref[:, cols[n]],
                                                        preferred_element_type=F32)

    def run(chunks, projections):
        state = [state_ref[h] for h in range(RET_HEADS)]
        first_of = lambda c: first_matmuls(c) if chunks else None
        tail = lambda c, stage1, state: chunk_tail(c, stage1, state) if chunks else state
        s0 = first_of(0)
        s1 = first_of(1)
        if projections:
            merged_half(0)
        state = tail(0, s0, state)
        s2 = first_of(2)
        if projections:
            merged_half(1)
        state = tail(1, s1, state)
        s3 = first_of(3)
        state = tail(2, s2, state)
        if projections:
            out_half(0)
        state = tail(3, s3, state)
        if projections:
            out_half(1)
        if chunks:
            for h in range(RET_HEADS):
                state_ref[h] = state[h]
            kext_ref[:BLOCK, :] = kext_ref[nblk * BLOCK:, :]
            vext_ref[0] = vext_ref[nblk]

    pl.when(t == 0)(lambda: run(True, False))
    pl.when((t > 0) & (t < ntiles))(lambda: run(True, True))
    pl.when(t == ntiles)(lambda: run(False, True))


def _mix_call(proj_out, x, cdec, sinks, gn_gain, tril, w_att_up, w_ret_up, w_out, w_ff1, w_ff2):
    b, s, d = x.shape
    rows = MIX_ROWS
    nblk = rows // BLOCK
    tps = s // rows
    ntiles = b * tps
    grid = (ntiles + 1,)

    def weight_slab(w):
        slab = _slab_rows(w.shape[0], ntiles)
        return pl.BlockSpec((slab, w.shape[1]), lambda t: (jnp.minimum(t, w.shape[0] // slab - 1), 0))

    chunk_tile = lambda t: jnp.minimum(t, ntiles - 1)
    proj_tile = lambda t: jnp.maximum(t - 1, 0)

    def tok(width, tile):
        return pl.BlockSpec((None, rows, width), lambda t: (tile(t) // tps, tile(t) % tps, 0))

    def tok_t(feat):
        return pl.BlockSpec((None, nblk, feat, BLOCK), lambda t: (chunk_tile(t) // tps, chunk_tile(t) % tps, 0, 0))

    def const(shape):
        return pl.BlockSpec(shape, lambda t: (0,) * len(shape), pipeline_mode=pl.Buffered(1))

    smem = pl.BlockSpec(memory_space=pltpu.SMEM)
    ctok = lambda width: tok(width, chunk_tile)
    return pl.pallas_call(
        functools.partial(_mix_kernel, tiles_per_seq=tps, ntiles=ntiles),
        grid=grid,
        in_specs=[smem,
                  tok_t(ATT_Q_W), ctok(ATT_KV_W), tok_t(ATT_KV_W), ctok(RET_QK_W),
                  tok_t(RET_QK_W), ctok(RET_V_W), ctok(RET_V_W),
                  tok(D_MODEL, proj_tile), tok(D_MODEL, proj_tile), tok(d, proj_tile),
                  const((ATT_KV_HEADS, 1, ATT_GROUP * BLOCK)), const((1, RET_V_W)),
                  const((RET_HEADS, RET_CHUNK, RET_CHUNK)),
                  const((ATT_Q_W, d)), const((RET_V_W, d)), const((d, d)),
                  weight_slab(w_ff1), weight_slab(w_ff2)],
        out_specs=[tok(d, proj_tile), weight_slab(w_ff1), weight_slab(w_ff2)],
        out_shape=[jax.ShapeDtypeStruct((b, s, d), F32),
                   jax.ShapeDtypeStruct(w_ff1.shape, BF16), jax.ShapeDtypeStruct(w_ff2.shape, BF16)],
        scratch_shapes=[
            pltpu.VMEM((BLOCK + rows, ATT_KV_W), BF16),
            pltpu.VMEM((1 + nblk, ATT_KV_W, BLOCK), BF16),
            pltpu.VMEM((RET_HEADS, RET_KEY_DIM, RET_VAL_DIM), F32),
            pltpu.VMEM((2, rows, ATT_Q_W), BF16),
            pltpu.VMEM((2, rows, RET_V_W), BF16),
            pltpu.VMEM((rows, d), BF16),
        ],
        compiler_params=pltpu.CompilerParams(
            dimension_semantics=("arbitrary",), vmem_limit_bytes=V7X_VMEM_LIMIT_BYTES),
        name="mix",
    )(cdec, *proj_out, x, sinks, gn_gain, tril, w_att_up, w_ret_up, w_out, w_ff1, w_ff2)


def _ffn_kernel(h_ref, gain_ref, w1_ref, w2_ref, fgain_ref, o_ref, *, final_norm):
    nchunks = D_FF // FFN_COLS
    nsub = h_ref.shape[0] // FFN_SUB_ROWS
    assert nchunks >= 2 * FFN_SKEW + 2
    rows_of = lambda s: slice(s * FFN_SUB_ROWS, (s + 1) * FFN_SUB_ROWS)

    def start(s):
        h = h_ref[rows_of(s), :]
        return h, _rms_scale(h, gain_ref[...]).astype(BF16), jnp.zeros(h.shape, F32)

    def chunk(xb, acc, c):
        sl = slice(c * FFN_COLS, (c + 1) * FFN_COLS)
        a = jnp.maximum(jnp.dot(xb, w1_ref[:, sl], preferred_element_type=F32), 0.0)
        return acc + jnp.dot((a * a).astype(BF16), w2_ref[sl, :], preferred_element_type=F32)

    def finish(s, h, acc):
        y = h + acc
        if final_norm:
            y = _rms_scale(y, fgain_ref[...])
        o_ref[rows_of(s), :] = y

    live = {}

    def run_chunk(s, c):
        h, xb, acc = live[s]
        live[s] = (h, xb, chunk(xb, acc, c))

    for s in range(nsub):
        live[s] = start(s)
        for c in range(nchunks - (FFN_SKEW if s < nsub - 1 else 0)):
            run_chunk(s, c)
            if s > 0 and c < FFN_SKEW:
                run_chunk(s - 1, nchunks - FFN_SKEW + c)
            if s > 0 and c == FFN_SKEW + 1:
                h, _, acc = live.pop(s - 1)
                finish(s - 1, h, acc)
    h, _, acc = live.pop(nsub - 1)
    finish(nsub - 1, h, acc)


def _ffn_call(h, gain, w1, w2, fgain, final_norm):
    b, s, d = h.shape
    rows = FFN_ROWS
    grid = (b, s // rows)

    def const(shape):
        return pl.BlockSpec(shape, lambda i, j: (0,) * len(shape), pipeline_mode=pl.Buffered(1))

    tok = pl.BlockSpec((None, rows, d), lambda i, j: (i, j, 0))
    return pl.pallas_call(
        functools.partial(_ffn_kernel, final_norm=final_norm),
        grid=grid,
        in_specs=[tok, const((1, d)), const((d, D_FF)), const((D_FF, d)), const((1, d))],
        out_specs=tok,
        out_shape=jax.ShapeDtypeStruct((b, s, d), F32),
        compiler_params=pltpu.CompilerParams(
            dimension_semantics=("arbitrary", "arbitrary"), vmem_limit_bytes=V7X_VMEM_LIMIT_BYTES),
        name="ffn",
    )(h, gain, w1, w2, fgain)


def _decay_terms():
    h, c = RET_HEADS, RET_CHUNK
    f32 = np.float32
    log_gamma = np.log1p(-np.exp2(f32(-5.0) - np.arange(h, dtype=f32)))
    idx = np.arange(c, dtype=f32)
    diff = idx[:, None] - idx[None, :]
    tril = np.where(diff >= 0, np.exp(f32(-c) * log_gamma)[:, None, None], f32(0.0)).astype(f32)
    q_decay = np.exp((idx + f32(1.0))[None, :] * log_gamma[:, None])
    k_decay = np.exp((f32(c - 1.0) - idx)[None, :] * log_gamma[:, None])
    chunk_decay = np.exp(f32(c) * log_gamma)
    return tril, q_decay, k_decay, chunk_decay


def _position_tables(seq_len, q_decay, k_decay):
    f32 = np.float32
    pos = np.arange(seq_len, dtype=f32)

    def cos_sin(dim, theta):
        inv_freq = f32(theta) ** (-np.arange(0, dim, 2, dtype=f32) / f32(dim))
        ang = pos[:, None] * inv_freq[None, :]
        return np.cos(ang), np.sin(ang)

    cos, sin = cos_sin(ROPE_DIM, ROPE_THETA)
    pad = ATT_HEAD_DIM - ROPE_DIM
    zeros_half = np.zeros_like(sin)
    one_head = lambda parts: np.tile(np.concatenate(parts, axis=-1), (1, LANES // ATT_HEAD_DIM))
    ca = one_head([cos, cos, np.ones((seq_len, pad), f32)])
    s1 = one_head([-sin, zeros_half, np.zeros((seq_len, pad), f32)])
    s2 = one_head([zeros_half, sin, np.zeros((seq_len, pad), f32)])
    cos_r, sin_r = cos_sin(RET_KEY_DIM, RET_ROT_BASE)
    cr = np.concatenate([cos_r, cos_r], axis=-1)
    sr = np.concatenate([-sin_r, sin_r], axis=-1)

    def per_row(decay):
        return np.repeat(decay.T, RET_KEY_DIM, axis=1)

    tables = (cos.T, sin.T, ca, s1, s2, cr, sr, per_row(q_decay), per_row(k_decay))
    return tuple(np.ascontiguousarray(t, dtype=f32) for t in tables)


def kernel(x, norm_mix_gain, w_in, b_gates, attn_sinks, ret_gn_gain, w_att_up, w_ret_up, w_out,
           norm_mlp_gain, w_ff1, w_ff2, norm_final_gain):
    b, s, d = x.shape
    depth = w_in.shape[0]
    assert d == D_MODEL and s % MIX_ROWS == 0 and s % PROJ_ROWS == 0 and s % FFN_ROWS == 0
    tril, q_decay, k_decay, cdec = _decay_terms()
    tables = _position_tables(s, q_decay, k_decay)
    row = lambda v: v.reshape(1, -1).astype(F32)
    h = x
    for l in range(depth):
        w = (w_in[l] * norm_mix_gain[l].astype(F32)[:, None]).astype(BF16)
        *proj_out, w_att, w_ret, w_o = _proj_call(h, w, row(b_gates[l]), tables, w_att_up[l], w_ret_up[l], w_out[l])
        sinks = jnp.repeat((attn_sinks[l].astype(F32) * LOG2E).reshape(ATT_KV_HEADS, 1, ATT_GROUP), BLOCK, axis=-1)
        h, w1, w2 = _mix_call(proj_out, h, cdec, sinks, row(ret_gn_gain[l]), tril, w_att, w_ret, w_o,
                              w_ff1[l], w_ff2[l])
        h = _ffn_call(h, row(norm_mlp_gain[l]), w1, w2, row(norm_final_gain), final_norm=(l == depth - 1))
    return h
```

```python
import functools

import jax
import jax.numpy as jnp
import numpy as np
from jax import lax
from jax.experimental import pallas as pl
from jax.experimental.pallas import tpu as pltpu

D_MODEL = 1024
ATT_HEADS = 8
ATT_KV_HEADS = 2
ATT_GROUP = ATT_HEADS // ATT_KV_HEADS
ATT_HEAD_DIM = 64
WINDOW = 128
ATT_BLOCK = 128
ROPE_DIM = ATT_HEAD_DIM // 4
ROPE_HALF = ROPE_DIM // 2
ROPE_THETA = 500000.0
RET_HEADS = 4
RET_KEY_DIM = 128
RET_VAL_DIM = 256
RET_CHUNK = 128
RET_ROT_BASE = 10000.0
D_FF = 4 * D_MODEL
NORM_EPS = 1e-6
GN_EPS = 1e-6
NEG_INF = -1e30
LOG2E = 1.4426950408889634

ATT_Q_W = ATT_HEADS * ATT_HEAD_DIM
ATT_KV_W = ATT_KV_HEADS * ATT_HEAD_DIM
RET_QK_W = RET_HEADS * RET_KEY_DIM
RET_V_W = RET_HEADS * RET_VAL_DIM
IN_SPLITS = (ATT_Q_W, ATT_KV_W, ATT_KV_W, RET_QK_W, RET_QK_W, RET_V_W, RET_V_W, D_MODEL, D_MODEL)
IN_WIDTH = sum(IN_SPLITS)
(OFF_QA, OFF_KA, OFF_VA, OFF_QR, OFF_KR, OFF_VR, OFF_GR, OFF_GA, OFF_GB) = (
    int(v) for v in np.concatenate([[0], np.cumsum(IN_SPLITS)[:-1]]))

LANES = 128
SUBLANES = 8
BF16_SUBLANES = 16
V7X_VMEM_LIMIT_BYTES = 56 * 1024 * 1024

BLOCK = 128
PROJ_ROWS = 1024
PROJ_COLS = 512
MIX_ROWS = 512
FFN_ROWS = 1024
FFN_SUB_ROWS = 512
FFN_COLS = 512
FFN_SKEW = 2

BF16 = jnp.bfloat16
F32 = jnp.float32

assert WINDOW == ATT_BLOCK == RET_CHUNK == BLOCK == LANES
assert RET_KEY_DIM == LANES and ATT_KV_W == LANES and ROPE_HALF == SUBLANES
assert PROJ_ROWS % BLOCK == 0 and MIX_ROWS % BLOCK == 0


def _sigmoid_of_twice(hx):
    return 0.5 * jnp.tanh(hx) + 0.5


def _silu_of_twice(hx):
    return hx * jnp.tanh(hx) + hx


def _inv_rms(x):
    return lax.rsqrt(jnp.mean(x * x, axis=-1, keepdims=True) + NORM_EPS)


def _rms_scale(x, gain):
    return (x * _inv_rms(x)) * gain


def _proj_kernel(x_ref, w_ref, bg_ref,
                 cat_ref, sat_ref, ca_ref, s1_ref, s2_ref, cr_ref, sr_ref, qdec_ref, kdec_ref,
                 watt_ref, wret_ref, wout_ref,
                 qat_ref, ka_ref, vat_ref, qdr_ref, kdt_ref, vr_ref, sg_ref, ga_ref, gb_ref,
                 wattb_ref, wretb_ref, woutb_ref):
    rows = x_ref.shape[0]
    nblk = rows // BLOCK
    wattb_ref[...] = watt_ref[...].astype(BF16)
    wretb_ref[...] = wret_ref[...].astype(BF16)
    woutb_ref[...] = wout_ref[...].astype(BF16)
    x = x_ref[...]
    xb = x.astype(BF16)
    inv = _inv_rms(x)

    def proj(off, width):
        return jnp.dot(xb, w_ref[:, off:off + width], preferred_element_type=F32) * inv

    def store_blocks(ref, feat, val):
        for c in range(nblk):
            ref[c, feat, :] = val[:, c * BLOCK:(c + 1) * BLOCK].astype(BF16)

    q_inv = inv * (ATT_HEAD_DIM ** -0.5 * LOG2E)
    zt = (jnp.dot(xb, w_ref[:, OFF_QA:OFF_QA + ATT_Q_W], preferred_element_type=F32) * q_inv).T
    cat, sat = cat_ref[...], sat_ref[...]
    for hd in range(ATT_HEADS):
        b0 = hd * ATT_HEAD_DIM
        x1, x2 = zt[b0:b0 + ROPE_HALF], zt[b0 + ROPE_HALF:b0 + ROPE_DIM]
        head = jnp.concatenate([x1 * cat - x2 * sat, x2 * cat + x1 * sat, zt[b0 + ROPE_DIM:b0 + ATT_HEAD_DIM]], axis=0)
        store_blocks(qat_ref, slice(b0, b0 + ATT_HEAD_DIM), head)
    zkv = proj(OFF_KA, 2 * ATT_KV_W)
    store_blocks(vat_ref, slice(None), zkv[:, ATT_KV_W:].T)
    z = zkv[:, :ATT_KV_W]
    ka_ref[...] = (z * ca_ref[...] + pltpu.roll(z, LANES - ROPE_HALF, 1) * s1_ref[...]
                   + pltpu.roll(z, ROPE_HALF, 1) * s2_ref[...]).astype(BF16)

    cr, sr = cr_ref[...], sr_ref[...]
    every_block = lambda tab: jnp.concatenate([tab] * nblk, axis=0)

    def rope_ret(zh):
        return zh * cr + pltpu.roll(zh, RET_KEY_DIM // 2, 1) * sr

    z = proj(OFF_QR, RET_QK_W)
    for h in range(RET_HEADS):
        sl = slice(h * RET_KEY_DIM, (h + 1) * RET_KEY_DIM)
        qdr_ref[:, sl] = (rope_ret(z[:, sl]) * every_block(qdec_ref[:, sl])).astype(BF16)
    z = proj(OFF_KR, RET_QK_W)
    for h in range(RET_HEADS):
        sl = slice(h * RET_KEY_DIM, (h + 1) * RET_KEY_DIM)
        store_blocks(kdt_ref, sl, (rope_ret(z[:, sl]) * every_block(kdec_ref[:, sl])).T)
    half_inv = 0.5 * inv

    def half_proj(off, width):
        return jnp.dot(xb, w_ref[:, off:off + width], preferred_element_type=F32) * half_inv

    for c in range(RET_V_W // PROJ_COLS):
        sl = slice(c * PROJ_COLS, (c + 1) * PROJ_COLS)
        sg_ref[:, sl] = _silu_of_twice(half_proj(OFF_GR + c * PROJ_COLS, PROJ_COLS)).astype(BF16)
    for c in range(D_MODEL // PROJ_COLS):
        sl = slice(c * PROJ_COLS, (c + 1) * PROJ_COLS)
        hg = half_proj(OFF_GA + c * PROJ_COLS, PROJ_COLS) + 0.5 * bg_ref[:, sl]
        ga_ref[:, sl] = _sigmoid_of_twice(hg).astype(BF16)
    for c in range(D_MODEL // PROJ_COLS):
        sl = slice(c * PROJ_COLS, (c + 1) * PROJ_COLS)
        hg = (half_proj(OFF_GB + c * PROJ_COLS, PROJ_COLS)
              + 0.5 * bg_ref[:, D_MODEL + c * PROJ_COLS:D_MODEL + (c + 1) * PROJ_COLS])
        gb_ref[:, sl] = _sigmoid_of_twice(hg).astype(BF16)
    for c in range(RET_V_W // PROJ_COLS):
        sl = slice(c * PROJ_COLS, (c + 1) * PROJ_COLS)
        vr_ref[:, sl] = proj(OFF_VR + c * PROJ_COLS, PROJ_COLS).astype(BF16)


def _slab_rows(nrows, nsteps):
    slab = max(BF16_SUBLANES, nrows // nsteps)
    assert nrows % slab == 0 and slab % BF16_SUBLANES == 0 and nrows // slab <= nsteps
    return slab


def _proj_call(x, w_in, b_gates, tables, w_att_up, w_ret_up, w_out):
    b, s, d = x.shape
    rows = PROJ_ROWS
    nblk = rows // BLOCK
    grid = (s // rows, b)

    def weight_slabs(w, out_block=lambda blk: blk):
        slab = _slab_rows(w.shape[0], grid[0] * grid[1])
        blk = lambda j, i: jnp.minimum(j * b + i, w.shape[0] // slab - 1)
        return (pl.BlockSpec((slab, w.shape[1]), lambda j, i: (blk(j, i), 0)),
                pl.BlockSpec((slab, w.shape[1]), lambda j, i: (out_block(blk(j, i)), 0)),
                jax.ShapeDtypeStruct(w.shape, BF16))

    att_slab = _slab_rows(w_att_up.shape[0], grid[0] * grid[1])
    assert ATT_HEAD_DIM % att_slab == 0
    head_slabs = ATT_HEAD_DIM // att_slab

    def att_out_block(blk):
        head, part = blk // head_slabs, blk % head_slabs
        return ((head % ATT_GROUP) * ATT_KV_HEADS + head // ATT_GROUP) * head_slabs + part

    weights = [weight_slabs(w_att_up, att_out_block), weight_slabs(w_ret_up), weight_slabs(w_out)]

    def tok(width):
        return pl.BlockSpec((None, rows, width), lambda j, i: (i, j, 0))

    def tok_t(feat):
        return pl.BlockSpec((None, nblk, feat, BLOCK), lambda j, i: (i, j, 0, 0))

    def const(shape):
        return pl.BlockSpec(shape, lambda j, i: (0,) * len(shape), pipeline_mode=pl.Buffered(1))

    pos_t = pl.BlockSpec((ROPE_HALF, rows), lambda j, i: (0, j))
    pos = pl.BlockSpec((rows, LANES), lambda j, i: (j, 0))
    tok_shape = lambda w: jax.ShapeDtypeStruct((b, s, w), BF16)
    tok_t_shape = lambda f: jax.ShapeDtypeStruct((b, s // BLOCK, f, BLOCK), BF16)
    return pl.pallas_call(
        _proj_kernel,
        grid=grid,
        in_specs=[tok(d), const((d, IN_WIDTH)),
                  const((1, 2 * D_MODEL)), pos_t, pos_t, pos, pos, pos, pos, pos,
                  const((BLOCK, RET_QK_W)), const((BLOCK, RET_QK_W))] + [w[0] for w in weights],
        out_specs=[tok_t(ATT_Q_W), tok(ATT_KV_W), tok_t(ATT_KV_W), tok(RET_QK_W),
                   tok_t(RET_QK_W), tok(RET_V_W), tok(RET_V_W), tok(D_MODEL), tok(D_MODEL)] + [w[1] for w in weights],
        out_shape=[tok_t_shape(ATT_Q_W), tok_shape(ATT_KV_W), tok_t_shape(ATT_KV_W), tok_shape(RET_QK_W),
                   tok_t_shape(RET_QK_W), tok_shape(RET_V_W),
                   tok_shape(RET_V_W), tok_shape(D_MODEL), tok_shape(D_MODEL)] + [w[2] for w in weights],
        compiler_params=pltpu.CompilerParams(
            dimension_semantics=("arbitrary", "arbitrary"), vmem_limit_bytes=V7X_VMEM_LIMIT_BYTES),
        name="proj",
    )(x, w_in, b_gates, *tables, w_att_up, w_ret_up, w_out)


def _mix_kernel(cdec_ref,
                qat_ref, ka_ref, vat_ref, qdr_ref, kdt_ref, vr_ref, sg_ref, ga_ref, gb_ref, x_ref,
                sink_ref, gn_ref, tril_ref, watt_ref, wret_ref, wout_ref, w1_ref, w2_ref,
                h_ref, w1b_ref, w2b_ref,
                kext_ref, vext_ref, state_ref, att_ref, ret_ref, merged_ref, *, tiles_per_seq, ntiles):
    t = pl.program_id(0)
    w1b_ref[...] = w1_ref[...].astype(BF16)
    w2b_ref[...] = w2_ref[...].astype(BF16)
    ws = t & 1
    rs = 1 - ws
    nblk = x_ref.shape[0] // BLOCK
    first = (jnp.minimum(t, ntiles - 1) % tiles_per_seq) == 0

    @pl.when(first)
    def _():
        kext_ref[:BLOCK, :] = jnp.zeros((BLOCK, ATT_KV_W), BF16)
        vext_ref[0] = jnp.zeros((ATT_KV_W, BLOCK), BF16)
        state_ref[...] = jnp.zeros_like(state_ref)

    kext_ref[BLOCK:, :] = ka_ref[...]
    vext_ref[1:] = vat_ref[...]

    from_prev = (lax.broadcasted_iota(jnp.int32, (BLOCK, BLOCK), 0)
                 > lax.broadcasted_iota(jnp.int32, (BLOCK, BLOCK), 1))

    def band_select(a, b):
        pick = lambda v, g: v[:, g * BLOCK:(g + 1) * BLOCK] if hasattr(v, "shape") else v
        return jnp.concatenate([jnp.where(from_prev, pick(a, g), pick(b, g)) for g in range(ATT_GROUP)], axis=1)
    zeros_q = jnp.zeros((ATT_HEAD_DIM, ATT_GROUP * BLOCK), BF16)

    ksl = [slice(h * RET_KEY_DIM, (h + 1) * RET_KEY_DIM) for h in range(RET_HEADS)]
    vsl = [slice(h * RET_VAL_DIM, (h + 1) * RET_VAL_DIM) for h in range(RET_HEADS)]
    rows_of = lambda c: pl.ds(c * BLOCK, BLOCK)

    def first_matmuls(c):
        rows = rows_of(c)
        qt = qat_ref[c]
        kc = kext_ref[pl.ds(c * BLOCK, 2 * BLOCK), :]
        scores = []
        for kv in range(ATT_KV_HEADS):
            qg = jnp.concatenate(
                [qt[(kv * ATT_GROUP + g) * ATT_HEAD_DIM:(kv * ATT_GROUP + g + 1) * ATT_HEAD_DIM] for g in range(ATT_GROUP)],
                axis=1)
            rhs = jnp.concatenate([qg, zeros_q] if kv == 0 else [zeros_q, qg], axis=0)
            scores.append(jnp.dot(kc, rhs, preferred_element_type=F32))
        zeros_k = jnp.zeros((RET_KEY_DIM, BLOCK), BF16)
        att = []
        for h in range(0, RET_HEADS, 2):
            pair = slice(h * RET_KEY_DIM, (h + 2) * RET_KEY_DIM)
            kdiag = jnp.concatenate([jnp.concatenate([kdt_ref[c, ksl[h], :], zeros_k], axis=1),
                                     jnp.concatenate([zeros_k, kdt_ref[c, ksl[h + 1], :]], axis=1)], axis=0)
            both = jnp.dot(qdr_ref[rows, pair], kdiag, preferred_element_type=F32)
            att += [both[:, :BLOCK], both[:, BLOCK:]]
        return scores, att

    def softmax_and_decay(c, stage1, state):
        scores, att = stage1
        probs, inv_den = [], []
        for kv in range(ATT_KV_HEADS):
            sc = scores[kv]
            prev = sc[:BLOCK]
            if c == 0:
                prev = prev + jnp.where(first, NEG_INF, 0.0).astype(F32)
            f = band_select(prev, sc[BLOCK:])
            sink = sink_ref[kv]
            m = jnp.maximum(jnp.max(f, axis=0, keepdims=True), sink)
            e = jnp.exp2(f - m)
            inv_den.append(1.0 / (jnp.sum(e, axis=0, keepdims=True) + jnp.exp2(sink - m)))
            probs.append(jnp.concatenate([band_select(e, 0.0), band_select(0.0, e)], axis=0).astype(BF16))
        att_b = [(att[h] * tril_ref[h]).astype(BF16) for h in range(RET_HEADS)]
        state_b = [state[h].astype(BF16) for h in range(RET_HEADS)]
        return probs, inv_den, att_b, state_b

    def second_matmuls(c, ops):
        probs, inv_den, att_b, state_b = ops
        rows = rows_of(c)
        vt_prev, vt_cur = vext_ref[c], vext_ref[c + 1]
        outs = []
        for kv in range(ATT_KV_HEADS):
            hs = slice(kv * ATT_HEAD_DIM, (kv + 1) * ATT_HEAD_DIM)
            vt = jnp.concatenate([vt_prev[hs], vt_cur[hs]], axis=1)
            outs.append(jnp.dot(vt, probs[kv], preferred_element_type=F32) * inv_den[kv])
        zeros_k = jnp.zeros((RET_KEY_DIM, RET_KEY_DIM), BF16)
        both = [jnp.dot(jnp.concatenate([jnp.concatenate([att_b[h], qdr_ref[rows, ksl[h]]], axis=1),
                                         jnp.concatenate([kdt_ref[c, ksl[h], :], zeros_k], axis=1)], axis=0),
                        jnp.concatenate([vr_ref[rows, vsl[h]], state_b[h]], axis=0), preferred_element_type=F32)
                for h in range(RET_HEADS)]
        ret = [bh[:BLOCK] for bh in both]
        upd = [bh[BLOCK:] for bh in both]
        return outs, ret, upd

    def finish(c, stage2):
        outs, ret, _ = stage2
        rows = rows_of(c)
        ot = jnp.concatenate(outs, axis=0)
        for g in range(ATT_GROUP):
            att_ref[ws, rows, g * LANES:(g + 1) * LANES] = ot[:, g * BLOCK:(g + 1) * BLOCK].T.astype(BF16)
        for h in range(RET_HEADS):
            out = ret[h]
            mu = jnp.mean(out, axis=-1, keepdims=True)
            dev = out - mu
            var = jnp.mean(dev * dev, axis=-1, keepdims=True)
            yn = dev * lax.rsqrt(var + GN_EPS) * gn_ref[:, vsl[h]]
            ret_ref[ws, rows, vsl[h]] = yn.astype(BF16) * sg_ref[rows, vsl[h]]

    def chunk_tail(c, stage1, state):
        stage2 = second_matmuls(c, softmax_and_decay(c, stage1, state))
        finish(c, stage2)
        return [state[h] * cdec_ref[h] + stage2[2][h] for h in range(RET_HEADS)]

    half = D_MODEL // 2
    cols = [slice(0, half), slice(half, D_MODEL)]

    assert nblk == 4

    def merged_half(n):
        ya = jnp.dot(att_ref[rs], watt_ref[:, cols[n]], preferred_element_type=F32)
        yr = jnp.dot(ret_ref[rs], wret_ref[:, cols[n]], preferred_element_type=F32)
        merged_ref[:, cols[n]] = ga_ref[:, cols[n]] * ya.astype(BF16) + gb_ref[:, cols[n]] * yr.astype(BF16)

    def out_half(n):
        h_ref[:, cols[n]] = x_ref[:, cols[n]] + jnp.dot(merged_ref[...], wout_ref[:, cols[n]],
                                                        preferred_element_type=F32)

    def run(chunks, projections):
        state = [state_ref[h] for h in range(RET_HEADS)]
        first_of = lambda c: first_matmuls(c) if chunks else None
        tail = lambda c, stage1, state: chunk_tail(c, stage1, state) if chunks else state
        s0 = first_of(0)
        s1 = first_of(1)
        if projections:
            merged_half(0)
        state = tail(0, s0, state)
        s2 = first_of(2)
        if projections:
            merged_half(1)
        state = tail(1, s1, state)
        s3 = first_of(3)
        state = tail(2, s2, state)
        if projections:
            out_half(0)
        state = tail(3, s3, state)
        if projections:
            out_half(1)
        if chunks:
            for h in range(RET_HEADS):
                state_ref[h] = state[h]
            kext_ref[:BLOCK, :] = kext_ref[nblk * BLOCK:, :]
            vext_ref[0] = vext_ref[nblk]

    pl.when(t == 0)(lambda: run(True, False))
    pl.when((t > 0) & (t < ntiles))(lambda: run(True, True))
    pl.when(t == ntiles)(lambda: run(False, True))


def _mix_call(proj_out, x, cdec, sinks, gn_gain, tril, w_att_up, w_ret_up, w_out, w_ff1, w_ff2):
    b, s, d = x.shape
    rows = MIX_ROWS
    nblk = rows // BLOCK
    tps = s // rows
    ntiles = b * tps
    grid = (ntiles + 1,)

    def weight_slab(w):
        slab = _slab_rows(w.shape[0], ntiles)
        return pl.BlockSpec((slab, w.shape[1]), lambda t: (jnp.minimum(t, w.shape[0] // slab - 1), 0))

    chunk_tile = lambda t: jnp.minimum(t, ntiles - 1)
    proj_tile = lambda t: jnp.maximum(t - 1, 0)

    def tok(width, tile):
        return pl.BlockSpec((None, rows, width), lambda t: (tile(t) // tps, tile(t) % tps, 0))

    def tok_t(feat):
        return pl.BlockSpec((None, nblk, feat, BLOCK), lambda t: (chunk_tile(t) // tps, chunk_tile(t) % tps, 0, 0))

    def const(shape):
        return pl.BlockSpec(shape, lambda t: (0,) * len(shape), pipeline_mode=pl.Buffered(1))

    smem = pl.BlockSpec(memory_space=pltpu.SMEM)
    ctok = lambda width: tok(width, chunk_tile)
    return pl.pallas_call(
        functools.partial(_mix_kernel, tiles_per_seq=tps, ntiles=ntiles),
        grid=grid,
        in_specs=[smem,
                  tok_t(ATT_Q_W), ctok(ATT_KV_W), tok_t(ATT_KV_W), ctok(RET_QK_W),
                  tok_t(RET_QK_W), ctok(RET_V_W), ctok(RET_V_W),
                  tok(D_MODEL, proj_tile), tok(D_MODEL, proj_tile), tok(d, proj_tile),
                  const((ATT_KV_HEADS, 1, ATT_GROUP * BLOCK)), const((1, RET_V_W)),
                  const((RET_HEADS, RET_CHUNK, RET_CHUNK)),
                  const((ATT_Q_W, d)), const((RET_V_W, d)), const((d, d)),
                  weight_slab(w_ff1), weight_slab(w_ff2)],
        out_specs=[tok(d, proj_tile), weight_slab(w_ff1), weight_slab(w_ff2)],
        out_shape=[jax.ShapeDtypeStruct((b, s, d), F32),
                   jax.ShapeDtypeStruct(w_ff1.shape, BF16), jax.ShapeDtypeStruct(w_ff2.shape, BF16)],
        scratch_shapes=[
            pltpu.VMEM((BLOCK + rows, ATT_KV_W), BF16),
            pltpu.VMEM((1 + nblk, ATT_KV_W, BLOCK), BF16),
            pltpu.VMEM((RET_HEADS, RET_KEY_DIM, RET_VAL_DIM), F32),
            pltpu.VMEM((2, rows, ATT_Q_W), BF16),
            pltpu.VMEM((2, rows, RET_V_W), BF16),
            pltpu.VMEM((rows, d), BF16),
        ],
        compiler_params=pltpu.CompilerParams(
            dimension_semantics=("arbitrary",), vmem_limit_bytes=V7X_VMEM_LIMIT_BYTES),
        name="mix",
    )(cdec, *proj_out, x, sinks, gn_gain, tril, w_att_up, w_ret_up, w_out, w_ff1, w_ff2)


def _ffn_kernel(h_ref, gain_ref, w1_ref, w2_ref, fgain_ref, o_ref, *, final_norm):
    nchunks = D_FF // FFN_COLS
    assert h_ref.shape[0] == 2 * FFN_SUB_ROWS and nchunks > FFN_SKEW
    rows_of = lambda s: slice(s * FFN_SUB_ROWS, (s + 1) * FFN_SUB_ROWS)

    def start(s):
        h = h_ref[rows_of(s), :]
        return h, _rms_scale(h, gain_ref[...]).astype(BF16), jnp.zeros(h.shape, F32)

    def chunk(xb, acc, c):
        sl = slice(c * FFN_COLS, (c + 1) * FFN_COLS)
        a = jnp.maximum(jnp.dot(xb, w1_ref[:, sl], preferred_element_type=F32), 0.0)
        return acc + jnp.dot((a * a).astype(BF16), w2_ref[sl, :], preferred_element_type=F32)

    def finish(s, h, acc):
        y = h + acc
        if final_norm:
            y = _rms_scale(y, fgain_ref[...])
        o_ref[rows_of(s), :] = y

    h0, xb0, acc0 = start(0)
    for c in range(nchunks - FFN_SKEW):
        acc0 = chunk(xb0, acc0, c)
    h1, xb1, acc1 = start(1)
    for k in range(FFN_SKEW):
        acc1 = chunk(xb1, acc1, k)
        acc0 = chunk(xb0, acc0, nchunks - FFN_SKEW + k)
    for c in range(FFN_SKEW, nchunks):
        acc1 = chunk(xb1, acc1, c)
        if c == FFN_SKEW + 1:
            finish(0, h0, acc0)
    finish(1, h1, acc1)


def _ffn_call(h, gain, w1, w2, fgain, final_norm):
    b, s, d = h.shape
    rows = FFN_ROWS
    grid = (b, s // rows)

    def const(shape):
        return pl.BlockSpec(shape, lambda i, j: (0,) * len(shape), pipeline_mode=pl.Buffered(1))

    tok = pl.BlockSpec((None, rows, d), lambda i, j: (i, j, 0))
    return pl.pallas_call(
        functools.partial(_ffn_kernel, final_norm=final_norm),
        grid=grid,
        in_specs=[tok, const((1, d)), const((d, D_FF)), const((D_FF, d)), const((1, d))],
        out_specs=tok,
        out_shape=jax.ShapeDtypeStruct((b, s, d), F32),
        compiler_params=pltpu.CompilerParams(
            dimension_semantics=("arbitrary", "arbitrary"), vmem_limit_bytes=V7X_VMEM_LIMIT_BYTES),
        name="ffn",
    )(h, gain, w1, w2, fgain)


def _decay_terms():
    h, c = RET_HEADS, RET_CHUNK
    f32 = np.float32
    log_gamma = np.log1p(-np.exp2(f32(-5.0) - np.arange(h, dtype=f32)))
    idx = np.arange(c, dtype=f32)
    diff = idx[:, None] - idx[None, :]
    tril = np.where(diff >= 0, np.exp(f32(-c) * log_gamma)[:, None, None], f32(0.0)).astype(f32)
    q_decay = np.exp((idx + f32(1.0))[None, :] * log_gamma[:, None])
    k_decay = np.exp((f32(c - 1.0) - idx)[None, :] * log_gamma[:, None])
    chunk_decay = np.exp(f32(c) * log_gamma)
    return tril, q_decay, k_decay, chunk_decay


def _position_tables(seq_len, q_decay, k_decay):
    f32 = np.float32
    pos = np.arange(seq_len, dtype=f32)

    def cos_sin(dim, theta):
        inv_freq = f32(theta) ** (-np.arange(0, dim, 2, dtype=f32) / f32(dim))
        ang = pos[:, None] * inv_freq[None, :]
        return np.cos(ang), np.sin(ang)

    cos, sin = cos_sin(ROPE_DIM, ROPE_THETA)
    pad = ATT_HEAD_DIM - ROPE_DIM
    zeros_half = np.zeros_like(sin)
    one_head = lambda parts: np.tile(np.concatenate(parts, axis=-1), (1, LANES // ATT_HEAD_DIM))
    ca = one_head([cos, cos, np.ones((seq_len, pad), f32)])
    s1 = one_head([-sin, zeros_half, np.zeros((seq_len, pad), f32)])
    s2 = one_head([zeros_half, sin, np.zeros((seq_len, pad), f32)])
    cos_r, sin_r = cos_sin(RET_KEY_DIM, RET_ROT_BASE)
    cr = np.concatenate([cos_r, cos_r], axis=-1)
    sr = np.concatenate([-sin_r, sin_r], axis=-1)

    def per_row(decay):
        return np.repeat(decay.T, RET_KEY_DIM, axis=1)

    key_scale = f32(RET_KEY_DIM ** -0.5)
    tables = (cos.T, sin.T, ca, s1, s2, cr, sr, per_row(q_decay), per_row(k_decay * key_scale))
    return tuple(np.ascontiguousarray(t, dtype=f32) for t in tables)


def kernel(x, norm_mix_gain, w_in, b_gates, attn_sinks, ret_gn_gain, w_att_up, w_ret_up, w_out,
           norm_mlp_gain, w_ff1, w_ff2, norm_final_gain):
    b, s, d = x.shape
    depth = w_in.shape[0]
    assert d == D_MODEL and s % MIX_ROWS == 0 and s % PROJ_ROWS == 0 and s % FFN_ROWS == 0
    tril, q_decay, k_decay, cdec = _decay_terms()
    tables = _position_tables(s, q_decay, k_decay)
    row = lambda v: v.reshape(1, -1).astype(F32)
    h = x
    for l in range(depth):
        w = (w_in[l] * norm_mix_gain[l].astype(F32)[:, None]).astype(BF16)
        *proj_out, w_att, w_ret, w_o = _proj_call(h, w, row(b_gates[l]), tables, w_att_up[l], w_ret_up[l], w_out[l])
        sinks = jnp.repeat((attn_sinks[l].astype(F32) * LOG2E).reshape(ATT_KV_HEADS, 1, ATT_GROUP), BLOCK, axis=-1)
        h, w1, w2 = _mix_call(proj_out, h, cdec, sinks, row(ret_gn_gain[l]), tril, w_att, w_ret, w_o,
                              w_ff1[l], w_ff2[l])
        h = _ffn_call(h, row(norm_mlp_gain[l]), w1, w2, row(norm_final_gain), final_norm=(l == depth - 1))
    return h
```

```python
import functools

import jax
import jax.numpy as jnp
import numpy as np
from jax import lax
from jax.experimental import pallas as pl
from jax.experimental.pallas import tpu as pltpu

D_MODEL = 1024
ATT_HEADS = 8
ATT_KV_HEADS = 2
ATT_GROUP = ATT_HEADS // ATT_KV_HEADS
ATT_HEAD_DIM = 64
WINDOW = 128
ATT_BLOCK = 128
ROPE_DIM = ATT_HEAD_DIM // 4
ROPE_HALF = ROPE_DIM // 2
ROPE_THETA = 500000.0
RET_HEADS = 4
RET_KEY_DIM = 128
RET_VAL_DIM = 256
RET_CHUNK = 128
RET_ROT_BASE = 10000.0
D_FF = 4 * D_MODEL
NORM_EPS = 1e-6
GN_EPS = 1e-6
NEG_INF = -1e30
LOG2E = 1.4426950408889634

ATT_Q_W = ATT_HEADS * ATT_HEAD_DIM
ATT_KV_W = ATT_KV_HEADS * ATT_HEAD_DIM
RET_QK_W = RET_HEADS * RET_KEY_DIM
RET_V_W = RET_HEADS * RET_VAL_DIM
IN_SPLITS = (ATT_Q_W, ATT_KV_W, ATT_KV_W, RET_QK_W, RET_QK_W, RET_V_W, RET_V_W, D_MODEL, D_MODEL)
IN_WIDTH = sum(IN_SPLITS)
(OFF_QA, OFF_KA, OFF_VA, OFF_QR, OFF_KR, OFF_VR, OFF_GR, OFF_GA, OFF_GB) = (
    int(v) for v in np.concatenate([[0], np.cumsum(IN_SPLITS)[:-1]]))

LANES = 128
SUBLANES = 8
BF16_SUBLANES = 16
V7X_VMEM_LIMIT_BYTES = 56 * 1024 * 1024

BLOCK = 128
PROJ_ROWS = 1024
PROJ_COLS = 512
MIX_ROWS = 512
FFN_ROWS = 1024
FFN_SUB_ROWS = 512
FFN_COLS = 512
FFN_SKEW = 2

BF16 = jnp.bfloat16
F32 = jnp.float32

assert WINDOW == ATT_BLOCK == RET_CHUNK == BLOCK == LANES
assert RET_KEY_DIM == LANES and ATT_KV_W == LANES and ROPE_HALF == SUBLANES
assert PROJ_ROWS % BLOCK == 0 and MIX_ROWS % BLOCK == 0


def _sigmoid_of_twice(hx):
    return 0.5 * jnp.tanh(hx) + 0.5


def _silu_of_twice(hx):
    return hx * jnp.tanh(hx) + hx


def _inv_rms(x):
    return lax.rsqrt(jnp.mean(x * x, axis=-1, keepdims=True) + NORM_EPS)


def _rms_scale(x, gain):
    return (x * _inv_rms(x)) * gain


def _proj_kernel(x_ref, w_ref, bg_ref,
                 cat_ref, sat_ref, ca_ref, s1_ref, s2_ref, cr_ref, sr_ref, qdec_ref, kdec_ref,
                 watt_ref, wret_ref, wout_ref,
                 qat_ref, ka_ref, vat_ref, qdr_ref, kdt_ref, vr_ref, sg_ref, ga_ref, gb_ref,
                 wattb_ref, wretb_ref, woutb_ref):
    rows = x_ref.shape[0]
    nblk = rows // BLOCK
    wattb_ref[...] = watt_ref[...].astype(BF16)
    wretb_ref[...] = wret_ref[...].astype(BF16)
    woutb_ref[...] = wout_ref[...].astype(BF16)
    x = x_ref[...]
    xb = x.astype(BF16)
    inv = _inv_rms(x)

    def proj(off, width):
        return jnp.dot(xb, w_ref[:, off:off + width], preferred_element_type=F32) * inv

    def store_blocks(ref, feat, val):
        for c in range(nblk):
            ref[c, feat, :] = val[:, c * BLOCK:(c + 1) * BLOCK].astype(BF16)

    q_inv = inv * (ATT_HEAD_DIM ** -0.5 * LOG2E)
    zt = (jnp.dot(xb, w_ref[:, OFF_QA:OFF_QA + ATT_Q_W], preferred_element_type=F32) * q_inv).T
    cat, sat = cat_ref[...], sat_ref[...]
    for hd in range(ATT_HEADS):
        b0 = hd * ATT_HEAD_DIM
        x1, x2 = zt[b0:b0 + ROPE_HALF], zt[b0 + ROPE_HALF:b0 + ROPE_DIM]
        head = jnp.concatenate([x1 * cat - x2 * sat, x2 * cat + x1 * sat, zt[b0 + ROPE_DIM:b0 + ATT_HEAD_DIM]], axis=0)
        store_blocks(qat_ref, slice(b0, b0 + ATT_HEAD_DIM), head)
    zkv = proj(OFF_KA, 2 * ATT_KV_W)
    store_blocks(vat_ref, slice(None), zkv[:, ATT_KV_W:].T)
    z = zkv[:, :ATT_KV_W]
    ka_ref[...] = (z * ca_ref[...] + pltpu.roll(z, LANES - ROPE_HALF, 1) * s1_ref[...]
                   + pltpu.roll(z, ROPE_HALF, 1) * s2_ref[...]).astype(BF16)

    cr, sr = cr_ref[...], sr_ref[...]
    every_block = lambda tab: jnp.concatenate([tab] * nblk, axis=0)

    def rope_ret(zh):
        return zh * cr + pltpu.roll(zh, RET_KEY_DIM // 2, 1) * sr

    z = proj(OFF_QR, RET_QK_W)
    for h in range(RET_HEADS):
        sl = slice(h * RET_KEY_DIM, (h + 1) * RET_KEY_DIM)
        qdr_ref[:, sl] = (rope_ret(z[:, sl]) * every_block(qdec_ref[:, sl])).astype(BF16)
    z = proj(OFF_KR, RET_QK_W)
    for h in range(RET_HEADS):
        sl = slice(h * RET_KEY_DIM, (h + 1) * RET_KEY_DIM)
        store_blocks(kdt_ref, sl, (rope_ret(z[:, sl]) * every_block(kdec_ref[:, sl])).T)
    half_inv = 0.5 * inv

    def half_proj(off, width):
        return jnp.dot(xb, w_ref[:, off:off + width], preferred_element_type=F32) * half_inv

    for c in range(RET_V_W // PROJ_COLS):
        sl = slice(c * PROJ_COLS, (c + 1) * PROJ_COLS)
        sg_ref[:, sl] = _silu_of_twice(half_proj(OFF_GR + c * PROJ_COLS, PROJ_COLS)).astype(BF16)
    for c in range(D_MODEL // PROJ_COLS):
        sl = slice(c * PROJ_COLS, (c + 1) * PROJ_COLS)
        hg = half_proj(OFF_GA + c * PROJ_COLS, PROJ_COLS) + 0.5 * bg_ref[:, sl]
        ga_ref[:, sl] = _sigmoid_of_twice(hg).astype(BF16)
    for c in range(D_MODEL // PROJ_COLS):
        sl = slice(c * PROJ_COLS, (c + 1) * PROJ_COLS)
        hg = (half_proj(OFF_GB + c * PROJ_COLS, PROJ_COLS)
              + 0.5 * bg_ref[:, D_MODEL + c * PROJ_COLS:D_MODEL + (c + 1) * PROJ_COLS])
        gb_ref[:, sl] = _sigmoid_of_twice(hg).astype(BF16)
    for c in range(RET_V_W // PROJ_COLS):
        sl = slice(c * PROJ_COLS, (c + 1) * PROJ_COLS)
        vr_ref[:, sl] = proj(OFF_VR + c * PROJ_COLS, PROJ_COLS).astype(BF16)


def _slab_rows(nrows, nsteps):
    slab = max(BF16_SUBLANES, nrows // nsteps)
    assert nrows % slab == 0 and slab % BF16_SUBLANES == 0 and nrows // slab <= nsteps
    return slab


def _proj_call(x, w_in, b_gates, tables, w_att_up, w_ret_up, w_out):
    b, s, d = x.shape
    rows = PROJ_ROWS
    nblk = rows // BLOCK
    grid = (s // rows, b)

    def weight_slabs(w, out_block=lambda blk: blk):
        slab = _slab_rows(w.shape[0], grid[0] * grid[1])
        blk = lambda j, i: jnp.minimum(j * b + i, w.shape[0] // slab - 1)
        return (pl.BlockSpec((slab, w.shape[1]), lambda j, i: (blk(j, i), 0)),
                pl.BlockSpec((slab, w.shape[1]), lambda j, i: (out_block(blk(j, i)), 0)),
                jax.ShapeDtypeStruct(w.shape, BF16))

    att_slab = _slab_rows(w_att_up.shape[0], grid[0] * grid[1])
    assert ATT_HEAD_DIM % att_slab == 0
    head_slabs = ATT_HEAD_DIM // att_slab

    def att_out_block(blk):
        head, part = blk // head_slabs, blk % head_slabs
        return ((head % ATT_GROUP) * ATT_KV_HEADS + head // ATT_GROUP) * head_slabs + part

    weights = [weight_slabs(w_att_up, att_out_block), weight_slabs(w_ret_up), weight_slabs(w_out)]

    def tok(width):
        return pl.BlockSpec((None, rows, width), lambda j, i: (i, j, 0))

    def tok_t(feat):
        return pl.BlockSpec((None, nblk, feat, BLOCK), lambda j, i: (i, j, 0, 0))

    def const(shape):
        return pl.BlockSpec(shape, lambda j, i: (0,) * len(shape), pipeline_mode=pl.Buffered(1))

    pos_t = pl.BlockSpec((ROPE_HALF, rows), lambda j, i: (0, j))
    pos = pl.BlockSpec((rows, LANES), lambda j, i: (j, 0))
    tok_shape = lambda w: jax.ShapeDtypeStruct((b, s, w), BF16)
    tok_t_shape = lambda f: jax.ShapeDtypeStruct((b, s // BLOCK, f, BLOCK), BF16)
    return pl.pallas_call(
        _proj_kernel,
        grid=grid,
        in_specs=[tok(d), const((d, IN_WIDTH)),
                  const((1, 2 * D_MODEL)), pos_t, pos_t, pos, pos, pos, pos, pos,
                  const((BLOCK, RET_QK_W)), const((BLOCK, RET_QK_W))] + [w[0] for w in weights],
        out_specs=[tok_t(ATT_Q_W), tok(ATT_KV_W), tok_t(ATT_KV_W), tok(RET_QK_W),
                   tok_t(RET_QK_W), tok(RET_V_W), tok(RET_V_W), tok(D_MODEL), tok(D_MODEL)] + [w[1] for w in weights],
        out_shape=[tok_t_shape(ATT_Q_W), tok_shape(ATT_KV_W), tok_t_shape(ATT_KV_W), tok_shape(RET_QK_W),
                   tok_t_shape(RET_QK_W), tok_shape(RET_V_W),
                   tok_shape(RET_V_W), tok_shape(D_MODEL), tok_shape(D_MODEL)] + [w[2] for w in weights],
        compiler_params=pltpu.CompilerParams(
            dimension_semantics=("arbitrary", "arbitrary"), vmem_limit_bytes=V7X_VMEM_LIMIT_BYTES),
        name="proj",
    )(x, w_in, b_gates, *tables, w_att_up, w_ret_up, w_out)


def _mix_kernel(cdec_ref,
                qat_ref, ka_ref, vat_ref, qdr_ref, kdt_ref, vr_ref, sg_ref, ga_ref, gb_ref, x_ref,
                sink_ref, gn_ref, tril_ref, watt_ref, wret_ref, wout_ref, w1_ref, w2_ref,
                h_ref, w1b_ref, w2b_ref,
                kext_ref, vext_ref, state_ref, att_ref, ret_ref, merged_ref, *, tiles_per_seq, ntiles):
    t = pl.program_id(0)
    w1b_ref[...] = w1_ref[...].astype(BF16)
    w2b_ref[...] = w2_ref[...].astype(BF16)
    ws = t & 1
    rs = 1 - ws
    nblk = x_ref.shape[0] // BLOCK
    first = (jnp.minimum(t, ntiles - 1) % tiles_per_seq) == 0

    @pl.when(first)
    def _():
        kext_ref[:BLOCK, :] = jnp.zeros((BLOCK, ATT_KV_W), BF16)
        vext_ref[0] = jnp.zeros((ATT_KV_W, BLOCK), BF16)
        state_ref[...] = jnp.zeros_like(state_ref)

    kext_ref[BLOCK:, :] = ka_ref[...]
    vext_ref[1:] = vat_ref[...]

    from_prev = (lax.broadcasted_iota(jnp.int32, (BLOCK, BLOCK), 0)
                 > lax.broadcasted_iota(jnp.int32, (BLOCK, BLOCK), 1))

    def band_select(a, b):
        pick = lambda v, g: v[:, g * BLOCK:(g + 1) * BLOCK] if hasattr(v, "shape") else v
        return jnp.concatenate([jnp.where(from_prev, pick(a, g), pick(b, g)) for g in range(ATT_GROUP)], axis=1)
    zeros_q = jnp.zeros((ATT_HEAD_DIM, ATT_GROUP * BLOCK), BF16)

    ksl = [slice(h * RET_KEY_DIM, (h + 1) * RET_KEY_DIM) for h in range(RET_HEADS)]
    vsl = [slice(h * RET_VAL_DIM, (h + 1) * RET_VAL_DIM) for h in range(RET_HEADS)]
    rows_of = lambda c: pl.ds(c * BLOCK, BLOCK)

    def first_matmuls(c):
        rows = rows_of(c)
        qt = qat_ref[c]
        kc = kext_ref[pl.ds(c * BLOCK, 2 * BLOCK), :]
        scores = []
        for kv in range(ATT_KV_HEADS):
            qg = jnp.concatenate(
                [qt[(kv * ATT_GROUP + g) * ATT_HEAD_DIM:(kv * ATT_GROUP + g + 1) * ATT_HEAD_DIM] for g in range(ATT_GROUP)],
                axis=1)
            rhs = jnp.concatenate([qg, zeros_q] if kv == 0 else [zeros_q, qg], axis=0)
            scores.append(jnp.dot(kc, rhs, preferred_element_type=F32))
        zeros_k = jnp.zeros((RET_KEY_DIM, BLOCK), BF16)
        att = []
        for h in range(0, RET_HEADS, 2):
            pair = slice(h * RET_KEY_DIM, (h + 2) * RET_KEY_DIM)
            kdiag = jnp.concatenate([jnp.concatenate([kdt_ref[c, ksl[h], :], zeros_k], axis=1),
                                     jnp.concatenate([zeros_k, kdt_ref[c, ksl[h + 1], :]], axis=1)], axis=0)
            both = jnp.dot(qdr_ref[rows, pair], kdiag, preferred_element_type=F32)
            att += [both[:, :BLOCK], both[:, BLOCK:]]
        return scores, att

    def softmax_and_decay(c, stage1, state):
        scores, att = stage1
        probs, inv_den = [], []
        for kv in range(ATT_KV_HEADS):
            sc = scores[kv]
            prev = sc[:BLOCK]
            if c == 0:
                prev = prev + jnp.where(first, NEG_INF, 0.0).astype(F32)
            f = band_select(prev, sc[BLOCK:])
            sink = sink_ref[kv]
            m = jnp.maximum(jnp.max(f, axis=0, keepdims=True), sink)
            e = jnp.exp2(f - m)
            inv_den.append(1.0 / (jnp.sum(e, axis=0, keepdims=True) + jnp.exp2(sink - m)))
            probs.append(jnp.concatenate([band_select(e, 0.0), band_select(0.0, e)], axis=0).astype(BF16))
        att_b = [(att[h] * tril_ref[h]).astype(BF16) for h in range(RET_HEADS)]
        state_b = [state[h].astype(BF16) for h in range(RET_HEADS)]
        return probs, inv_den, att_b, state_b

    def second_matmuls(c, ops):
        probs, inv_den, att_b, state_b = ops
        rows = rows_of(c)
        vt_prev, vt_cur = vext_ref[c], vext_ref[c + 1]
        outs = []
        for kv in range(ATT_KV_HEADS):
            hs = slice(kv * ATT_HEAD_DIM, (kv + 1) * ATT_HEAD_DIM)
            vt = jnp.concatenate([vt_prev[hs], vt_cur[hs]], axis=1)
            outs.append(jnp.dot(vt, probs[kv], preferred_element_type=F32) * inv_den[kv])
        zeros_k = jnp.zeros((RET_KEY_DIM, RET_KEY_DIM), BF16)
        both = [jnp.dot(jnp.concatenate([jnp.concatenate([att_b[h], qdr_ref[rows, ksl[h]]], axis=1),
                                         jnp.concatenate([kdt_ref[c, ksl[h], :], zeros_k], axis=1)], axis=0),
                        jnp.concatenate([vr_ref[rows, vsl[h]], state_b[h]], axis=0), preferred_element_type=F32)
                for h in range(RET_HEADS)]
        ret = [bh[:BLOCK] for bh in both]
        upd = [bh[BLOCK:] for bh in both]
        return outs, ret, upd

    def finish(c, stage2):
        outs, ret, _ = stage2
        rows = rows_of(c)
        ot = jnp.concatenate(outs, axis=0)
        for g in range(ATT_GROUP):
            att_ref[ws, rows, g * LANES:(g + 1) * LANES] = ot[:, g * BLOCK:(g + 1) * BLOCK].T.astype(BF16)
        for h in range(RET_HEADS):
            out = ret[h]
            mu = jnp.mean(out, axis=-1, keepdims=True)
            dev = out - mu
            var = jnp.mean(dev * dev, axis=-1, keepdims=True)
            yn = dev * lax.rsqrt(var + GN_EPS) * gn_ref[:, vsl[h]]
            ret_ref[ws, rows, vsl[h]] = yn.astype(BF16) * sg_ref[rows, vsl[h]]

    def chunk_tail(c, stage1, state):
        stage2 = second_matmuls(c, softmax_and_decay(c, stage1, state))
        finish(c, stage2)
        return [state[h] * cdec_ref[h] + stage2[2][h] for h in range(RET_HEADS)]

    half = D_MODEL // 2
    cols = [slice(0, half), slice(half, D_MODEL)]

    assert nblk == 4

    def merged_half(n):
        ya = jnp.dot(att_ref[rs], watt_ref[:, cols[n]], preferred_element_type=F32)
        yr = jnp.dot(ret_ref[rs], wret_ref[:, cols[n]], preferred_element_type=F32)
        merged_ref[:, cols[n]] = ga_ref[:, cols[n]] * ya.astype(BF16) + gb_ref[:, cols[n]] * yr.astype(BF16)

    def out_half(n):
        h_ref[:, cols[n]] = x_ref[:, cols[n]] + jnp.dot(merged_ref[...], wout_ref[:, cols[n]],
                                                        preferred_element_type=F32)

    def run(chunks, projections):
        state = [state_ref[h] for h in range(RET_HEADS)]
        first_of = lambda c: first_matmuls(c) if chunks else None
        tail = lambda c, stage1, state: chunk_tail(c, stage1, state) if chunks else state
        s0 = first_of(0)
        s1 = first_of(1)
        if projections:
            merged_half(0)
        state = tail(0, s0, state)
        s2 = first_of(2)
        if projections:
            merged_half(1)
        state = tail(1, s1, state)
        s3 = first_of(3)
        state = tail(2, s2, state)
        if projections:
            out_half(0)
        state = tail(3, s3, state)
        if projections:
            out_half(1)
        if chunks:
            for h in range(RET_HEADS):
                state_ref[h] = state[h]
            kext_ref[:BLOCK, :] = kext_ref[nblk * BLOCK:, :]
            vext_ref[0] = vext_ref[nblk]

    pl.when(t == 0)(lambda: run(True, False))
    pl.when((t > 0) & (t < ntiles))(lambda: run(True, True))
    pl.when(t == ntiles)(lambda: run(False, True))


def _mix_call(proj_out, x, cdec, sinks, gn_gain, tril, w_att_up, w_ret_up, w_out, w_ff1, w_ff2):
    b, s, d = x.shape
    rows = MIX_ROWS
    nblk = rows // BLOCK
    tps = s // rows
    ntiles = b * tps
    grid = (ntiles + 1,)

    def weight_slab(w):
        slab = _slab_rows(w.shape[0], ntiles)
        return pl.BlockSpec((slab, w.shape[1]), lambda t: (jnp.minimum(t, w.shape[0] // slab - 1), 0))

    chunk_tile = lambda t: jnp.minimum(t, ntiles - 1)
    proj_tile = lambda t: jnp.maximum(t - 1, 0)

    def tok(width, tile):
        return pl.BlockSpec((None, rows, width), lambda t: (tile(t) // tps, tile(t) % tps, 0))

    def tok_t(feat):
        return pl.BlockSpec((None, nblk, feat, BLOCK), lambda t: (chunk_tile(t) // tps, chunk_tile(t) % tps, 0, 0))

    def const(shape):
        return pl.BlockSpec(shape, lambda t: (0,) * len(shape), pipeline_mode=pl.Buffered(1))

    smem = pl.BlockSpec(memory_space=pltpu.SMEM)
    ctok = lambda width: tok(width, chunk_tile)
    return pl.pallas_call(
        functools.partial(_mix_kernel, tiles_per_seq=tps, ntiles=ntiles),
        grid=grid,
        in_specs=[smem,
                  tok_t(ATT_Q_W), ctok(ATT_KV_W), tok_t(ATT_KV_W), ctok(RET_QK_W),
                  tok_t(RET_QK_W), ctok(RET_V_W), ctok(RET_V_W),
                  tok(D_MODEL, proj_tile), tok(D_MODEL, proj_tile), tok(d, proj_tile),
                  const((ATT_KV_HEADS, 1, ATT_GROUP * BLOCK)), const((1, RET_V_W)),
                  const((RET_HEADS, RET_CHUNK, RET_CHUNK)),
                  const((ATT_Q_W, d)), const((RET_V_W, d)), const((d, d)),
                  weight_slab(w_ff1), weight_slab(w_ff2)],
        out_specs=[tok(d, proj_tile), weight_slab(w_ff1), weight_slab(w_ff2)],
        out_shape=[jax.ShapeDtypeStruct((b, s, d), F32),
                   jax.ShapeDtypeStruct(w_ff1.shape, BF16), jax.ShapeDtypeStruct(w_ff2.shape, BF16)],
        scratch_shapes=[
            pltpu.VMEM((BLOCK + rows, ATT_KV_W), BF16),
            pltpu.VMEM((1 + nblk, ATT_KV_W, BLOCK), BF16),
            pltpu.VMEM((RET_HEADS, RET_KEY_DIM, RET_VAL_DIM), F32),
            pltpu.VMEM((2, rows, ATT_Q_W), BF16),
            pltpu.VMEM((2, rows, RET_V_W), BF16),
            pltpu.VMEM((rows, d), BF16),
        ],
        compiler_params=pltpu.CompilerParams(
            dimension_semantics=("arbitrary",), vmem_limit_bytes=V7X_VMEM_LIMIT_BYTES),
        name="mix",
    )(cdec, *proj_out, x, sinks, gn_gain, tril, w_att_up, w_ret_up, w_out, w_ff1, w_ff2)


def _ffn_kernel(h_ref, gain_ref, w1_ref, w2_ref, fgain_ref, o_ref, *, final_norm):
    nchunks = D_FF // FFN_COLS
    assert h_ref.shape[0] == 2 * FFN_SUB_ROWS and nchunks > FFN_SKEW
    rows_of = lambda s: slice(s * FFN_SUB_ROWS, (s + 1) * FFN_SUB_ROWS)

    def start(s):
        h = h_ref[rows_of(s), :]
        return (h, _inv_rms(h)), (h * gain_ref[...]).astype(BF16), jnp.zeros(h.shape, F32)

    def chunk(xb, acc, c):
        sl = slice(c * FFN_COLS, (c + 1) * FFN_COLS)
        a = jnp.maximum(jnp.dot(xb, w1_ref[:, sl], preferred_element_type=F32), 0.0)
        return acc + jnp.dot((a * a).astype(BF16), w2_ref[sl, :], preferred_element_type=F32)

    def finish(s, h_inv, acc):
        h, inv = h_inv
        y = h + acc * (inv * inv)
        if final_norm:
            y = _rms_scale(y, fgain_ref[...])
        o_ref[rows_of(s), :] = y

    h0, xb0, acc0 = start(0)
    for c in range(nchunks - FFN_SKEW):
        acc0 = chunk(xb0, acc0, c)
    h1, xb1, acc1 = start(1)
    for k in range(FFN_SKEW):
        acc1 = chunk(xb1, acc1, k)
        acc0 = chunk(xb0, acc0, nchunks - FFN_SKEW + k)
    for c in range(FFN_SKEW, nchunks):
        acc1 = chunk(xb1, acc1, c)
        if c == FFN_SKEW + 1:
            finish(0, h0, acc0)
    finish(1, h1, acc1)


def _ffn_call(h, gain, w1, w2, fgain, final_norm):
    b, s, d = h.shape
    rows = FFN_ROWS
    grid = (b, s // rows)

    def const(shape):
        return pl.BlockSpec(shape, lambda i, j: (0,) * len(shape), pipeline_mode=pl.Buffered(1))

    tok = pl.BlockSpec((None, rows, d), lambda i, j: (i, j, 0))
    return pl.pallas_call(
        functools.partial(_ffn_kernel, final_norm=final_norm),
        grid=grid,
        in_specs=[tok, const((1, d)), const((d, D_FF)), const((D_FF, d)), const((1, d))],
        out_specs=tok,
        out_shape=jax.ShapeDtypeStruct((b, s, d), F32),
        compiler_params=pltpu.CompilerParams(
            dimension_semantics=("arbitrary", "arbitrary"), vmem_limit_bytes=V7X_VMEM_LIMIT_BYTES),
        name="ffn",
    )(h, gain, w1, w2, fgain)


def _decay_terms():
    h, c = RET_HEADS, RET_CHUNK
    f32 = np.float32
    log_gamma = np.log1p(-np.exp2(f32(-5.0) - np.arange(h, dtype=f32)))
    idx = np.arange(c, dtype=f32)
    diff = idx[:, None] - idx[None, :]
    tril = np.where(diff >= 0, np.exp(f32(-c) * log_gamma)[:, None, None], f32(0.0)).astype(f32)
    q_decay = np.exp((idx + f32(1.0))[None, :] * log_gamma[:, None])
    k_decay = np.exp((f32(c - 1.0) - idx)[None, :] * log_gamma[:, None])
    chunk_decay = np.exp(f32(c) * log_gamma)
    return tril, q_decay, k_decay, chunk_decay


def _position_tables(seq_len, q_decay, k_decay):
    f32 = np.float32
    pos = np.arange(seq_len, dtype=f32)

    def cos_sin(dim, theta):
        inv_freq = f32(theta) ** (-np.arange(0, dim, 2, dtype=f32) / f32(dim))
        ang = pos[:, None] * inv_freq[None, :]
        return np.cos(ang), np.sin(ang)

    cos, sin = cos_sin(ROPE_DIM, ROPE_THETA)
    pad = ATT_HEAD_DIM - ROPE_DIM
    zeros_half = np.zeros_like(sin)
    one_head = lambda parts: np.tile(np.concatenate(parts, axis=-1), (1, LANES // ATT_HEAD_DIM))
    ca = one_head([cos, cos, np.ones((seq_len, pad), f32)])
    s1 = one_head([-sin, zeros_half, np.zeros((seq_len, pad), f32)])
    s2 = one_head([zeros_half, sin, np.zeros((seq_len, pad), f32)])
    cos_r, sin_r = cos_sin(RET_KEY_DIM, RET_ROT_BASE)
    cr = np.concatenate([cos_r, cos_r], axis=-1)
    sr = np.concatenate([-sin_r, sin_r], axis=-1)

    def per_row(decay):
        return np.repeat(decay.T, RET_KEY_DIM, axis=1)

    key_scale = f32(RET_KEY_DIM ** -0.5)
    tables = (cos.T, sin.T, ca, s1, s2, cr, sr, per_row(q_decay), per_row(k_decay * key_scale))
    return tuple(np.ascontiguousarray(t, dtype=f32) for t in tables)


def kernel(x, norm_mix_gain, w_in, b_gates, attn_sinks, ret_gn_gain, w_att_up, w_ret_up, w_out,
           norm_mlp_gain, w_ff1, w_ff2, norm_final_gain):
    b, s, d = x.shape
    depth = w_in.shape[0]
    assert d == D_MODEL and s % MIX_ROWS == 0 and s % PROJ_ROWS == 0 and s % FFN_ROWS == 0
    tril, q_decay, k_decay, cdec = _decay_terms()
    tables = _position_tables(s, q_decay, k_decay)
    row = lambda v: v.reshape(1, -1).astype(F32)
    h = x
    for l in range(depth):
        w = (w_in[l] * norm_mix_gain[l].astype(F32)[:, None]).astype(BF16)
        *proj_out, w_att, w_ret, w_o = _proj_call(h, w, row(b_gates[l]), tables, w_att_up[l], w_ret_up[l], w_out[l])
        sinks = jnp.repeat((attn_sinks[l].astype(F32) * LOG2E).reshape(ATT_KV_HEADS, 1, ATT_GROUP), BLOCK, axis=-1)
        h, w1, w2 = _mix_call(proj_out, h, cdec, sinks, row(ret_gn_gain[l]), tril, w_att, w_ret, w_o,
                              w_ff1[l], w_ff2[l])
        h = _ffn_call(h, row(norm_mlp_gain[l]), w1, w2, row(norm_final_gain), final_norm=(l == depth - 1))
    return h
```
